```python
import jax, jax.numpy as jnp
from jax import lax
import numpy as np

D_MODEL = 2048
BATCH = 2
SEQ = 4096
DEPTH = 1

CHUNK = 64
N_META = 16
ATT_HEADS = 16
ATT_HEAD_DIM = 128
ATT_WIDTH = ATT_HEADS * ATT_HEAD_DIM
CONV_WIDTH = D_MODEL
CONV_GROUPS = 16
CONV_K = 3
Q_BLOCK = 128
EPS = 1e-6
_IN_SIZES = (ATT_WIDTH, ATT_WIDTH, ATT_WIDTH, ATT_HEADS, ATT_WIDTH,
             CONV_WIDTH, CONV_WIDTH, CONV_WIDTH, CONV_WIDTH, D_MODEL, D_MODEL)
N_IN = 4 * ATT_WIDTH + ATT_HEADS + 4 * CONV_WIDTH + 2 * D_MODEL

kernel_name = "fox_shortconv_gated_hybrid_block"


def _split_points():
    pts, acc = [], 0
    for s in _IN_SIZES[:-1]:
        acc += s
        pts.append(acc)
    return pts


def rmsnorm(x, g):
    xf = x.astype(jnp.float32)
    y = xf * lax.rsqrt(jnp.mean(xf * xf, axis=-1, keepdims=True) + EPS)
    return (y * g.astype(jnp.float32)).astype(x.dtype)


def forgetting_attention(q, k, v, log_f):
    b, l, h, dh = q.shape
    nb = l // Q_BLOCK
    c = jnp.cumsum(log_f, axis=1).transpose(0, 2, 1)
    scale = dh ** -0.5
    kpos = jnp.arange(l)
    q_blocks = q.reshape(b, nb, Q_BLOCK, h, dh).transpose(1, 0, 3, 2, 4)
    c_blocks = c.reshape(b, h, nb, Q_BLOCK).transpose(2, 0, 1, 3)
    starts = jnp.arange(nb) * Q_BLOCK

    def one_block(args):
        q_blk, c_blk, start = args
        s = jnp.einsum('bhqd,bkhd->bhqk', q_blk, k).astype(jnp.float32) * scale
        s = s + (c_blk[..., None] - c[:, :, None, :])
        qpos = start + jnp.arange(Q_BLOCK)
        causal = kpos[None, :] <= qpos[:, None]
        p = jax.nn.softmax(jnp.where(causal, s, -jnp.inf), axis=-1)
        return jnp.einsum('bhqk,bkhd->bqhd', p.astype(v.dtype), v)

    out = lax.map(one_block, (q_blocks, c_blocks, starts))
    return out.transpose(1, 0, 2, 3, 4).reshape(b, l, h * dh)


def short_conv(u, w):
    l = u.shape[1]
    up = jnp.pad(u, ((0, 0), (CONV_K - 1, 0), (0, 0)))
    out = up[:, 0:l] * w[0]
    for j in range(1, CONV_K):
        out = out + up[:, j:j + l] * w[j]
    return out


def hybrid_layer(x, norm_g, w_in, b_f, conv_w, w_att_o, w_conv_o, w_out):
    b, l, _ = x.shape
    h = rmsnorm(x, norm_g)
    proj = jnp.einsum('bld,dn->bln', h, w_in)
    (q, k, v, f_logit, z_att, u, gate_b, gate_c, z_conv,
     m_att, m_conv) = jnp.split(proj, _split_points(), axis=-1)
    heads = lambda t: t.reshape(b, l, ATT_HEADS, ATT_HEAD_DIM)
    log_f = jax.nn.log_sigmoid((f_logit + b_f).astype(jnp.float32))
    o_att = forgetting_attention(heads(q), heads(k), heads(v), log_f)
    y_att = jnp.einsum('blc,cd->bld', o_att * jax.nn.silu(z_att), w_att_o)
    o_conv = gate_b * short_conv(gate_c * u, conv_w)
    y_conv = jnp.einsum('blc,cd->bld', o_conv * jax.nn.silu(z_conv), w_conv_o)
    merged = jax.nn.sigmoid(m_att) * y_att + jax.nn.sigmoid(m_conv) * y_conv
    return x + jnp.einsum('bld,de->ble', merged, w_out)


def setup_inputs(seed: int = 0) -> dict:
    key = jax.random.key(seed)
    ks = jax.random.split(key, 11)
    f32 = jnp.float32
    x = jax.random.normal(ks[0], (BATCH, SEQ, D_MODEL), f32)
    meta_tokens = jax.random.normal(ks[1], (N_META, D_MODEL), f32)
    norm_gain = 1.0 + 0.01 * jax.random.normal(ks[2], (DEPTH, D_MODEL), f32)
    w_in = jax.random.normal(ks[3], (DEPTH, D_MODEL, N_IN), f32) * D_MODEL ** -0.5
    b_f = (jnp.linspace(1.0, 6.0, ATT_HEADS, dtype=f32)[None, :]
           + 0.1 * jax.random.normal(ks[4], (DEPTH, ATT_HEADS), f32))
    conv_w = jax.random.normal(ks[5], (DEPTH, CONV_K, CONV_WIDTH), f32) * CONV_K ** -0.5
    w_att_o = jax.random.normal(ks[6], (DEPTH, ATT_WIDTH, D_MODEL), f32) * ATT_WIDTH ** -0.5
    w_conv_o = jax.random.normal(ks[7], (DEPTH, CONV_WIDTH, D_MODEL), f32) * CONV_WIDTH ** -0.5
    w_out = jax.random.normal(ks[8], (DEPTH, D_MODEL, D_MODEL), f32) * D_MODEL ** -0.5
    final_gain = 1.0 + 0.01 * jax.random.normal(ks[9], (D_MODEL,), f32)
    return {"x": x, "meta_tokens": meta_tokens, "norm_gain": norm_gain, "w_in": w_in,
            "b_f": b_f, "conv_w": conv_w, "w_att_o": w_att_o, "w_conv_o": w_conv_o,
            "w_out": w_out, "final_gain": final_gain}


def reference(x, meta_tokens, norm_gain, w_in, b_f, conv_w, w_att_o, w_conv_o, w_out, final_gain):
    b = x.shape[0]
    l_real = N_META + x.shape[1]
    l_pad = -(-l_real // Q_BLOCK) * Q_BLOCK
    meta = jnp.broadcast_to(meta_tokens.astype(x.dtype)[None], (b, N_META, D_MODEL))
    hdn = jnp.concatenate([meta, x], axis=1)
    hdn = jnp.pad(hdn, ((0, 0), (0, l_pad - l_real), (0, 0)))
    for i in range(DEPTH):
        hdn = hybrid_layer(hdn, norm_gain[i], w_in[i], b_f[i], conv_w[i],
                           w_att_o[i], w_conv_o[i], w_out[i])
    hdn = rmsnorm(hdn, final_gain)
    return hdn[:, N_META:l_real]
```

```python
import functools

import jax
import jax.numpy as jnp
from jax import lax
from jax.experimental import pallas as pl
from jax.experimental.pallas import tpu as pltpu

D_MODEL = 2048
N_META = 16
HEADS = 16
HEAD_DIM = 128
CONV_K = 3
EPS = 1e-6
LANES = 128
SUBLANES = 8
NEG_BIG = -1e30
VMEM_LIMIT = 56 * 1024 * 1024

F32 = jnp.float32
BF16 = jnp.bfloat16


def _dot(a, b):
    return jnp.dot(a, b, preferred_element_type=F32)


def _dot_nt(a, b):
    return lax.dot_general(a, b, (((1,), (1,)), ((), ())), preferred_element_type=F32)


def _rmsnorm(x, gain):
    return x * lax.rsqrt(jnp.mean(x * x, axis=-1, keepdims=True) + EPS) * gain


def _log_sigmoid(x):
    return jnp.minimum(x, 0.0) - jnp.log1p(jnp.exp(-jnp.abs(x)))


def _silu(x):
    return x * jax.nn.sigmoid(x)


def _cumsum_rows(x):
    n = x.shape[0]
    tri = (lax.broadcasted_iota(jnp.int32, (n, n), 0)
           >= lax.broadcasted_iota(jnp.int32, (n, n), 1)).astype(BF16)
    hi = x.astype(BF16)
    r1 = x - hi.astype(F32)
    mid = r1.astype(BF16)
    lo = (r1 - mid.astype(F32)).astype(BF16)
    return _dot(tri, hi) + _dot(tri, mid) + _dot(tri, lo)


def _meta_kernel(meta_ref, gain_ref, wk_ref, wv_ref, wu_ref, wgc_ref, wf_ref, bf_ref,
                 k_ref, v_ref, g_ref, cm_ref):
    h = _rmsnorm(meta_ref[...], gain_ref[...]).astype(BF16)
    k_ref[...] = _dot(h, wk_ref[...]).astype(BF16)
    v_ref[...] = _dot(h, wv_ref[...]).astype(BF16)
    g_ref[...] = _dot(h, wgc_ref[...]) * _dot(h, wu_ref[...])
    cm_ref[...] = _cumsum_rows(_log_sigmoid(_dot(h, wf_ref[...]) + bf_ref[...]))


def _meta_call(meta, gain, w_a, w_b, wf, bf, tn=512):
    nt = D_MODEL // tn
    wspec = lambda g: pl.BlockSpec((None, D_MODEL, tn), lambda j, g=g: (g, 0, j))
    return pl.pallas_call(
        _meta_kernel,
        grid=(nt,),
        in_specs=[
            pl.BlockSpec((N_META, D_MODEL), lambda j: (0, 0)),
            pl.BlockSpec((1, D_MODEL), lambda j: (0, 0)),
            wspec(1), wspec(2),
            wspec(0), wspec(2),
            pl.BlockSpec((D_MODEL, LANES), lambda j: (0, 0)),
            pl.BlockSpec((1, LANES), lambda j: (0, 0)),
        ],
        out_specs=[
            pl.BlockSpec((N_META, tn), lambda j: (0, j)),
            pl.BlockSpec((N_META, tn), lambda j: (0, j)),
            pl.BlockSpec((N_META, tn), lambda j: (0, j)),
            pl.BlockSpec((N_META, LANES), lambda j: (0, 0)),
        ],
        out_shape=[
            jax.ShapeDtypeStruct((N_META, D_MODEL), BF16),
            jax.ShapeDtypeStruct((N_META, D_MODEL), BF16),
            jax.ShapeDtypeStruct((N_META, D_MODEL), F32),
            jax.ShapeDtypeStruct((N_META, LANES), F32),
        ],
        compiler_params=pltpu.CompilerParams(
            dimension_semantics=("arbitrary",), vmem_limit_bytes=VMEM_LIMIT),
        name="meta_proj",
    )(meta, gain, w_a, w_a, w_b, w_b, wf, bf)


def _norm_kernel(x_ref, gain_ref, wf_ref, bf_ref, cm_ref,
                 h_ref, ccol_ref, crow_ref, carry_ref, *, chunk):
    @pl.when(pl.program_id(1) == 0)
    def _():
        carry_ref[...] = cm_ref[N_META - 1:N_META, :]

    h = _rmsnorm(x_ref[...], gain_ref[...]).astype(BF16)
    h_ref[...] = h
    logf = _log_sigmoid(_dot(h, wf_ref[...]) + bf_ref[...])
    tm = logf.shape[0]
    run = carry_ref[...]
    parts = []
    for j in range(tm // chunk):
        c = _cumsum_rows(logf[j * chunk:(j + 1) * chunk]) + run
        run = c[chunk - 1:chunk, :]
        parts.append(c)
    carry_ref[...] = run
    c = jnp.concatenate(parts, axis=0)
    ccol_ref[...] = c[:, :HEADS]
    crow_ref[...] = c.T[:HEADS, :]


def _norm_call(x, gain, wf, bf, cm, tm=512, chunk=256):
    b, s, _ = x.shape
    nt = s // tm
    return pl.pallas_call(
        functools.partial(_norm_kernel, chunk=chunk),
        grid=(b, nt),
        in_specs=[
            pl.BlockSpec((None, tm, D_MODEL), lambda bi, i: (bi, i, 0)),
            pl.BlockSpec((1, D_MODEL), lambda bi, i: (0, 0)),
            pl.BlockSpec((D_MODEL, LANES), lambda bi, i: (0, 0)),
            pl.BlockSpec((1, LANES), lambda bi, i: (0, 0)),
            pl.BlockSpec((N_META, LANES), lambda bi, i: (0, 0)),
        ],
        out_specs=[
            pl.BlockSpec((tm, D_MODEL), lambda bi, i: (bi * nt + i, 0)),
            pl.BlockSpec((tm, HEADS), lambda bi, i: (bi * nt + i, 0)),
            pl.BlockSpec((None, HEADS, tm), lambda bi, i: (bi, 0, i)),
        ],
        out_shape=[
            jax.ShapeDtypeStruct((b * s, D_MODEL), BF16),
            jax.ShapeDtypeStruct((b * s, HEADS), F32),
            jax.ShapeDtypeStruct((b, HEADS, s), F32),
        ],
        scratch_shapes=[pltpu.VMEM((1, LANES), F32)],
        compiler_params=pltpu.CompilerParams(
            dimension_semantics=("arbitrary", "arbitrary"), vmem_limit_bytes=VMEM_LIMIT),
        name="norm_forget",
    )(x, gain, wf, bf, cm)


def _proj_kernel(h_ref, w_ref, o_ref):
    g = pl.program_id(1)
    y = _dot(h_ref[...], w_ref[...])

    @pl.when(g == 0)
    def _():
        o_ref[...] = (y * (HEAD_DIM ** -0.5)).astype(BF16)

    @pl.when((g == 1) | (g == 2))
    def _():
        o_ref[...] = y.astype(BF16)

    @pl.when(g == 3)
    def _():
        o_ref[...] = _silu(y).astype(BF16)

    @pl.when(g >= 4)
    def _():
        o_ref[...] = jax.nn.sigmoid(y).astype(BF16)


def _proj_call(h, w_a, tm=1024, tn=512):
    r = h.shape[0]
    ng = w_a.shape[0]
    return pl.pallas_call(
        _proj_kernel,
        grid=(r // tm, ng, D_MODEL // tn),
        in_specs=[
            pl.BlockSpec((tm, D_MODEL), lambda i, g, j: (i, 0)),
            pl.BlockSpec((None, D_MODEL, tn), lambda i, g, j: (g, 0, j)),
        ],
        out_specs=pl.BlockSpec((None, tm, tn), lambda i, g, j: (g, i, j)),
        out_shape=jax.ShapeDtypeStruct((ng, r, D_MODEL), BF16),
        compiler_params=pltpu.CompilerParams(
            dimension_semantics=("arbitrary", "arbitrary", "arbitrary"),
            vmem_limit_bytes=VMEM_LIMIT),
        name="plain_proj",
    )(h, w_a)


def _conv_kernel(h_ref, w_ref, gm_ref, cw_ref, o_ref, gbuf_ref, *, tiles_per_batch):
    i = pl.program_id(1)
    tm = h_ref.shape[0]

    @pl.when(i % tiles_per_batch == 0)
    def _():
        gbuf_ref[0:SUBLANES, :] = gm_ref[N_META - SUBLANES:N_META, :]

    h = h_ref[...]
    u = _dot(h, w_ref[0])
    gc = _dot(h, w_ref[2])
    g = gc * u
    gbuf_ref[SUBLANES:SUBLANES + tm, :] = g
    conv = (gbuf_ref[SUBLANES - 2:SUBLANES - 2 + tm, :] * cw_ref[0:1, :]
            + gbuf_ref[SUBLANES - 1:SUBLANES - 1 + tm, :] * cw_ref[1:2, :]
            + g * cw_ref[2:3, :])
    gbuf_ref[0:SUBLANES, :] = g[tm - SUBLANES:tm, :]
    gb = _dot(h, w_ref[1])
    zc = _dot(h, w_ref[3])
    o_ref[...] = (gb * conv * _silu(zc)).astype(BF16)


def _conv_call(h, w_b, g_meta, conv_w, rows_per_batch, tm=1024, tc=256):
    r = h.shape[0]
    return pl.pallas_call(
        functools.partial(_conv_kernel, tiles_per_batch=rows_per_batch // tm),
        grid=(D_MODEL // tc, r // tm),
        in_specs=[
            pl.BlockSpec((tm, D_MODEL), lambda j, i: (i, 0)),
            pl.BlockSpec((4, D_MODEL, tc), lambda j, i: (0, 0, j)),
            pl.BlockSpec((N_META, tc), lambda j, i: (0, j)),
            pl.BlockSpec((CONV_K, tc), lambda j, i: (0, j)),
        ],
        out_specs=pl.BlockSpec((tm, tc), lambda j, i: (i, j)),
        out_shape=jax.ShapeDtypeStruct((r, D_MODEL), BF16),
        scratch_shapes=[pltpu.VMEM((SUBLANES + tm, tc), F32)],
        compiler_params=pltpu.CompilerParams(
            dimension_semantics=("arbitrary", "arbitrary"), vmem_limit_bytes=VMEM_LIMIT),
        name="conv_proj",
    )(h, w_b, g_meta, conv_w)


def _attn_kernel(q_ref, k_ref, v_ref, sz_ref, km_ref, vm_ref, ccol_ref, crow_ref, cmrow_ref,
                 o_ref, m_ref, l_ref, acc_ref):
    hd = pl.program_id(1)
    i = pl.program_id(2)
    tq = q_ref.shape[0]
    q = q_ref[...]
    lane = lax.broadcasted_iota(jnp.int32, (tq, HEADS), 1)
    ct = jnp.sum(jnp.where(lane == hd, ccol_ref[...], 0.0), axis=1, keepdims=True)

    s = _dot_nt(q, km_ref[...]) + (ct - cmrow_ref[...])
    m0 = jnp.max(s, axis=1, keepdims=True)
    p = jnp.exp(s - m0)
    m_ref[...] = m0
    l_ref[...] = jnp.sum(p, axis=1, keepdims=True)
    acc_ref[...] = _dot(p.astype(BF16), vm_ref[...])

    def step(j, masked):
        off = pl.multiple_of(j * tq, tq)
        k = k_ref[pl.ds(off, tq), :]
        v = v_ref[pl.ds(off, tq), :]
        s = _dot_nt(q, k) + (ct - crow_ref[:, pl.ds(off, tq)])
        if masked:
            row = lax.broadcasted_iota(jnp.int32, (tq, tq), 0)
            col = lax.broadcasted_iota(jnp.int32, (tq, tq), 1)
            s = jnp.where(col <= row, s, NEG_BIG)
        m_old = m_ref[...]
        m_new = jnp.maximum(m_old, jnp.max(s, axis=1, keepdims=True))
        alpha = jnp.exp(m_old - m_new)
        p = jnp.exp(s - m_new)
        l_ref[...] = alpha * l_ref[...] + jnp.sum(p, axis=1, keepdims=True)
        acc_ref[...] = alpha * acc_ref[...] + _dot(p.astype(BF16), v)
        m_ref[...] = m_new

    def body(j, carry):
        step(j, False)
        return carry

    lax.fori_loop(0, i, body, 0)
    step(i, True)
    o = acc_ref[...] / l_ref[...]
    o_ref[...] = (o * sz_ref[...].astype(F32)).astype(BF16)


def _attn_call(proj, k_meta, v_meta, c_col, c_row, cm_row, batch, seq, tq=512):
    nq = seq // tq
    r = batch * seq
    return pl.pallas_call(
        _attn_kernel,
        grid=(batch, HEADS, nq),
        in_specs=[
            pl.BlockSpec((None, tq, HEAD_DIM), lambda b, h, i: (0, b * nq + i, h)),
            pl.BlockSpec((None, seq, HEAD_DIM), lambda b, h, i: (1, b, h)),
            pl.BlockSpec((None, seq, HEAD_DIM), lambda b, h, i: (2, b, h)),
            pl.BlockSpec((None, tq, HEAD_DIM), lambda b, h, i: (3, b * nq + i, h)),
            pl.BlockSpec((N_META, HEAD_DIM), lambda b, h, i: (0, h)),
            pl.BlockSpec((N_META, HEAD_DIM), lambda b, h, i: (0, h)),
            pl.BlockSpec((tq, HEADS), lambda b, h, i: (b * nq + i, 0)),
            pl.BlockSpec((None, 1, seq), lambda b, h, i: (b * HEADS + h, 0, 0)),
            pl.BlockSpec((None, 1, N_META), lambda b, h, i: (h, 0, 0)),
        ],
        out_specs=pl.BlockSpec((tq, HEAD_DIM), lambda b, h, i: (b * nq + i, h)),
        out_shape=jax.ShapeDtypeStruct((r, D_MODEL), BF16),
        scratch_shapes=[
            pltpu.VMEM((tq, 1), F32),
            pltpu.VMEM((tq, 1), F32),
            pltpu.VMEM((tq, HEAD_DIM), F32),
        ],
        compiler_params=pltpu.CompilerParams(
            dimension_semantics=("arbitrary", "arbitrary", "arbitrary"),
            vmem_limit_bytes=VMEM_LIMIT),
        name="fox_attention",
    )(proj, proj, proj, proj, k_meta, v_meta, c_col, c_row, cm_row)


def _out_kernel(aa_ref, ac_ref, sma_ref, smc_ref, x_ref, w_ref, gain_ref, o_ref):
    ya = _dot(aa_ref[...], w_ref[0])
    yc = _dot(ac_ref[...], w_ref[1])
    merged = sma_ref[...].astype(F32) * ya + smc_ref[...].astype(F32) * yc
    y = x_ref[...] + _dot(merged.astype(BF16), w_ref[2])
    o_ref[...] = _rmsnorm(y, gain_ref[...])


def _out_call(a_att, a_conv, proj, x2d, w_o, gain, tm=256):
    r = x2d.shape[0]
    row = lambda i: (i, 0)
    return pl.pallas_call(
        _out_kernel,
        grid=(r // tm,),
        in_specs=[
            pl.BlockSpec((tm, D_MODEL), row),
            pl.BlockSpec((tm, D_MODEL), row),
            pl.BlockSpec((None, tm, D_MODEL), lambda i: (4, i, 0)),
            pl.BlockSpec((None, tm, D_MODEL), lambda i: (5, i, 0)),
            pl.BlockSpec((tm, D_MODEL), row),
            pl.BlockSpec((3, D_MODEL, D_MODEL), lambda i: (0, 0, 0),
                         pipeline_mode=pl.Buffered(1)),
            pl.BlockSpec((1, D_MODEL), lambda i: (0, 0)),
        ],
        out_specs=pl.BlockSpec((tm, D_MODEL), row),
        out_shape=jax.ShapeDtypeStruct((r, D_MODEL), F32),
        compiler_params=pltpu.CompilerParams(
            dimension_semantics=("arbitrary",), vmem_limit_bytes=VMEM_LIMIT),
        name="out_proj",
    )(a_att, a_conv, proj, proj, x2d, w_o, gain)


def kernel(x, meta_tokens, norm_gain, w_in, b_f, conv_w, w_att_o, w_conv_o, w_out, final_gain):
    batch, seq, _ = x.shape
    assert norm_gain.shape[0] == 1, "single-layer block"
    w = w_in[0]
    sizes = (D_MODEL, D_MODEL, D_MODEL, HEADS, D_MODEL,
             D_MODEL, D_MODEL, D_MODEL, D_MODEL, D_MODEL, D_MODEL)
    offs = [0]
    for sz in sizes:
        offs.append(offs[-1] + sz)
    col = lambda n: w[:, offs[n]:offs[n + 1]].astype(BF16)
    w_a = jnp.stack([col(0), col(1), col(2), col(4), col(9), col(10)])
    w_b = jnp.stack([col(5), col(6), col(7), col(8)])
    wf = jnp.pad(col(3), ((0, 0), (0, LANES - HEADS)))
    bf = jnp.pad(b_f[0].astype(F32), (0, LANES - HEADS)).reshape(1, LANES)
    w_o = jnp.stack([w_att_o[0], w_conv_o[0], w_out[0]]).astype(BF16)
    gain = norm_gain[0].reshape(1, D_MODEL).astype(F32)
    fgain = final_gain.reshape(1, D_MODEL).astype(F32)

    k_meta, v_meta, g_meta, cm = _meta_call(meta_tokens.astype(F32), gain, w_a, w_b, wf, bf)
    h, c_col, c_row = _norm_call(x, gain, wf, bf, cm)
    proj = _proj_call(h, w_a)
    a_conv = _conv_call(h, w_b, g_meta, conv_w[0].astype(F32), seq)
    cm_row = cm.T[:HEADS].reshape(HEADS, 1, N_META)
    a_att = _attn_call(proj, k_meta, v_meta, c_col,
                       c_row.reshape(batch * HEADS, 1, seq), cm_row, batch, seq)
    out = _out_call(a_att, a_conv, proj, x.reshape(batch * seq, D_MODEL), w_o, fgain)
    return out.reshape(batch, seq, D_MODEL)
```

```python
import functools

import numpy as np

import jax
import jax.numpy as jnp
from jax import lax
from jax.experimental import pallas as pl
from jax.experimental.pallas import tpu as pltpu

D_MODEL = 2048
N_META = 16
HEADS = 16
HEAD_DIM = 128
CONV_K = 3
EPS = 1e-6
LANES = 128
SUBLANES = 8
NEG_BIG = -1e30
LOG2E = 1.4426950408889634
GATE_LANES = LANES // HEADS
VMEM_LIMIT = 56 * 1024 * 1024

F32 = jnp.float32
BF16 = jnp.bfloat16


def _dot(a, b):
    return jnp.dot(a, b, preferred_element_type=F32)


def _dot_nt(a, b):
    return lax.dot_general(a, b, (((1,), (1,)), ((), ())), preferred_element_type=F32)


def _rmsnorm(x, gain):
    return x * lax.rsqrt(jnp.mean(x * x, axis=-1, keepdims=True) + EPS) * gain


def _log_sigmoid(x):
    return jnp.minimum(x, 0.0) - jnp.log1p(jnp.exp(-jnp.abs(x)))


def _silu(x):
    return x * jax.nn.sigmoid(x)


def _split3(x):
    hi = x.astype(BF16)
    r1 = x - hi.astype(F32)
    mid = r1.astype(BF16)
    lo = (r1 - mid.astype(F32)).astype(BF16)
    return hi, mid, lo


def _cumsum_rows(x):
    n = x.shape[0]
    tri = (lax.broadcasted_iota(jnp.int32, (n, n), 0)
           >= lax.broadcasted_iota(jnp.int32, (n, n), 1)).astype(BF16)
    hi, mid, lo = _split3(x)
    return _dot(tri, hi) + _dot(tri, mid) + _dot(tri, lo)


def _gate_features(c, place_ref, ones_ref):
    hi, mid, lo = _split3(c * LOG2E)
    cat = jnp.concatenate([hi, mid, lo], axis=1)
    return (_dot(cat, place_ref[...]) + ones_ref[...]).astype(BF16)


def _feature_constants():
    place = np.zeros((3 * LANES, LANES), np.float32)
    ones = np.zeros((1, LANES), np.float32)
    eq = np.zeros((HEADS, LANES, LANES), np.float32)
    ek = np.zeros((HEADS, LANES, LANES), np.float32)
    for h in range(HEADS):
        base = GATE_LANES * h
        ones[0, base + 3] = 1.0
        for f in range(3):
            place[f * LANES + h, base + f] = 1.0
            eq[h, base + f, f] = 1.0
            eq[h, base + 3, 3 + f] = 1.0
            ek[h, base + 3, f] = 1.0
            ek[h, base + f, 3 + f] = -1.0
    return (jnp.asarray(place, BF16), jnp.asarray(ones, F32),
            jnp.asarray(eq, BF16), jnp.asarray(ek, BF16))


def _meta_kernel(meta_ref, gain_ref, wk_ref, wv_ref, wu_ref, wgc_ref, wf_ref, bf_ref,
                 place_ref, ones_ref, k_ref, v_ref, g_ref, cm_ref, gm_ref):
    h = _rmsnorm(meta_ref[...], gain_ref[...]).astype(BF16)
    k_ref[...] = _dot(h, wk_ref[...]).astype(BF16)
    v_ref[...] = _dot(h, wv_ref[...]).astype(BF16)
    g_ref[...] = _dot(h, wgc_ref[...]) * _dot(h, wu_ref[...])
    cm = _cumsum_rows(_log_sigmoid(_dot(h, wf_ref[...]) + bf_ref[...]))
    cm_ref[...] = cm
    gm_ref[...] = _gate_features(cm, place_ref, ones_ref)


def _meta_call(meta, gain, w_a, w_b, wf, bf, place, ones, tn=512):
    nt = D_MODEL // tn
    wspec = lambda g: pl.BlockSpec((None, D_MODEL, tn), lambda j, g=g: (g, 0, j))
    const = lambda shape: pl.BlockSpec(shape, lambda j: (0,) * len(shape))
    return pl.pallas_call(
        _meta_kernel,
        grid=(nt,),
        in_specs=[
            const((N_META, D_MODEL)),
            const((1, D_MODEL)),
            wspec(1), wspec(2),
            wspec(0), wspec(2),
            const((D_MODEL, LANES)),
            const((1, LANES)),
            const((3 * LANES, LANES)),
            const((1, LANES)),
        ],
        out_specs=[
            pl.BlockSpec((N_META, tn), lambda j: (0, j)),
            pl.BlockSpec((N_META, tn), lambda j: (0, j)),
            pl.BlockSpec((N_META, tn), lambda j: (0, j)),
            const((N_META, LANES)),
            const((N_META, LANES)),
        ],
        out_shape=[
            jax.ShapeDtypeStruct((N_META, D_MODEL), BF16),
            jax.ShapeDtypeStruct((N_META, D_MODEL), BF16),
            jax.ShapeDtypeStruct((N_META, D_MODEL), F32),
            jax.ShapeDtypeStruct((N_META, LANES), F32),
            jax.ShapeDtypeStruct((N_META, LANES), BF16),
        ],
        compiler_params=pltpu.CompilerParams(
            dimension_semantics=("arbitrary",), vmem_limit_bytes=VMEM_LIMIT),
        name="meta_proj",
    )(meta, gain, w_a, w_a, w_b, w_b, wf, bf, place, ones)


def _norm_kernel(x_ref, gain_ref, wf_ref, bf_ref, cm_ref, place_ref, ones_ref,
                 h_ref, gate_ref, carry_ref, *, chunk):
    @pl.when(pl.program_id(1) == 0)
    def _():
        carry_ref[...] = cm_ref[N_META - 1:N_META, :]

    h = _rmsnorm(x_ref[...], gain_ref[...]).astype(BF16)
    h_ref[...] = h
    logf = _log_sigmoid(_dot(h, wf_ref[...]) + bf_ref[...])
    tm = logf.shape[0]
    run = carry_ref[...]
    parts = []
    for j in range(tm // chunk):
        c = _cumsum_rows(logf[j * chunk:(j + 1) * chunk]) + run
        run = c[chunk - 1:chunk, :]
        parts.append(c)
    carry_ref[...] = run
    gate_ref[...] = _gate_features(jnp.concatenate(parts, axis=0), place_ref, ones_ref)


def _norm_call(x, gain, wf, bf, cm, place, ones, tm=512, chunk=256):
    b, s, _ = x.shape
    nt = s // tm
    const = lambda shape: pl.BlockSpec(shape, lambda bi, i: (0,) * len(shape))
    return pl.pallas_call(
        functools.partial(_norm_kernel, chunk=chunk),
        grid=(b, nt),
        in_specs=[
            pl.BlockSpec((None, tm, D_MODEL), lambda bi, i: (bi, i, 0)),
            const((1, D_MODEL)),
            const((D_MODEL, LANES)),
            const((1, LANES)),
            const((N_META, LANES)),
            const((3 * LANES, LANES)),
            const((1, LANES)),
        ],
        out_specs=[
            pl.BlockSpec((tm, D_MODEL), lambda bi, i: (bi * nt + i, 0)),
            pl.BlockSpec((tm, LANES), lambda bi, i: (bi * nt + i, 0)),
        ],
        out_shape=[
            jax.ShapeDtypeStruct((b * s, D_MODEL), BF16),
            jax.ShapeDtypeStruct((b * s, LANES), BF16),
        ],
        scratch_shapes=[pltpu.VMEM((1, LANES), F32)],
        compiler_params=pltpu.CompilerParams(
            dimension_semantics=("arbitrary", "arbitrary"), vmem_limit_bytes=VMEM_LIMIT),
        name="norm_forget",
    )(x, gain, wf, bf, cm, place, ones)


def _proj_kernel(h_ref, w_ref, o_ref):
    g = pl.program_id(1)
    y = _dot(h_ref[...], w_ref[...])

    @pl.when(g == 0)
    def _():
        o_ref[...] = (y * (HEAD_DIM ** -0.5 * LOG2E)).astype(BF16)

    @pl.when((g == 1) | (g == 2))
    def _():
        o_ref[...] = y.astype(BF16)

    @pl.when(g == 3)
    def _():
        o_ref[...] = _silu(y).astype(BF16)

    @pl.when(g >= 4)
    def _():
        o_ref[...] = jax.nn.sigmoid(y).astype(BF16)


def _proj_call(h, w_a, tm=1024, tn=512):
    r = h.shape[0]
    ng = w_a.shape[0]
    return pl.pallas_call(
        _proj_kernel,
        grid=(r // tm, ng, D_MODEL // tn),
        in_specs=[
            pl.BlockSpec((tm, D_MODEL), lambda i, g, j: (i, 0)),
            pl.BlockSpec((None, D_MODEL, tn), lambda i, g, j: (g, 0, j)),
        ],
        out_specs=pl.BlockSpec((None, tm, tn), lambda i, g, j: (g, i, j)),
        out_shape=jax.ShapeDtypeStruct((ng, r, D_MODEL), BF16),
        compiler_params=pltpu.CompilerParams(
            dimension_semantics=("arbitrary", "arbitrary", "arbitrary"),
            vmem_limit_bytes=VMEM_LIMIT),
        name="plain_proj",
    )(h, w_a)


def _conv_kernel(h_ref, w_ref, gm_ref, cw_ref, o_ref, gbuf_ref, *, tiles_per_batch):
    i = pl.program_id(1)
    tm = h_ref.shape[0]

    @pl.when(i % tiles_per_batch == 0)
    def _():
        gbuf_ref[0:SUBLANES, :] = gm_ref[N_META - SUBLANES:N_META, :]

    h = h_ref[...]
    u = _dot(h, w_ref[0])
    gc = _dot(h, w_ref[2])
    g = gc * u
    gbuf_ref[SUBLANES:SUBLANES + tm, :] = g
    conv = (gbuf_ref[SUBLANES - 2:SUBLANES - 2 + tm, :] * cw_ref[0:1, :]
            + gbuf_ref[SUBLANES - 1:SUBLANES - 1 + tm, :] * cw_ref[1:2, :]
            + g * cw_ref[2:3, :])
    gbuf_ref[0:SUBLANES, :] = g[tm - SUBLANES:tm, :]
    gb = _dot(h, w_ref[1])
    zc = _dot(h, w_ref[3])
    o_ref[...] = (gb * conv * _silu(zc)).astype(BF16)


def _conv_call(h, w_b, g_meta, conv_w, rows_per_batch, tm=1024, tc=256):
    r = h.shape[0]
    return pl.pallas_call(
        functools.partial(_conv_kernel, tiles_per_batch=rows_per_batch // tm),
        grid=(D_MODEL // tc, r // tm),
        in_specs=[
            pl.BlockSpec((tm, D_MODEL), lambda j, i: (i, 0)),
            pl.BlockSpec((4, D_MODEL, tc), lambda j, i: (0, 0, j)),
            pl.BlockSpec((N_META, tc), lambda j, i: (0, j)),
            pl.BlockSpec((CONV_K, tc), lambda j, i: (0, j)),
        ],
        out_specs=pl.BlockSpec((tm, tc), lambda j, i: (i, j)),
        out_shape=jax.ShapeDtypeStruct((r, D_MODEL), BF16),
        scratch_shapes=[pltpu.VMEM((SUBLANES + tm, tc), F32)],
        compiler_params=pltpu.CompilerParams(
            dimension_semantics=("arbitrary", "arbitrary"), vmem_limit_bytes=VMEM_LIMIT),
        name="conv_proj",
    )(h, w_b, g_meta, conv_w)


def _attn_kernel(q_ref, k_ref, v_ref, sz_ref, km_ref, vm_ref, gq_ref, gk_ref, gm_ref,
                 eq_ref, ek_ref, o_ref,
                 qa_ref, kf_ref, s_ref, tmax_ref, m_ref, l_ref, acc_ref):
    i = pl.program_id(2)
    tq = q_ref.shape[0]
    nblk = tq // LANES

    @pl.when(i == 0)
    def _():
        kf_ref[...] = _dot(gk_ref[...], ek_ref[...]).astype(BF16)

    qa = jnp.concatenate(
        [q_ref[...], _dot(gq_ref[...], eq_ref[...]).astype(BF16)], axis=1)
    qa_ref[...] = qa

    kma = jnp.concatenate(
        [km_ref[...], _dot(gm_ref[...], ek_ref[...]).astype(BF16)], axis=1)
    sm = _dot_nt(qa, kma)
    m0 = jnp.max(sm, axis=1, keepdims=True)
    pm = jnp.exp2(sm - m0)
    m_ref[...] = jnp.broadcast_to(m0, (tq, LANES))
    lane = lax.broadcasted_iota(jnp.int32, (tq, LANES), 1)
    l_ref[...] = jnp.where(lane == 0, jnp.sum(pm, axis=1, keepdims=True), 0.0)
    acc_ref[...] = _dot(pm.astype(BF16), vm_ref[...])

    def stage_a(j, slot, diag_offset):
        off = pl.multiple_of(j * tq, tq)
        ka = jnp.concatenate([k_ref[pl.ds(off, tq), :], kf_ref[pl.ds(off, tq), :]], axis=1)
        s = _dot_nt(qa_ref[...], ka)
        if diag_offset is not None:
            row = lax.broadcasted_iota(jnp.int32, (tq, tq), 0)
            col = lax.broadcasted_iota(jnp.int32, (tq, tq), 1)
            s = jnp.where(col + diag_offset <= row, s, NEG_BIG)
        s_ref[slot] = s
        t = s[:, 0:LANES]
        for c in range(1, nblk):
            t = jnp.maximum(t, s[:, c * LANES:(c + 1) * LANES])
        tmax_ref[slot] = jnp.broadcast_to(jnp.max(t, axis=1, keepdims=True), (tq, LANES))

    def stage_b(j, slot):
        off = pl.multiple_of(j * tq, tq)
        m_old = m_ref[...]
        m_new = jnp.maximum(m_old, tmax_ref[slot])
        alpha = jnp.exp2(m_old - m_new)
        l_new = alpha * l_ref[...]
        ps = []
        for c in range(nblk):
            p = jnp.exp2(s_ref[slot, :, c * LANES:(c + 1) * LANES] - m_new)
            l_new = l_new + p
            ps.append(p.astype(BF16))
        pv = _dot(jnp.concatenate(ps, axis=1), v_ref[pl.ds(off, tq), :])
        acc_ref[...] = alpha * acc_ref[...] + pv
        l_ref[...] = l_new
        m_ref[...] = m_new

    stage_a(0, 0, (0 - i) * tq)

    def body(j, carry):
        slot = lax.rem(j, 2)
        stage_b(j, slot)
        stage_a(j + 1, 1 - slot, None)
        return carry

    lax.fori_loop(0, jnp.maximum(i - 1, 0), body, 0)

    @pl.when(i >= 1)
    def _():
        slot = lax.rem(i - 1, 2)
        stage_b(i - 1, slot)
        stage_a(i, 1 - slot, 0)

    stage_b(i, lax.rem(i, 2))
    l = jnp.sum(l_ref[...], axis=1, keepdims=True)
    o_ref[...] = (acc_ref[...] / l * sz_ref[...].astype(F32)).astype(BF16)


def _attn_call(proj, k_meta, v_meta, gate, gate_meta, eq, ek, batch, seq, tq=512):
    nq = seq // tq
    r = batch * seq
    return pl.pallas_call(
        _attn_kernel,
        grid=(batch, HEADS, nq),
        in_specs=[
            pl.BlockSpec((None, tq, HEAD_DIM), lambda b, h, i: (0, b * nq + i, h)),
            pl.BlockSpec((None, seq, HEAD_DIM), lambda b, h, i: (1, b, h)),
            pl.BlockSpec((None, seq, HEAD_DIM), lambda b, h, i: (2, b, h)),
            pl.BlockSpec((None, tq, HEAD_DIM), lambda b, h, i: (3, b * nq + i, h)),
            pl.BlockSpec((N_META, HEAD_DIM), lambda b, h, i: (0, h)),
            pl.BlockSpec((N_META, HEAD_DIM), lambda b, h, i: (0, h)),
            pl.BlockSpec((tq, LANES), lambda b, h, i: (b * nq + i, 0)),
            pl.BlockSpec((seq, LANES), lambda b, h, i: (b, 0)),
            pl.BlockSpec((N_META, LANES), lambda b, h, i: (0, 0)),
            pl.BlockSpec((None, LANES, LANES), lambda b, h, i: (h, 0, 0)),
            pl.BlockSpec((None, LANES, LANES), lambda b, h, i: (h, 0, 0)),
        ],
        out_specs=pl.BlockSpec((tq, HEAD_DIM), lambda b, h, i: (b * nq + i, h)),
        out_shape=jax.ShapeDtypeStruct((r, D_MODEL), BF16),
        scratch_shapes=[
            pltpu.VMEM((tq, 2 * LANES), BF16),
            pltpu.VMEM((seq, LANES), BF16),
            pltpu.VMEM((2, tq, tq), F32),
            pltpu.VMEM((2, tq, LANES), F32),
            pltpu.VMEM((tq, LANES), F32),
            pltpu.VMEM((tq, LANES), F32),
            pltpu.VMEM((tq, HEAD_DIM), F32),
        ],
        compiler_params=pltpu.CompilerParams(
            dimension_semantics=("arbitrary", "arbitrary", "arbitrary"),
            vmem_limit_bytes=VMEM_LIMIT),
        name="fox_attention",
    )(proj, proj, proj, proj, k_meta, v_meta, gate, gate, gate_meta, eq, ek)


def _out_kernel(aa_ref, ac_ref, sma_ref, smc_ref, x_ref, w_ref, gain_ref, o_ref):
    ya = _dot(aa_ref[...], w_ref[0])
    yc = _dot(ac_ref[...], w_ref[1])
    merged = sma_ref[...].astype(F32) * ya + smc_ref[...].astype(F32) * yc
    y = x_ref[...] + _dot(merged.astype(BF16), w_ref[2])
    o_ref[...] = _rmsnorm(y, gain_ref[...])


def _out_call(a_att, a_conv, proj, x2d, w_o, gain, tm=256):
    r = x2d.shape[0]
    row = lambda i: (i, 0)
    return pl.pallas_call(
        _out_kernel,
        grid=(r // tm,),
        in_specs=[
            pl.BlockSpec((tm, D_MODEL), row),
            pl.BlockSpec((tm, D_MODEL), row),
            pl.BlockSpec((None, tm, D_MODEL), lambda i: (4, i, 0)),
            pl.BlockSpec((None, tm, D_MODEL), lambda i: (5, i, 0)),
            pl.BlockSpec((tm, D_MODEL), row),
            pl.BlockSpec((3, D_MODEL, D_MODEL), lambda i: (0, 0, 0),
                         pipeline_mode=pl.Buffered(1)),
            pl.BlockSpec((1, D_MODEL), lambda i: (0, 0)),
        ],
        out_specs=pl.BlockSpec((tm, D_MODEL), row),
        out_shape=jax.ShapeDtypeStruct((r, D_MODEL), F32),
        compiler_params=pltpu.CompilerParams(
            dimension_semantics=("arbitrary",), vmem_limit_bytes=VMEM_LIMIT),
        name="out_proj",
    )(a_att, a_conv, proj, proj, x2d, w_o, gain)


def kernel(x, meta_tokens, norm_gain, w_in, b_f, conv_w, w_att_o, w_conv_o, w_out, final_gain):
    batch, seq, _ = x.shape
    assert norm_gain.shape[0] == 1, "single-layer block"
    w = w_in[0]
    sizes = (D_MODEL, D_MODEL, D_MODEL, HEADS, D_MODEL,
             D_MODEL, D_MODEL, D_MODEL, D_MODEL, D_MODEL, D_MODEL)
    offs = [0]
    for sz in sizes:
        offs.append(offs[-1] + sz)
    col = lambda n: w[:, offs[n]:offs[n + 1]].astype(BF16)
    w_a = jnp.stack([col(0), col(1), col(2), col(4), col(9), col(10)])
    w_b = jnp.stack([col(5), col(6), col(7), col(8)])
    wf = jnp.pad(col(3), ((0, 0), (0, LANES - HEADS)))
    bf = jnp.pad(b_f[0].astype(F32), (0, LANES - HEADS)).reshape(1, LANES)
    w_o = jnp.stack([w_att_o[0], w_conv_o[0], w_out[0]]).astype(BF16)
    gain = norm_gain[0].reshape(1, D_MODEL).astype(F32)
    fgain = final_gain.reshape(1, D_MODEL).astype(F32)
    place, ones, eq, ek = _feature_constants()

    k_meta, v_meta, g_meta, cm, gate_meta = _meta_call(
        meta_tokens.astype(F32), gain, w_a, w_b, wf, bf, place, ones)
    h, gate = _norm_call(x, gain, wf, bf, cm, place, ones)
    proj = _proj_call(h, w_a)
    a_conv = _conv_call(h, w_b, g_meta, conv_w[0].astype(F32), seq)
    a_att = _attn_call(proj, k_meta, v_meta, gate, gate_meta, eq, ek, batch, seq)
    out = _out_call(a_att, a_conv, proj, x.reshape(batch * seq, D_MODEL), w_o, fgain)
    return out.reshape(batch, seq, D_MODEL)
```

```python
import functools

import numpy as np

import jax
import jax.numpy as jnp
from jax import lax
from jax.experimental import pallas as pl
from jax.experimental.pallas import tpu as pltpu

D_MODEL = 2048
N_META = 16
HEADS = 16
HEAD_DIM = 128
CONV_K = 3
EPS = 1e-6
LANES = 128
SUBLANES = 8
NEG_BIG = -1e30
LOG2E = 1.4426950408889634
GATE_LANES = LANES // HEADS
VMEM_LIMIT = 56 * 1024 * 1024

_GROUPS = ("q", "k", "v", "f", "z_att", "u", "gate_b", "gate_c", "z_conv", "m_att", "m_conv")
_WIDTHS = (D_MODEL, D_MODEL, D_MODEL, HEADS) + (D_MODEL,) * 7
COL = {name: sum(_WIDTHS[:n]) for n, name in enumerate(_GROUPS)}
N_IN = sum(_WIDTHS)

F32 = jnp.float32
BF16 = jnp.bfloat16


def _dot(a, b):
    return jnp.dot(a, b, preferred_element_type=F32)


def _dot_nt(a, b):
    return lax.dot_general(a, b, (((1,), (1,)), ((), ())), preferred_element_type=F32)


def _rmsnorm(x, gain):
    return x * lax.rsqrt(jnp.mean(x * x, axis=-1, keepdims=True) + EPS) * gain


def _log_sigmoid(x):
    return jnp.minimum(x, 0.0) - jnp.log1p(jnp.exp(-jnp.abs(x)))


def _silu(x):
    return x * jax.nn.sigmoid(x)


def _split3(x):
    hi = x.astype(BF16)
    r1 = x - hi.astype(F32)
    mid = r1.astype(BF16)
    lo = (r1 - mid.astype(F32)).astype(BF16)
    return hi, mid, lo


def _cumsum_rows(x):
    n = x.shape[0]
    tri = (lax.broadcasted_iota(jnp.int32, (n, n), 0)
           >= lax.broadcasted_iota(jnp.int32, (n, n), 1)).astype(BF16)
    hi, mid, lo = _split3(x)
    return _dot(tri, hi) + _dot(tri, mid) + _dot(tri, lo)


def _gate_features(c, place_ref, ones_ref):
    hi, mid, lo = _split3(c * LOG2E)
    cat = jnp.concatenate([hi, mid, lo], axis=1)
    return (_dot(cat, place_ref[...]) + ones_ref[...]).astype(BF16)


def _feature_constants():
    place = np.zeros((3 * LANES, LANES), np.float32)
    ones = np.zeros((1, LANES), np.float32)
    eq = np.zeros((HEADS, LANES, LANES), np.float32)
    ek = np.zeros((HEADS, LANES, LANES), np.float32)
    for h in range(HEADS):
        base = GATE_LANES * h
        ones[0, base + 3] = 1.0
        for f in range(3):
            place[f * LANES + h, base + f] = 1.0
            eq[h, base + f, f] = 1.0
            eq[h, base + 3, 3 + f] = 1.0
            ek[h, base + 3, f] = 1.0
            ek[h, base + f, 3 + f] = -1.0
    return (jnp.asarray(place, BF16), jnp.asarray(ones, F32),
            jnp.asarray(eq, BF16), jnp.asarray(ek, BF16))


def _w_tile_specs(col_of, tn, tile_of):
    main = pl.BlockSpec((None, D_MODEL, tn),
                        lambda *ids: (0, 0, col_of(*ids) // tn + tile_of(*ids)))
    tail = pl.BlockSpec((None, D_MODEL, LANES),
                        lambda *ids: (0, 0, (col_of(*ids) + (tile_of(*ids) + 1) * tn) // LANES))
    return main, tail


def _aligned(col):
    return col - col % LANES


def _stage_weight(main_ref, tail_ref, dst_ref, shift, chunk=256):
    tn = main_ref.shape[1]
    for r in range(0, main_ref.shape[0], chunk):
        a = main_ref[r:r + chunk, :]
        if shift:
            a = jnp.concatenate([a, tail_ref[r:r + chunk, :]], axis=1)[:, shift:shift + tn]
        dst_ref[r:r + chunk, :] = a.astype(BF16)


def _meta_kernel(meta_ref, gain_ref, wk_ref, wv_ref, wu_ref, wu_tail_ref, wgc_ref, wgc_tail_ref,
                 wf_ref, bf_ref, place_ref, ones_ref,
                 k_ref, v_ref, g_ref, cm_ref, gm_ref, wu_s, wgc_s):
    h = _rmsnorm(meta_ref[...], gain_ref[...]).astype(BF16)
    k_ref[...] = _dot(h, wk_ref[...].astype(BF16)).astype(BF16)
    v_ref[...] = _dot(h, wv_ref[...].astype(BF16)).astype(BF16)
    _stage_weight(wu_ref, wu_tail_ref, wu_s, COL["u"] % LANES)
    _stage_weight(wgc_ref, wgc_tail_ref, wgc_s, COL["gate_c"] % LANES)
    g_ref[...] = _dot(h, wgc_s[...]) * _dot(h, wu_s[...])
    cm = _cumsum_rows(_log_sigmoid(_dot(h, wf_ref[...].astype(BF16)) + bf_ref[...]))
    cm_ref[...] = cm
    gm_ref[...] = _gate_features(cm, place_ref, ones_ref)


def _meta_call(meta, gain, w_in, bf, place, ones, tn=512):
    nt = D_MODEL // tn
    const = lambda shape: pl.BlockSpec(shape, lambda j: (0,) * len(shape))
    tile = lambda j: j
    group = lambda name: _w_tile_specs(lambda j: _aligned(COL[name]), tn, tile)
    wk, _ = group("k")
    wv, _ = group("v")
    wu, wu_tail = group("u")
    wgc, wgc_tail = group("gate_c")
    wf = pl.BlockSpec((None, D_MODEL, LANES), lambda j: (0, 0, COL["f"] // LANES))
    return pl.pallas_call(
        _meta_kernel,
        grid=(nt,),
        in_specs=[
            const((N_META, D_MODEL)), const((1, D_MODEL)),
            wk, wv, wu, wu_tail, wgc, wgc_tail, wf,
            const((1, LANES)), const((3 * LANES, LANES)), const((1, LANES)),
        ],
        out_specs=[
            pl.BlockSpec((N_META, tn), lambda j: (0, j)),
            pl.BlockSpec((N_META, tn), lambda j: (0, j)),
            pl.BlockSpec((N_META, tn), lambda j: (0, j)),
            const((N_META, LANES)),
            const((N_META, LANES)),
        ],
        out_shape=[
            jax.ShapeDtypeStruct((N_META, D_MODEL), BF16),
            jax.ShapeDtypeStruct((N_META, D_MODEL), BF16),
            jax.ShapeDtypeStruct((N_META, D_MODEL), F32),
            jax.ShapeDtypeStruct((N_META, LANES), F32),
            jax.ShapeDtypeStruct((N_META, LANES), BF16),
        ],
        scratch_shapes=[pltpu.VMEM((D_MODEL, tn), BF16), pltpu.VMEM((D_MODEL, tn), BF16)],
        compiler_params=pltpu.CompilerParams(
            dimension_semantics=("arbitrary",), vmem_limit_bytes=VMEM_LIMIT),
        name="meta_proj",
    )(meta, gain, w_in, w_in, w_in, w_in, w_in, w_in, w_in, bf, place, ones)


def _norm_kernel(x_ref, gain_ref, wf_ref, bf_ref, cm_ref, place_ref, ones_ref,
                 h_ref, gate_ref, carry_ref, *, chunk):
    @pl.when(pl.program_id(1) == 0)
    def _():
        carry_ref[...] = cm_ref[N_META - 1:N_META, :]

    h = _rmsnorm(x_ref[...], gain_ref[...]).astype(BF16)
    h_ref[...] = h
    logf = _log_sigmoid(_dot(h, wf_ref[...].astype(BF16)) + bf_ref[...])
    tm = logf.shape[0]
    run = carry_ref[...]
    parts = []
    for j in range(tm // chunk):
        c = _cumsum_rows(logf[j * chunk:(j + 1) * chunk]) + run
        run = c[chunk - 1:chunk, :]
        parts.append(c)
    carry_ref[...] = run
    gate_ref[...] = _gate_features(jnp.concatenate(parts, axis=0), place_ref, ones_ref)


def _norm_call(x, gain, w_in, bf, cm, place, ones, tm=512, chunk=256):
    b, s, _ = x.shape
    nt = s // tm
    const = lambda shape: pl.BlockSpec(shape, lambda bi, i: (0,) * len(shape))
    return pl.pallas_call(
        functools.partial(_norm_kernel, chunk=chunk),
        grid=(b, nt),
        in_specs=[
            pl.BlockSpec((None, tm, D_MODEL), lambda bi, i: (bi, i, 0)),
            const((1, D_MODEL)),
            pl.BlockSpec((None, D_MODEL, LANES), lambda bi, i: (0, 0, COL["f"] // LANES)),
            const((1, LANES)),
            const((N_META, LANES)),
            const((3 * LANES, LANES)),
            const((1, LANES)),
        ],
        out_specs=[
            pl.BlockSpec((tm, D_MODEL), lambda bi, i: (bi * nt + i, 0)),
            pl.BlockSpec((tm, LANES), lambda bi, i: (bi * nt + i, 0)),
        ],
        out_shape=[
            jax.ShapeDtypeStruct((b * s, D_MODEL), BF16),
            jax.ShapeDtypeStruct((b * s, LANES), BF16),
        ],
        scratch_shapes=[pltpu.VMEM((1, LANES), F32)],
        compiler_params=pltpu.CompilerParams(
            dimension_semantics=("arbitrary", "arbitrary"), vmem_limit_bytes=VMEM_LIMIT),
        name="norm_forget",
    )(x, gain, w_in, bf, cm, place, ones)


_PROJ_GROUPS = ("q", "k", "v", "z_att", "m_att", "m_conv")
_PROJ_ALIGNED = 3


def _proj_group_col(g):
    col = jnp.int32(_aligned(COL[_PROJ_GROUPS[0]]))
    for n in range(1, len(_PROJ_GROUPS)):
        col = jnp.where(g == n, _aligned(COL[_PROJ_GROUPS[n]]), col)
    return col


def _proj_kernel(h_ref, w_ref, w_tail_ref, o_ref, w_s, *, row_chunk):
    g = pl.program_id(0)

    @pl.when(pl.program_id(2) == 0)
    def _():
        @pl.when(g < _PROJ_ALIGNED)
        def _():
            _stage_weight(w_ref, w_tail_ref, w_s, 0)

        @pl.when(g >= _PROJ_ALIGNED)
        def _():
            _stage_weight(w_ref, w_tail_ref, w_s, COL["z_att"] % LANES)

    scale = jnp.where(g == 0, HEAD_DIM ** -0.5 * LOG2E, 1.0).astype(F32)
    for r in range(0, h_ref.shape[0], row_chunk):
        y = _dot(h_ref[r:r + row_chunk, :], w_s[...])
        o_ref[r:r + row_chunk, :] = (y * scale).astype(BF16)


def _proj_call(h, w_in, tm=2048, tn=512, row_chunk=512):
    r = h.shape[0]
    ng = len(_PROJ_GROUPS)
    assert all(COL[n] % LANES == COL["z_att"] % LANES for n in _PROJ_GROUPS[_PROJ_ALIGNED:])
    main, tail = _w_tile_specs(lambda g, j, i: _proj_group_col(g), tn, lambda g, j, i: j)
    return pl.pallas_call(
        functools.partial(_proj_kernel, row_chunk=row_chunk),
        grid=(ng, D_MODEL // tn, r // tm),
        in_specs=[pl.BlockSpec((tm, D_MODEL), lambda g, j, i: (i, 0)), main, tail],
        out_specs=pl.BlockSpec((None, tm, tn), lambda g, j, i: (g, i, j)),
        out_shape=jax.ShapeDtypeStruct((ng, r, D_MODEL), BF16),
        scratch_shapes=[pltpu.VMEM((D_MODEL, tn), BF16)],
        compiler_params=pltpu.CompilerParams(
            dimension_semantics=("arbitrary", "arbitrary", "arbitrary"),
            vmem_limit_bytes=VMEM_LIMIT),
        name="plain_proj",
    )(h, w_in, w_in)


_CONV_GROUPS = ("u", "gate_b", "gate_c", "z_conv")


def _conv_kernel(h_ref, wu_ref, wu_t, wgb_ref, wgb_t, wgc_ref, wgc_t, wzc_ref, wzc_t,
                 gm_ref, cw_ref, o_ref, w_s, gbuf_ref, *, tiles_per_batch):
    i = pl.program_id(1)
    tm = h_ref.shape[0]

    @pl.when(i == 0)
    def _():
        pairs = ((wu_ref, wu_t), (wgb_ref, wgb_t), (wgc_ref, wgc_t), (wzc_ref, wzc_t))
        for n, (main, tail) in enumerate(pairs):
            _stage_weight(main, tail, w_s.at[n], COL[_CONV_GROUPS[n]] % LANES)

    @pl.when(i % tiles_per_batch == 0)
    def _():
        gbuf_ref[0:SUBLANES, :] = gm_ref[N_META - SUBLANES:N_META, :]

    h = h_ref[...]
    u = _dot(h, w_s[0])
    gc = _dot(h, w_s[2])
    g = gc * u
    gbuf_ref[SUBLANES:SUBLANES + tm, :] = g
    conv = (gbuf_ref[SUBLANES - 2:SUBLANES - 2 + tm, :] * cw_ref[0:1, :]
            + gbuf_ref[SUBLANES - 1:SUBLANES - 1 + tm, :] * cw_ref[1:2, :]
            + g * cw_ref[2:3, :])
    gbuf_ref[0:SUBLANES, :] = g[tm - SUBLANES:tm, :]
    gb = _dot(h, w_s[1])
    zc = _dot(h, w_s[3])
    o_ref[...] = (gb * conv * _silu(zc)).astype(BF16)


def _conv_call(h, w_in, g_meta, conv_w, rows_per_batch, tm=1024, tc=256):
    r = h.shape[0]
    w_specs = []
    for name in _CONV_GROUPS:
        w_specs += _w_tile_specs(lambda j, i, name=name: _aligned(COL[name]), tc, lambda j, i: j)
    return pl.pallas_call(
        functools.partial(_conv_kernel, tiles_per_batch=rows_per_batch // tm),
        grid=(D_MODEL // tc, r // tm),
        in_specs=[pl.BlockSpec((tm, D_MODEL), lambda j, i: (i, 0))] + w_specs + [
            pl.BlockSpec((N_META, tc), lambda j, i: (0, j)),
            pl.BlockSpec((CONV_K, tc), lambda j, i: (0, j)),
        ],
        out_specs=pl.BlockSpec((tm, tc), lambda j, i: (i, j)),
        out_shape=jax.ShapeDtypeStruct((r, D_MODEL), BF16),
        scratch_shapes=[pltpu.VMEM((len(_CONV_GROUPS), D_MODEL, tc), BF16),
                        pltpu.VMEM((SUBLANES + tm, tc), F32)],
        compiler_params=pltpu.CompilerParams(
            dimension_semantics=("arbitrary", "arbitrary"), vmem_limit_bytes=VMEM_LIMIT),
        name="conv_proj",
    )(h, *([w_in] * (2 * len(_CONV_GROUPS))), g_meta, conv_w)


def _attn_kernel(q_ref, k_ref, v_ref, z_ref, km_ref, vm_ref, gq_ref, gk_ref, gm_ref,
                 eq_ref, ek_ref, o_ref,
                 qa_ref, kf_ref, s_ref, tmax_ref, m_ref, l_ref, acc_ref):
    i = pl.program_id(2)
    tq = q_ref.shape[0]
    nblk = tq // LANES

    @pl.when(i == 0)
    def _():
        kf_ref[...] = _dot(gk_ref[...], ek_ref[...]).astype(BF16)

    qa = jnp.concatenate(
        [q_ref[...], _dot(gq_ref[...], eq_ref[...]).astype(BF16)], axis=1)
    qa_ref[...] = qa

    kma = jnp.concatenate(
        [km_ref[...], _dot(gm_ref[...], ek_ref[...]).astype(BF16)], axis=1)
    sm = _dot_nt(qa, kma)
    m0 = jnp.max(sm, axis=1, keepdims=True)
    pm = jnp.exp2(sm - m0)
    m_ref[...] = jnp.broadcast_to(m0, (tq, LANES))
    lane = lax.broadcasted_iota(jnp.int32, (tq, LANES), 1)
    l_ref[...] = jnp.where(lane == 0, jnp.sum(pm, axis=1, keepdims=True), 0.0)
    acc_ref[...] = _dot(pm.astype(BF16), vm_ref[...])

    def stage_a(j, slot, diag_offset):
        off = pl.multiple_of(j * tq, tq)
        ka = jnp.concatenate([k_ref[pl.ds(off, tq), :], kf_ref[pl.ds(off, tq), :]], axis=1)
        s = _dot_nt(qa_ref[...], ka)
        if diag_offset is not None:
            row = lax.broadcasted_iota(jnp.int32, (tq, tq), 0)
            col = lax.broadcasted_iota(jnp.int32, (tq, tq), 1)
            s = jnp.where(col + diag_offset <= row, s, NEG_BIG)
        s_ref[slot] = s
        t = s[:, 0:LANES]
        for c in range(1, nblk):
            t = jnp.maximum(t, s[:, c * LANES:(c + 1) * LANES])
        tmax_ref[slot] = jnp.broadcast_to(jnp.max(t, axis=1, keepdims=True), (tq, LANES))

    def stage_b(j, slot):
        off = pl.multiple_of(j * tq, tq)
        m_old = m_ref[...]
        m_new = jnp.maximum(m_old, tmax_ref[slot])
        alpha = jnp.exp2(m_old - m_new)
        l_new = alpha * l_ref[...]
        ps = []
        for c in range(nblk):
            p = jnp.exp2(s_ref[slot, :, c * LANES:(c + 1) * LANES] - m_new)
            l_new = l_new + p
            ps.append(p.astype(BF16))
        pv = _dot(jnp.concatenate(ps, axis=1), v_ref[pl.ds(off, tq), :])
        acc_ref[...] = alpha * acc_ref[...] + pv
        l_ref[...] = l_new
        m_ref[...] = m_new

    stage_a(0, 0, (0 - i) * tq)

    def body(j, carry):
        slot = lax.rem(j, 2)
        stage_b(j, slot)
        stage_a(j + 1, 1 - slot, None)
        return carry

    lax.fori_loop(0, jnp.maximum(i - 1, 0), body, 0)

    @pl.when(i >= 1)
    def _():
        slot = lax.rem(i - 1, 2)
        stage_b(i - 1, slot)
        stage_a(i, 1 - slot, 0)

    stage_b(i, lax.rem(i, 2))
    l = jnp.sum(l_ref[...], axis=1, keepdims=True)
    o_ref[...] = (acc_ref[...] / l * _silu(z_ref[...].astype(F32))).astype(BF16)


def _attn_call(proj, k_meta, v_meta, gate, gate_meta, eq, ek, batch, seq, tq=512):
    nq = seq // tq
    r = batch * seq
    return pl.pallas_call(
        _attn_kernel,
        grid=(batch, HEADS, nq),
        in_specs=[
            pl.BlockSpec((None, tq, HEAD_DIM), lambda b, h, i: (0, b * nq + i, h)),
            pl.BlockSpec((None, seq, HEAD_DIM), lambda b, h, i: (1, b, h)),
            pl.BlockSpec((None, seq, HEAD_DIM), lambda b, h, i: (2, b, h)),
            pl.BlockSpec((None, tq, HEAD_DIM), lambda b, h, i: (3, b * nq + i, h)),
            pl.BlockSpec((N_META, HEAD_DIM), lambda b, h, i: (0, h)),
            pl.BlockSpec((N_META, HEAD_DIM), lambda b, h, i: (0, h)),
            pl.BlockSpec((tq, LANES), lambda b, h, i: (b * nq + i, 0)),
            pl.BlockSpec((seq, LANES), lambda b, h, i: (b, 0)),
            pl.BlockSpec((N_META, LANES), lambda b, h, i: (0, 0)),
            pl.BlockSpec((None, LANES, LANES), lambda b, h, i: (h, 0, 0)),
            pl.BlockSpec((None, LANES, LANES), lambda b, h, i: (h, 0, 0)),
        ],
        out_specs=pl.BlockSpec((tq, HEAD_DIM), lambda b, h, i: (b * nq + i, h)),
        out_shape=jax.ShapeDtypeStruct((r, D_MODEL), BF16),
        scratch_shapes=[
            pltpu.VMEM((tq, 2 * LANES), BF16),
            pltpu.VMEM((seq, LANES), BF16),
            pltpu.VMEM((2, tq, tq), F32),
            pltpu.VMEM((2, tq, LANES), F32),
            pltpu.VMEM((tq, LANES), F32),
            pltpu.VMEM((tq, LANES), F32),
            pltpu.VMEM((tq, HEAD_DIM), F32),
        ],
        compiler_params=pltpu.CompilerParams(
            dimension_semantics=("arbitrary", "arbitrary", "arbitrary"),
            vmem_limit_bytes=VMEM_LIMIT),
        name="fox_attention",
    )(proj, proj, proj, proj, k_meta, v_meta, gate, gate, gate_meta, eq, ek)


def _out_kernel(aa_ref, ac_ref, ma_ref, mc_ref, x_ref, w_ref, gain_ref, o_ref):
    ya = _dot(aa_ref[...], w_ref[0])
    yc = _dot(ac_ref[...], w_ref[1])
    merged = (jax.nn.sigmoid(ma_ref[...].astype(F32)) * ya
              + jax.nn.sigmoid(mc_ref[...].astype(F32)) * yc)
    y = x_ref[...] + _dot(merged.astype(BF16), w_ref[2])
    o_ref[...] = _rmsnorm(y, gain_ref[...])


def _out_call(a_att, a_conv, proj, x2d, w_o, gain, tm=256):
    r = x2d.shape[0]
    row = lambda i: (i, 0)
    return pl.pallas_call(
        _out_kernel,
        grid=(r // tm,),
        in_specs=[
            pl.BlockSpec((tm, D_MODEL), row),
            pl.BlockSpec((tm, D_MODEL), row),
            pl.BlockSpec((None, tm, D_MODEL), lambda i: (4, i, 0)),
            pl.BlockSpec((None, tm, D_MODEL), lambda i: (5, i, 0)),
            pl.BlockSpec((tm, D_MODEL), row),
            pl.BlockSpec((3, D_MODEL, D_MODEL), lambda i: (0, 0, 0),
                         pipeline_mode=pl.Buffered(1)),
            pl.BlockSpec((1, D_MODEL), lambda i: (0, 0)),
        ],
        out_specs=pl.BlockSpec((tm, D_MODEL), row),
        out_shape=jax.ShapeDtypeStruct((r, D_MODEL), F32),
        compiler_params=pltpu.CompilerParams(
            dimension_semantics=("arbitrary",), vmem_limit_bytes=VMEM_LIMIT),
        name="out_proj",
    )(a_att, a_conv, proj, proj, x2d, w_o, gain)


def kernel(x, meta_tokens, norm_gain, w_in, b_f, conv_w, w_att_o, w_conv_o, w_out, final_gain):
    batch, seq, _ = x.shape
    assert norm_gain.shape[0] == 1, "single-layer block"
    assert w_in.shape == (1, D_MODEL, N_IN) and w_in.dtype == F32
    bf = jnp.pad(b_f[0].astype(F32), (0, LANES - HEADS)).reshape(1, LANES)
    w_o = jnp.stack([w_att_o[0], w_conv_o[0], w_out[0]]).astype(BF16)
    gain = norm_gain[0].reshape(1, D_MODEL).astype(F32)
    fgain = final_gain.reshape(1, D_MODEL).astype(F32)
    place, ones, eq, ek = _feature_constants()

    k_meta, v_meta, g_meta, cm, gate_meta = _meta_call(
        meta_tokens.astype(F32), gain, w_in, bf, place, ones)
    h, gate = _norm_call(x, gain, w_in, bf, cm, place, ones)
    proj = _proj_call(h, w_in)
    a_conv = _conv_call(h, w_in, g_meta, conv_w[0].astype(F32), seq)
    a_att = _attn_call(proj, k_meta, v_meta, gate, gate_meta, eq, ek, batch, seq)
    out = _out_call(a_att, a_conv, proj, x.reshape(batch * seq, D_MODEL), w_o, fgain)
    return out.reshape(batch, seq, D_MODEL)
```

```python
import functools

import numpy as np

import jax
import jax.numpy as jnp
from jax import lax
from jax.experimental import pallas as pl
from jax.experimental.pallas import tpu as pltpu

D_MODEL = 2048
N_META = 16
HEADS = 16
HEAD_DIM = 128
CONV_K = 3
EPS = 1e-6
LANES = 128
SUBLANES = 8
NEG_BIG = -1e30
LOG2E = 1.4426950408889634
GATE_LANES = LANES // HEADS
VMEM_LIMIT = 56 * 1024 * 1024

_GROUPS = ("q", "k", "v", "f", "z_att", "u", "gate_b", "gate_c", "z_conv", "m_att", "m_conv")
_WIDTHS = (D_MODEL, D_MODEL, D_MODEL, HEADS) + (D_MODEL,) * 7
COL = {name: sum(_WIDTHS[:n]) for n, name in enumerate(_GROUPS)}
N_IN = sum(_WIDTHS)

F32 = jnp.float32
BF16 = jnp.bfloat16


def _dot(a, b):
    return jnp.dot(a, b, preferred_element_type=F32)


def _dot_nt(a, b):
    return lax.dot_general(a, b, (((1,), (1,)), ((), ())), preferred_element_type=F32)


def _rmsnorm(x, gain):
    return x * lax.rsqrt(jnp.mean(x * x, axis=-1, keepdims=True) + EPS) * gain


def _log_sigmoid(x):
    return jnp.minimum(x, 0.0) - jnp.log1p(jnp.exp(-jnp.abs(x)))


def _silu(x):
    return x * jax.nn.sigmoid(x)


def _split3(x):
    hi = x.astype(BF16)
    r1 = x - hi.astype(F32)
    mid = r1.astype(BF16)
    lo = (r1 - mid.astype(F32)).astype(BF16)
    return hi, mid, lo


def _cumsum_rows(x):
    n = x.shape[0]
    tri = (lax.broadcasted_iota(jnp.int32, (n, n), 0)
           >= lax.broadcasted_iota(jnp.int32, (n, n), 1)).astype(BF16)
    hi, mid, lo = _split3(x)
    return _dot(tri, hi) + _dot(tri, mid) + _dot(tri, lo)


def _gate_features(c, place_ref, ones_ref):
    hi, mid, lo = _split3(c * LOG2E)
    cat = jnp.concatenate([hi, mid, lo], axis=1)
    return (_dot(cat, place_ref[...]) + ones_ref[...]).astype(BF16)


def _feature_constants():
    place = np.zeros((3 * LANES, LANES), np.float32)
    ones = np.zeros((1, LANES), np.float32)
    eq = np.zeros((HEADS, LANES, LANES), np.float32)
    ek = np.zeros((HEADS, LANES, LANES), np.float32)
    for h in range(HEADS):
        base = GATE_LANES * h
        ones[0, base + 3] = 1.0
        for f in range(3):
            place[f * LANES + h, base + f] = 1.0
            eq[h, base + f, f] = 1.0
            eq[h, base + 3, 3 + f] = 1.0
            ek[h, base + 3, f] = 1.0
            ek[h, base + f, 3 + f] = -1.0
    return (jnp.asarray(place, BF16), jnp.asarray(ones, F32),
            jnp.asarray(eq, BF16), jnp.asarray(ek, BF16))


def _w_tile_spec(tn, row_of):
    assert all(c % HEADS == 0 for c in COL.values()) and tn % HEADS == 0
    return pl.BlockSpec((pl.Element(tn), pl.Element(D_MODEL)),
                        lambda *ids: (pl.multiple_of(row_of(*ids), HEADS), 0))


def _forget_w_spec():
    assert COL["f"] % LANES == 0
    return pl.BlockSpec((LANES, D_MODEL), lambda *ids: (COL["f"] // LANES, 0))


def _stage_weight(src_ref, dst_ref, chunk=128):
    for r in range(0, src_ref.shape[0], chunk):
        dst_ref[r:r + chunk, :] = src_ref[r:r + chunk, :].astype(BF16)


def _meta_kernel(meta_ref, gain_ref, wk_ref, wv_ref, wu_ref, wgc_ref, wf_ref, bf_ref,
                 place_ref, ones_ref, k_ref, v_ref, g_ref, cm_ref, gm_ref):
    h = _rmsnorm(meta_ref[...], gain_ref[...]).astype(BF16)
    proj = lambda w_ref: _dot_nt(h, w_ref[...].astype(BF16))
    k_ref[...] = proj(wk_ref).astype(BF16)
    v_ref[...] = proj(wv_ref).astype(BF16)
    g_ref[...] = proj(wgc_ref) * proj(wu_ref)
    cm = _cumsum_rows(_log_sigmoid(proj(wf_ref) + bf_ref[...]))
    cm_ref[...] = cm
    gm_ref[...] = _gate_features(cm, place_ref, ones_ref)


def _meta_call(meta, gain, wt, bf, place, ones, tn=512):
    nt = D_MODEL // tn
    const = lambda shape: pl.BlockSpec(shape, lambda j: (0,) * len(shape))
    group = lambda name: _w_tile_spec(tn, lambda j: COL[name] + j * tn)
    return pl.pallas_call(
        _meta_kernel,
        grid=(nt,),
        in_specs=[
            const((N_META, D_MODEL)), const((1, D_MODEL)),
            group("k"), group("v"), group("u"), group("gate_c"), _forget_w_spec(),
            const((1, LANES)), const((3 * LANES, LANES)), const((1, LANES)),
        ],
        out_specs=[
            pl.BlockSpec((N_META, tn), lambda j: (0, j)),
            pl.BlockSpec((N_META, tn), lambda j: (0, j)),
            pl.BlockSpec((N_META, tn), lambda j: (0, j)),
            const((N_META, LANES)),
            const((N_META, LANES)),
        ],
        out_shape=[
            jax.ShapeDtypeStruct((N_META, D_MODEL), BF16),
            jax.ShapeDtypeStruct((N_META, D_MODEL), BF16),
            jax.ShapeDtypeStruct((N_META, D_MODEL), F32),
            jax.ShapeDtypeStruct((N_META, LANES), F32),
            jax.ShapeDtypeStruct((N_META, LANES), BF16),
        ],
        compiler_params=pltpu.CompilerParams(
            dimension_semantics=("arbitrary",), vmem_limit_bytes=VMEM_LIMIT),
        name="meta_proj",
    )(meta, gain, wt, wt, wt, wt, wt, bf, place, ones)


def _norm_kernel(x_ref, gain_ref, wf_ref, bf_ref, cm_ref, place_ref, ones_ref,
                 h_ref, gate_ref, carry_ref, *, chunk):
    @pl.when(pl.program_id(1) == 0)
    def _():
        carry_ref[...] = cm_ref[N_META - 1:N_META, :]

    h = _rmsnorm(x_ref[...], gain_ref[...]).astype(BF16)
    h_ref[...] = h
    logf = _log_sigmoid(_dot_nt(h, wf_ref[...].astype(BF16)) + bf_ref[...])
    tm = logf.shape[0]
    run = carry_ref[...]
    parts = []
    for j in range(tm // chunk):
        c = _cumsum_rows(logf[j * chunk:(j + 1) * chunk]) + run
        run = c[chunk - 1:chunk, :]
        parts.append(c)
    carry_ref[...] = run
    gate_ref[...] = _gate_features(jnp.concatenate(parts, axis=0), place_ref, ones_ref)


def _norm_call(x, gain, wt, bf, cm, place, ones, tm=512, chunk=256):
    b, s, _ = x.shape
    nt = s // tm
    const = lambda shape: pl.BlockSpec(shape, lambda bi, i: (0,) * len(shape))
    return pl.pallas_call(
        functools.partial(_norm_kernel, chunk=chunk),
        grid=(b, nt),
        in_specs=[
            pl.BlockSpec((None, tm, D_MODEL), lambda bi, i: (bi, i, 0)),
            const((1, D_MODEL)),
            _forget_w_spec(),
            const((1, LANES)),
            const((N_META, LANES)),
            const((3 * LANES, LANES)),
            const((1, LANES)),
        ],
        out_specs=[
            pl.BlockSpec((tm, D_MODEL), lambda bi, i: (bi * nt + i, 0)),
            pl.BlockSpec((tm, LANES), lambda bi, i: (bi * nt + i, 0)),
        ],
        out_shape=[
            jax.ShapeDtypeStruct((b * s, D_MODEL), BF16),
            jax.ShapeDtypeStruct((b * s, LANES), BF16),
        ],
        scratch_shapes=[pltpu.VMEM((1, LANES), F32)],
        compiler_params=pltpu.CompilerParams(
            dimension_semantics=("arbitrary", "arbitrary"), vmem_limit_bytes=VMEM_LIMIT),
        name="norm_forget",
    )(x, gain, wt, bf, cm, place, ones)


_PROJ_GROUPS = ("q", "k", "v", "z_att", "m_att", "m_conv")


def _proj_group_row(g):
    row = jnp.int32(COL[_PROJ_GROUPS[0]])
    for n in range(1, len(_PROJ_GROUPS)):
        row = jnp.where(g == n, COL[_PROJ_GROUPS[n]], row)
    return row


def _proj_kernel(h_ref, w_ref, o_ref, w_s, *, row_chunk):
    @pl.when(pl.program_id(2) == 0)
    def _():
        _stage_weight(w_ref, w_s)

    scale = jnp.where(pl.program_id(0) == 0, HEAD_DIM ** -0.5 * LOG2E, 1.0).astype(F32)
    for r in range(0, h_ref.shape[0], row_chunk):
        y = _dot_nt(h_ref[r:r + row_chunk, :], w_s[...])
        o_ref[r:r + row_chunk, :] = (y * scale).astype(BF16)


def _proj_call(h, wt, tm=2048, tn=512, row_chunk=512):
    r = h.shape[0]
    ng = len(_PROJ_GROUPS)
    return pl.pallas_call(
        functools.partial(_proj_kernel, row_chunk=row_chunk),
        grid=(ng, D_MODEL // tn, r // tm),
        in_specs=[
            pl.BlockSpec((tm, D_MODEL), lambda g, j, i: (i, 0)),
            _w_tile_spec(tn, lambda g, j, i: _proj_group_row(g) + j * tn),
        ],
        out_specs=pl.BlockSpec((None, tm, tn), lambda g, j, i: (g, i, j)),
        out_shape=jax.ShapeDtypeStruct((ng, r, D_MODEL), BF16),
        scratch_shapes=[pltpu.VMEM((tn, D_MODEL), BF16)],
        compiler_params=pltpu.CompilerParams(
            dimension_semantics=("arbitrary", "arbitrary", "arbitrary"),
            vmem_limit_bytes=VMEM_LIMIT),
        name="plain_proj",
    )(h, wt)


_CONV_GROUPS = ("u", "gate_b", "gate_c", "z_conv")


def _conv_kernel(h_ref, wu_ref, wgb_ref, wgc_ref, wzc_ref, gm_ref, cw_ref, o_ref,
                 w_s, gbuf_ref, *, tiles_per_batch):
    i = pl.program_id(1)
    tm = h_ref.shape[0]

    @pl.when(i == 0)
    def _():
        for n, w_ref in enumerate((wu_ref, wgb_ref, wgc_ref, wzc_ref)):
            _stage_weight(w_ref, w_s.at[n])

    @pl.when(i % tiles_per_batch == 0)
    def _():
        gbuf_ref[0:SUBLANES, :] = gm_ref[N_META - SUBLANES:N_META, :]

    h = h_ref[...]
    u = _dot_nt(h, w_s[0])
    gc = _dot_nt(h, w_s[2])
    g = gc * u
    gbuf_ref[SUBLANES:SUBLANES + tm, :] = g
    conv = (gbuf_ref[SUBLANES - 2:SUBLANES - 2 + tm, :] * cw_ref[0:1, :]
            + gbuf_ref[SUBLANES - 1:SUBLANES - 1 + tm, :] * cw_ref[1:2, :]
            + g * cw_ref[2:3, :])
    gbuf_ref[0:SUBLANES, :] = g[tm - SUBLANES:tm, :]
    gb = _dot_nt(h, w_s[1])
    zc = _dot_nt(h, w_s[3])
    o_ref[...] = (gb * conv * _silu(zc)).astype(BF16)


def _conv_call(h, wt, g_meta, conv_w, rows_per_batch, tm=1024, tc=256):
    r = h.shape[0]
    w_specs = [_w_tile_spec(tc, lambda j, i, name=name: COL[name] + j * tc)
               for name in _CONV_GROUPS]
    return pl.pallas_call(
        functools.partial(_conv_kernel, tiles_per_batch=rows_per_batch // tm),
        grid=(D_MODEL // tc, r // tm),
        in_specs=[pl.BlockSpec((tm, D_MODEL), lambda j, i: (i, 0))] + w_specs + [
            pl.BlockSpec((N_META, tc), lambda j, i: (0, j)),
            pl.BlockSpec((CONV_K, tc), lambda j, i: (0, j)),
        ],
        out_specs=pl.BlockSpec((tm, tc), lambda j, i: (i, j)),
        out_shape=jax.ShapeDtypeStruct((r, D_MODEL), BF16),
        scratch_shapes=[pltpu.VMEM((len(_CONV_GROUPS), tc, D_MODEL), BF16),
                        pltpu.VMEM((SUBLANES + tm, tc), F32)],
        compiler_params=pltpu.CompilerParams(
            dimension_semantics=("arbitrary", "arbitrary"), vmem_limit_bytes=VMEM_LIMIT),
        name="conv_proj",
    )(h, *([wt] * len(_CONV_GROUPS)), g_meta, conv_w)


def _attn_kernel(q_ref, k_ref, v_ref, z_ref, km_ref, vm_ref, gq_ref, gk_ref, gm_ref,
                 eq_ref, ek_ref, o_ref,
                 qa_ref, kf_ref, s_ref, tmax_ref, m_ref, l_ref, acc_ref):
    i = pl.program_id(2)
    tq = q_ref.shape[0]
    nblk = tq // LANES

    @pl.when(i == 0)
    def _():
        kf_ref[...] = _dot(gk_ref[...], ek_ref[...]).astype(BF16)

    qa = jnp.concatenate(
        [q_ref[...], _dot(gq_ref[...], eq_ref[...]).astype(BF16)], axis=1)
    qa_ref[...] = qa

    kma = jnp.concatenate(
        [km_ref[...], _dot(gm_ref[...], ek_ref[...]).astype(BF16)], axis=1)
    sm = _dot_nt(qa, kma)
    m0 = jnp.max(sm, axis=1, keepdims=True)
    pm = jnp.exp2(sm - m0)
    m_ref[...] = jnp.broadcast_to(m0, (tq, LANES))
    lane = lax.broadcasted_iota(jnp.int32, (tq, LANES), 1)
    l_ref[...] = jnp.where(lane == 0, jnp.sum(pm, axis=1, keepdims=True), 0.0)
    acc_ref[...] = _dot(pm.astype(BF16), vm_ref[...])

    def stage_a(j, slot, diag_offset):
        off = pl.multiple_of(j * tq, tq)
        ka = jnp.concatenate([k_ref[pl.ds(off, tq), :], kf_ref[pl.ds(off, tq), :]], axis=1)
        s = _dot_nt(qa_ref[...], ka)
        if diag_offset is not None:
            row = lax.broadcasted_iota(jnp.int32, (tq, tq), 0)
            col = lax.broadcasted_iota(jnp.int32, (tq, tq), 1)
            s = jnp.where(col + diag_offset <= row, s, NEG_BIG)
        s_ref[slot] = s
        t = s[:, 0:LANES]
        for c in range(1, nblk):
            t = jnp.maximum(t, s[:, c * LANES:(c + 1) * LANES])
        tmax_ref[slot] = jnp.broadcast_to(jnp.max(t, axis=1, keepdims=True), (tq, LANES))

    def stage_b(j, slot):
        off = pl.multiple_of(j * tq, tq)
        m_old = m_ref[...]
        m_new = jnp.maximum(m_old, tmax_ref[slot])
        alpha = jnp.exp2(m_old - m_new)
        l_new = alpha * l_ref[...]
        ps = []
        for c in range(nblk):
            p = jnp.exp2(s_ref[slot, :, c * LANES:(c + 1) * LANES] - m_new)
            l_new = l_new + p
            ps.append(p.astype(BF16))
        pv = _dot(jnp.concatenate(ps, axis=1), v_ref[pl.ds(off, tq), :])
        acc_ref[...] = alpha * acc_ref[...] + pv
        l_ref[...] = l_new
        m_ref[...] = m_new

    stage_a(0, 0, (0 - i) * tq)

    def body(j, carry):
        slot = lax.rem(j, 2)
        stage_b(j, slot)
        stage_a(j + 1, 1 - slot, None)
        return carry

    lax.fori_loop(0, jnp.maximum(i - 1, 0), body, 0)

    @pl.when(i >= 1)
    def _():
        slot = lax.rem(i - 1, 2)
        stage_b(i - 1, slot)
        stage_a(i, 1 - slot, 0)

    stage_b(i, lax.rem(i, 2))
    l = jnp.sum(l_ref[...], axis=1, keepdims=True)
    o_ref[...] = (acc_ref[...] / l * _silu(z_ref[...].astype(F32))).astype(BF16)


def _attn_call(proj, k_meta, v_meta, gate, gate_meta, eq, ek, batch, seq, tq=512):
    nq = seq // tq
    r = batch * seq
    return pl.pallas_call(
        _attn_kernel,
        grid=(batch, HEADS, nq),
        in_specs=[
            pl.BlockSpec((None, tq, HEAD_DIM), lambda b, h, i: (0, b * nq + i, h)),
            pl.BlockSpec((None, seq, HEAD_DIM), lambda b, h, i: (1, b, h)),
            pl.BlockSpec((None, seq, HEAD_DIM), lambda b, h, i: (2, b, h)),
            pl.BlockSpec((None, tq, HEAD_DIM), lambda b, h, i: (3, b * nq + i, h)),
            pl.BlockSpec((N_META, HEAD_DIM), lambda b, h, i: (0, h)),
            pl.BlockSpec((N_META, HEAD_DIM), lambda b, h, i: (0, h)),
            pl.BlockSpec((tq, LANES), lambda b, h, i: (b * nq + i, 0)),
            pl.BlockSpec((seq, LANES), lambda b, h, i: (b, 0)),
            pl.BlockSpec((N_META, LANES), lambda b, h, i: (0, 0)),
            pl.BlockSpec((None, LANES, LANES), lambda b, h, i: (h, 0, 0)),
            pl.BlockSpec((None, LANES, LANES), lambda b, h, i: (h, 0, 0)),
        ],
        out_specs=pl.BlockSpec((tq, HEAD_DIM), lambda b, h, i: (b * nq + i, h)),
        out_shape=jax.ShapeDtypeStruct((r, D_MODEL), BF16),
        scratch_shapes=[
            pltpu.VMEM((tq, 2 * LANES), BF16),
            pltpu.VMEM((seq, LANES), BF16),
            pltpu.VMEM((2, tq, tq), F32),
            pltpu.VMEM((2, tq, LANES), F32),
            pltpu.VMEM((tq, LANES), F32),
            pltpu.VMEM((tq, LANES), F32),
            pltpu.VMEM((tq, HEAD_DIM), F32),
        ],
        compiler_params=pltpu.CompilerParams(
            dimension_semantics=("arbitrary", "arbitrary", "arbitrary"),
            vmem_limit_bytes=VMEM_LIMIT),
        name="fox_attention",
    )(proj, proj, proj, proj, k_meta, v_meta, gate, gate, gate_meta, eq, ek)


def _out_kernel(aa_ref, ac_ref, ma_ref, mc_ref, x_ref, w_ref, gain_ref, o_ref):
    ya = _dot(aa_ref[...], w_ref[0])
    yc = _dot(ac_ref[...], w_ref[1])
    merged = (jax.nn.sigmoid(ma_ref[...].astype(F32)) * ya
              + jax.nn.sigmoid(mc_ref[...].astype(F32)) * yc)
    y = x_ref[...] + _dot(merged.astype(BF16), w_ref[2])
    o_ref[...] = _rmsnorm(y, gain_ref[...])


def _out_call(a_att, a_conv, proj, x2d, w_o, gain, tm=256):
    r = x2d.shape[0]
    row = lambda i: (i, 0)
    return pl.pallas_call(
        _out_kernel,
        grid=(r // tm,),
        in_specs=[
            pl.BlockSpec((tm, D_MODEL), row),
            pl.BlockSpec((tm, D_MODEL), row),
            pl.BlockSpec((None, tm, D_MODEL), lambda i: (4, i, 0)),
            pl.BlockSpec((None, tm, D_MODEL), lambda i: (5, i, 0)),
            pl.BlockSpec((tm, D_MODEL), row),
            pl.BlockSpec((3, D_MODEL, D_MODEL), lambda i: (0, 0, 0),
                         pipeline_mode=pl.Buffered(1)),
            pl.BlockSpec((1, D_MODEL), lambda i: (0, 0)),
        ],
        out_specs=pl.BlockSpec((tm, D_MODEL), row),
        out_shape=jax.ShapeDtypeStruct((r, D_MODEL), F32),
        compiler_params=pltpu.CompilerParams(
            dimension_semantics=("arbitrary",), vmem_limit_bytes=VMEM_LIMIT),
        name="out_proj",
    )(a_att, a_conv, proj, proj, x2d, w_o, gain)


def kernel(x, meta_tokens, norm_gain, w_in, b_f, conv_w, w_att_o, w_conv_o, w_out, final_gain):
    batch, seq, _ = x.shape
    assert norm_gain.shape[0] == 1, "single-layer block"
    assert w_in.shape == (1, D_MODEL, N_IN) and w_in.dtype == F32
    wt = jnp.transpose(w_in[0])
    bf = jnp.pad(b_f[0].astype(F32), (0, LANES - HEADS)).reshape(1, LANES)
    w_o = jnp.stack([w_att_o[0], w_conv_o[0], w_out[0]]).astype(BF16)
    gain = norm_gain[0].reshape(1, D_MODEL).astype(F32)
    fgain = final_gain.reshape(1, D_MODEL).astype(F32)
    place, ones, eq, ek = _feature_constants()

    k_meta, v_meta, g_meta, cm, gate_meta = _meta_call(
        meta_tokens.astype(F32), gain, wt, bf, place, ones)
    h, gate = _norm_call(x, gain, wt, bf, cm, place, ones)
    proj = _proj_call(h, wt)
    a_conv = _conv_call(h, wt, g_meta, conv_w[0].astype(F32), seq)
    a_att = _attn_call(proj, k_meta, v_meta, gate, gate_meta, eq, ek, batch, seq)
    out = _out_call(a_att, a_conv, proj, x.reshape(batch * seq, D_MODEL), w_o, fgain)
    return out.reshape(batch, seq, D_MODEL)
```

```python
import functools

import numpy as np

import jax
import jax.numpy as jnp
from jax import lax
from jax.experimental import pallas as pl
from jax.experimental.pallas import tpu as pltpu

D_MODEL = 2048
N_META = 16
HEADS = 16
HEAD_DIM = 128
CONV_K = 3
EPS = 1e-6
LANES = 128
SUBLANES = 8
NEG_BIG = -1e30
LOG2E = 1.4426950408889634
GATE_LANES = LANES // HEADS
VMEM_LIMIT = 56 * 1024 * 1024

_GROUPS = ("q", "k", "v", "f", "z_att", "u", "gate_b", "gate_c", "z_conv", "m_att", "m_conv")
_WIDTHS = (D_MODEL, D_MODEL, D_MODEL, HEADS) + (D_MODEL,) * 7
COL = {name: sum(_WIDTHS[:n]) for n, name in enumerate(_GROUPS)}
N_IN = sum(_WIDTHS)

F32 = jnp.float32
BF16 = jnp.bfloat16


def _dot(a, b):
    return jnp.dot(a, b, preferred_element_type=F32)


def _dot_nt(a, b):
    return lax.dot_general(a, b, (((1,), (1,)), ((), ())), preferred_element_type=F32)


def _rmsnorm(x, gain):
    return x * lax.rsqrt(jnp.mean(x * x, axis=-1, keepdims=True) + EPS) * gain


def _log_sigmoid(x):
    return jnp.minimum(x, 0.0) - jnp.log1p(jnp.exp(-jnp.abs(x)))


def _silu(x):
    return x * jax.nn.sigmoid(x)


def _split3(x):
    hi = x.astype(BF16)
    r1 = x - hi.astype(F32)
    mid = r1.astype(BF16)
    lo = (r1 - mid.astype(F32)).astype(BF16)
    return hi, mid, lo


def _cumsum_rows(x):
    n = x.shape[0]
    tri = (lax.broadcasted_iota(jnp.int32, (n, n), 0)
           >= lax.broadcasted_iota(jnp.int32, (n, n), 1)).astype(BF16)
    hi, mid, lo = _split3(x)
    return _dot(tri, hi) + _dot(tri, mid) + _dot(tri, lo)


def _gate_features(c, place_ref, ones_ref):
    hi, mid, lo = _split3(c * LOG2E)
    cat = jnp.concatenate([hi, mid, lo], axis=1)
    return (_dot(cat, place_ref[...]) + ones_ref[...]).astype(BF16)


def _feature_constants():
    place = np.zeros((3 * LANES, LANES), np.float32)
    ones = np.zeros((1, LANES), np.float32)
    eq = np.zeros((HEADS, LANES, LANES), np.float32)
    ek = np.zeros((HEADS, LANES, LANES), np.float32)
    for h in range(HEADS):
        base = GATE_LANES * h
        ones[0, base + 3] = 1.0
        for f in range(3):
            place[f * LANES + h, base + f] = 1.0
            eq[h, base + f, f] = 1.0
            eq[h, base + 3, 3 + f] = 1.0
            ek[h, base + 3, f] = 1.0
            ek[h, base + f, 3 + f] = -1.0
    return (jnp.asarray(place, BF16), jnp.asarray(ones, F32),
            jnp.asarray(eq, BF16), jnp.asarray(ek, BF16))


def _w_tile_spec(tn, row_of):
    assert all(c % HEADS == 0 for c in COL.values()) and tn % HEADS == 0
    return pl.BlockSpec((pl.Element(tn), pl.Element(D_MODEL)),
                        lambda *ids: (pl.multiple_of(row_of(*ids), HEADS), 0))


def _forget_w_spec():
    assert COL["f"] % LANES == 0
    return pl.BlockSpec((LANES, D_MODEL), lambda *ids: (COL["f"] // LANES, 0))


def _stage_weight(src_ref, dst_ref, chunk=128):
    for r in range(0, src_ref.shape[0], chunk):
        dst_ref[r:r + chunk, :] = src_ref[r:r + chunk, :].astype(BF16)


def _meta_kernel(meta_ref, gain_ref, wk_ref, wv_ref, wu_ref, wgc_ref, wf_ref, bf_ref,
                 place_ref, ones_ref, k_ref, v_ref, g_ref, cm_ref, gm_ref):
    h = _rmsnorm(meta_ref[...], gain_ref[...]).astype(BF16)
    proj = lambda w_ref: _dot_nt(h, w_ref[...].astype(BF16))
    k_ref[...] = proj(wk_ref).astype(BF16)
    v_ref[...] = proj(wv_ref).astype(BF16)
    g_ref[...] = proj(wgc_ref) * proj(wu_ref)
    cm = _cumsum_rows(_log_sigmoid(proj(wf_ref) + bf_ref[...]))
    cm_ref[...] = cm
    gm_ref[...] = _gate_features(cm, place_ref, ones_ref)


def _meta_call(meta, gain, wt, bf, place, ones, tn=512):
    nt = D_MODEL // tn
    const = lambda shape: pl.BlockSpec(shape, lambda j: (0,) * len(shape))
    group = lambda name: _w_tile_spec(tn, lambda j: COL[name] + j * tn)
    return pl.pallas_call(
        _meta_kernel,
        grid=(nt,),
        in_specs=[
            const((N_META, D_MODEL)), const((1, D_MODEL)),
            group("k"), group("v"), group("u"), group("gate_c"), _forget_w_spec(),
            const((1, LANES)), const((3 * LANES, LANES)), const((1, LANES)),
        ],
        out_specs=[
            pl.BlockSpec((N_META, tn), lambda j: (0, j)),
            pl.BlockSpec((N_META, tn), lambda j: (0, j)),
            pl.BlockSpec((N_META, tn), lambda j: (0, j)),
            const((N_META, LANES)),
            const((N_META, LANES)),
        ],
        out_shape=[
            jax.ShapeDtypeStruct((N_META, D_MODEL), BF16),
            jax.ShapeDtypeStruct((N_META, D_MODEL), BF16),
            jax.ShapeDtypeStruct((N_META, D_MODEL), F32),
            jax.ShapeDtypeStruct((N_META, LANES), F32),
            jax.ShapeDtypeStruct((N_META, LANES), BF16),
        ],
        compiler_params=pltpu.CompilerParams(
            dimension_semantics=("arbitrary",), vmem_limit_bytes=VMEM_LIMIT),
        name="meta_proj",
    )(meta, gain, wt, wt, wt, wt, wt, bf, place, ones)


def _norm_kernel(x_ref, gain_ref, wf_ref, bf_ref, cm_ref, place_ref, ones_ref,
                 h_ref, gate_ref, carry_ref, *, chunk):
    @pl.when(pl.program_id(1) == 0)
    def _():
        carry_ref[...] = cm_ref[N_META - 1:N_META, :]

    h = _rmsnorm(x_ref[...], gain_ref[...]).astype(BF16)
    h_ref[...] = h
    logf = _log_sigmoid(_dot_nt(h, wf_ref[...].astype(BF16)) + bf_ref[...])
    tm = logf.shape[0]
    run = carry_ref[...]
    parts = []
    for j in range(tm // chunk):
        c = _cumsum_rows(logf[j * chunk:(j + 1) * chunk]) + run
        run = c[chunk - 1:chunk, :]
        parts.append(c)
    carry_ref[...] = run
    gate_ref[...] = _gate_features(jnp.concatenate(parts, axis=0), place_ref, ones_ref)


def _norm_call(x, gain, wt, bf, cm, place, ones, tm=512, chunk=256):
    b, s, _ = x.shape
    nt = s // tm
    const = lambda shape: pl.BlockSpec(shape, lambda bi, i: (0,) * len(shape))
    return pl.pallas_call(
        functools.partial(_norm_kernel, chunk=chunk),
        grid=(b, nt),
        in_specs=[
            pl.BlockSpec((None, tm, D_MODEL), lambda bi, i: (bi, i, 0)),
            const((1, D_MODEL)),
            _forget_w_spec(),
            const((1, LANES)),
            const((N_META, LANES)),
            const((3 * LANES, LANES)),
            const((1, LANES)),
        ],
        out_specs=[
            pl.BlockSpec((tm, D_MODEL), lambda bi, i: (bi * nt + i, 0)),
            pl.BlockSpec((tm, LANES), lambda bi, i: (bi * nt + i, 0)),
        ],
        out_shape=[
            jax.ShapeDtypeStruct((b * s, D_MODEL), BF16),
            jax.ShapeDtypeStruct((b * s, LANES), BF16),
        ],
        scratch_shapes=[pltpu.VMEM((1, LANES), F32)],
        compiler_params=pltpu.CompilerParams(
            dimension_semantics=("arbitrary", "arbitrary"), vmem_limit_bytes=VMEM_LIMIT),
        name="norm_forget",
    )(x, gain, wt, bf, cm, place, ones)


_PROJ_GROUPS = ("q", "k", "v", "z_att", "m_att", "m_conv")


def _proj_group_row(g):
    row = jnp.int32(COL[_PROJ_GROUPS[0]])
    for n in range(1, len(_PROJ_GROUPS)):
        row = jnp.where(g == n, COL[_PROJ_GROUPS[n]], row)
    return row


def _proj_kernel(h_ref, w_ref, o_ref, w_s, *, row_chunk):
    @pl.when(pl.program_id(2) == 0)
    def _():
        _stage_weight(w_ref, w_s)

    scale = jnp.where(pl.program_id(0) == 0, HEAD_DIM ** -0.5 * LOG2E, 1.0).astype(F32)
    for r in range(0, h_ref.shape[0], row_chunk):
        y = _dot_nt(h_ref[r:r + row_chunk, :], w_s[...])
        o_ref[r:r + row_chunk, :] = (y * scale).astype(BF16)


def _proj_call(h, wt, tm=2048, tn=512, row_chunk=512):
    r = h.shape[0]
    ng = len(_PROJ_GROUPS)
    return pl.pallas_call(
        functools.partial(_proj_kernel, row_chunk=row_chunk),
        grid=(ng, D_MODEL // tn, r // tm),
        in_specs=[
            pl.BlockSpec((tm, D_MODEL), lambda g, j, i: (i, 0)),
            _w_tile_spec(tn, lambda g, j, i: _proj_group_row(g) + j * tn),
        ],
        out_specs=pl.BlockSpec((None, tm, tn), lambda g, j, i: (g, i, j)),
        out_shape=jax.ShapeDtypeStruct((ng, r, D_MODEL), BF16),
        scratch_shapes=[pltpu.VMEM((tn, D_MODEL), BF16)],
        compiler_params=pltpu.CompilerParams(
            dimension_semantics=("arbitrary", "arbitrary", "arbitrary"),
            vmem_limit_bytes=VMEM_LIMIT),
        name="plain_proj",
    )(h, wt)


_CONV_GROUPS = ("u", "gate_b", "gate_c", "z_conv")


def _conv_kernel(h_ref, wu_ref, wgb_ref, wgc_ref, wzc_ref, gm_ref, cw_ref, o_ref,
                 w_s, gbuf_ref, *, tiles_per_batch):
    i = pl.program_id(1)
    tm = h_ref.shape[0]

    @pl.when(i == 0)
    def _():
        for n, w_ref in enumerate((wu_ref, wgb_ref, wgc_ref, wzc_ref)):
            _stage_weight(w_ref, w_s.at[n])

    @pl.when(i % tiles_per_batch == 0)
    def _():
        gbuf_ref[0:SUBLANES, :] = gm_ref[N_META - SUBLANES:N_META, :]

    h = h_ref[...]
    u = _dot_nt(h, w_s[0])
    gc = _dot_nt(h, w_s[2])
    g = gc * u
    gbuf_ref[SUBLANES:SUBLANES + tm, :] = g
    conv = (gbuf_ref[SUBLANES - 2:SUBLANES - 2 + tm, :] * cw_ref[0:1, :]
            + gbuf_ref[SUBLANES - 1:SUBLANES - 1 + tm, :] * cw_ref[1:2, :]
            + g * cw_ref[2:3, :])
    gbuf_ref[0:SUBLANES, :] = g[tm - SUBLANES:tm, :]
    gb = _dot_nt(h, w_s[1])
    zc = _dot_nt(h, w_s[3])
    o_ref[...] = (gb * conv * _silu(zc)).astype(BF16)


def _conv_call(h, wt, g_meta, conv_w, rows_per_batch, tm=1024, tc=256):
    r = h.shape[0]
    w_specs = [_w_tile_spec(tc, lambda j, i, name=name: COL[name] + j * tc)
               for name in _CONV_GROUPS]
    return pl.pallas_call(
        functools.partial(_conv_kernel, tiles_per_batch=rows_per_batch // tm),
        grid=(D_MODEL // tc, r // tm),
        in_specs=[pl.BlockSpec((tm, D_MODEL), lambda j, i: (i, 0))] + w_specs + [
            pl.BlockSpec((N_META, tc), lambda j, i: (0, j)),
            pl.BlockSpec((CONV_K, tc), lambda j, i: (0, j)),
        ],
        out_specs=pl.BlockSpec((tm, tc), lambda j, i: (i, j)),
        out_shape=jax.ShapeDtypeStruct((r, D_MODEL), BF16),
        scratch_shapes=[pltpu.VMEM((len(_CONV_GROUPS), tc, D_MODEL), BF16),
                        pltpu.VMEM((SUBLANES + tm, tc), F32)],
        compiler_params=pltpu.CompilerParams(
            dimension_semantics=("arbitrary", "arbitrary"), vmem_limit_bytes=VMEM_LIMIT),
        name="conv_proj",
    )(h, *([wt] * len(_CONV_GROUPS)), g_meta, conv_w)


def _attn_kernel(q_ref, k_ref, v_ref, z_ref, km_ref, vm_ref, g_ref, gm_ref, eq_ref, ek_ref,
                 o_ref, qa_ref, kf_ref, tri_ref, s_ref, tmax_ref, m_ref, l_ref, acc_ref,
                 *, tq, tk, ts, rb):
    seq = q_ref.shape[0]
    nq = seq // tq
    assert tq == 2 * tk
    nblk = tk // LANES

    tri_ref[...] = jnp.where(
        lax.broadcasted_iota(jnp.int32, (tk, tk), 1) <= lax.broadcasted_iota(jnp.int32, (tk, tk), 0),
        0.0, NEG_BIG)
    kma = jnp.concatenate(
        [km_ref[...], _dot(gm_ref[...], ek_ref[...]).astype(BF16)], axis=1)
    lane = lax.broadcasted_iota(jnp.int32, (ts, LANES), 1)

    def setup(t, carry):
        rows = pl.ds(pl.multiple_of(t * ts, ts), ts)
        g = g_ref[rows, :]
        kf_ref[rows, :] = _dot(g, ek_ref[...]).astype(BF16)
        qa = jnp.concatenate([q_ref[rows, :], _dot(g, eq_ref[...]).astype(BF16)], axis=1)
        qa_ref[rows, :] = qa
        sm = _dot_nt(qa, kma)
        m0 = jnp.max(sm, axis=1, keepdims=True)
        pm = jnp.exp2(sm - m0)
        m_ref[rows, :] = jnp.broadcast_to(m0, (ts, LANES))
        l_ref[rows, :] = jnp.where(lane == 0, jnp.sum(pm, axis=1, keepdims=True), 0.0)
        acc_ref[rows, :] = _dot(pm.astype(BF16), vm_ref[...])
        return carry

    lax.fori_loop(0, seq // ts, setup, 0, unroll=4)

    def stage_a(row0, nrows, j, slot, tri):
        rows = pl.ds(pl.multiple_of(row0, tk), nrows)
        keys = pl.ds(pl.multiple_of(j * tk, tk), tk)
        ka = jnp.concatenate([k_ref[keys, :], kf_ref[keys, :]], axis=1)
        s = _dot_nt(qa_ref[rows, :], ka)
        for r0 in range(0, nrows, rb):
            part = s[r0:r0 + rb]
            if tri and r0 < tk:
                part = part + tri_ref[r0:r0 + rb, :]
            s_ref[slot, r0:r0 + rb, :] = part
            t = part[:, 0:LANES]
            for c in range(1, nblk):
                t = jnp.maximum(t, part[:, c * LANES:(c + 1) * LANES])
            tmax_ref[slot, r0:r0 + rb, :] = jnp.broadcast_to(
                jnp.max(t, axis=1, keepdims=True), (rb, LANES))

    def stage_b(row0, nrows, j, slot):
        keys = pl.ds(pl.multiple_of(j * tk, tk), tk)
        row0 = pl.multiple_of(row0, tk)
        alphas, ps = [], []
        for r0 in range(0, nrows, rb):
            rows = pl.ds(row0 + r0, rb)
            m_old = m_ref[rows, :]
            m_new = jnp.maximum(m_old, tmax_ref[slot, r0:r0 + rb, :])
            alpha = jnp.exp2(m_old - m_new)
            l_new = alpha * l_ref[rows, :]
            pr = []
            for c in range(nblk):
                p = jnp.exp2(s_ref[slot, r0:r0 + rb, c * LANES:(c + 1) * LANES] - m_new)
                l_new = l_new + p
                pr.append(p.astype(BF16))
            l_ref[rows, :] = l_new
            m_ref[rows, :] = m_new
            alphas.append(alpha)
            ps.append(jnp.concatenate(pr, axis=1))
        pv = _dot(jnp.concatenate(ps, axis=0), v_ref[keys, :])
        for n, r0 in enumerate(range(0, nrows, rb)):
            rows = pl.ds(row0 + r0, rb)
            acc_ref[rows, :] = alphas[n] * acc_ref[rows, :] + pv[r0:r0 + rb]

    stage_a(0, tq, 0, 0, True)

    def pair(item_b, item_a, tri, slot):
        stage_a(*item_a, 1 - slot, tri)
        stage_b(*item_b, slot)

    def tile(i, carry):
        full = 2 * i
        top = i * tq
        low = top + tk

        def body(t, c):
            j = 2 * t
            pair((top, tq, j), (top, tq, j + 1), False, 0)
            pair((top, tq, j + 1), (top, tq, j + 2), False, 1)
            return c

        lax.fori_loop(0, jnp.maximum(i - 1, 0), body, 0)

        @pl.when(i >= 1)
        def _():
            pair((top, tq, full - 2), (top, tq, full - 1), False, 0)
            pair((top, tq, full - 1), (top, tq, full), True, 1)

        pair((top, tq, full), (low, tk, full + 1), True, 0)
        pair((low, tk, full + 1), (jnp.minimum(i + 1, nq - 1) * tq, tq, 0), False, 1)
        return carry

    lax.fori_loop(0, nq, tile, 0)

    def finalize(t, carry):
        rows = pl.ds(pl.multiple_of(t * ts, ts), ts)
        l = jnp.sum(l_ref[rows, :], axis=1, keepdims=True)
        o = acc_ref[rows, :] / l * _silu(z_ref[rows, :].astype(F32))
        o_ref[rows, :] = o.astype(BF16)
        return carry

    lax.fori_loop(0, seq // ts, finalize, 0)


def _attn_call(proj, k_meta, v_meta, gate, gate_meta, eq, ek, batch, seq, tq=1024, tk=512, ts=512, rb=128):
    r = batch * seq
    head = lambda g: pl.BlockSpec((None, seq, HEAD_DIM), lambda b, h, g=g: (g, b, h))
    return pl.pallas_call(
        functools.partial(_attn_kernel, tq=tq, tk=tk, ts=ts, rb=rb),
        grid=(batch, HEADS),
        in_specs=[
            head(0), head(1), head(2), head(3),
            pl.BlockSpec((N_META, HEAD_DIM), lambda b, h: (0, h)),
            pl.BlockSpec((N_META, HEAD_DIM), lambda b, h: (0, h)),
            pl.BlockSpec((seq, LANES), lambda b, h: (b, 0)),
            pl.BlockSpec((N_META, LANES), lambda b, h: (0, 0)),
            pl.BlockSpec((None, LANES, LANES), lambda b, h: (h, 0, 0)),
            pl.BlockSpec((None, LANES, LANES), lambda b, h: (h, 0, 0)),
        ],
        out_specs=pl.BlockSpec((seq, HEAD_DIM), lambda b, h: (b, h)),
        out_shape=jax.ShapeDtypeStruct((r, D_MODEL), BF16),
        scratch_shapes=[
            pltpu.VMEM((seq, 2 * LANES), BF16),
            pltpu.VMEM((seq, LANES), BF16),
            pltpu.VMEM((tk, tk), F32),
            pltpu.VMEM((2, tq, tk), F32),
            pltpu.VMEM((2, tq, LANES), F32),
            pltpu.VMEM((seq, LANES), F32),
            pltpu.VMEM((seq, LANES), F32),
            pltpu.VMEM((seq, HEAD_DIM), F32),
        ],
        compiler_params=pltpu.CompilerParams(
            dimension_semantics=("arbitrary", "arbitrary"), vmem_limit_bytes=VMEM_LIMIT),
        name="fox_attention",
    )(proj, proj, proj, proj, k_meta, v_meta, gate, gate_meta, eq, ek)


def _out_kernel(aa_ref, ac_ref, ma_ref, mc_ref, x_ref, w_ref, gain_ref, o_ref):
    ya = _dot(aa_ref[...], w_ref[0])
    yc = _dot(ac_ref[...], w_ref[1])
    merged = (jax.nn.sigmoid(ma_ref[...].astype(F32)) * ya
              + jax.nn.sigmoid(mc_ref[...].astype(F32)) * yc)
    y = x_ref[...] + _dot(merged.astype(BF16), w_ref[2])
    o_ref[...] = _rmsnorm(y, gain_ref[...])


def _out_call(a_att, a_conv, proj, x2d, w_o, gain, tm=256):
    r = x2d.shape[0]
    row = lambda i: (i, 0)
    return pl.pallas_call(
        _out_kernel,
        grid=(r // tm,),
        in_specs=[
            pl.BlockSpec((tm, D_MODEL), row),
            pl.BlockSpec((tm, D_MODEL), row),
            pl.BlockSpec((None, tm, D_MODEL), lambda i: (4, i, 0)),
            pl.BlockSpec((None, tm, D_MODEL), lambda i: (5, i, 0)),
            pl.BlockSpec((tm, D_MODEL), row),
            pl.BlockSpec((3, D_MODEL, D_MODEL), lambda i: (0, 0, 0),
                         pipeline_mode=pl.Buffered(1)),
            pl.BlockSpec((1, D_MODEL), lambda i: (0, 0)),
        ],
        out_specs=pl.BlockSpec((tm, D_MODEL), row),
        out_shape=jax.ShapeDtypeStruct((r, D_MODEL), F32),
        compiler_params=pltpu.CompilerParams(
            dimension_semantics=("arbitrary",), vmem_limit_bytes=VMEM_LIMIT),
        name="out_proj",
    )(a_att, a_conv, proj, proj, x2d, w_o, gain)


def kernel(x, meta_tokens, norm_gain, w_in, b_f, conv_w, w_att_o, w_conv_o, w_out, final_gain):
    batch, seq, _ = x.shape
    assert norm_gain.shape[0] == 1, "single-layer block"
    assert w_in.shape == (1, D_MODEL, N_IN) and w_in.dtype == F32
    wt = jnp.transpose(w_in[0])
    bf = jnp.pad(b_f[0].astype(F32), (0, LANES - HEADS)).reshape(1, LANES)
    w_o = jnp.stack([w_att_o[0], w_conv_o[0], w_out[0]]).astype(BF16)
    gain = norm_gain[0].reshape(1, D_MODEL).astype(F32)
    fgain = final_gain.reshape(1, D_MODEL).astype(F32)
    place, ones, eq, ek = _feature_constants()

    k_meta, v_meta, g_meta, cm, gate_meta = _meta_call(
        meta_tokens.astype(F32), gain, wt, bf, place, ones)
    h, gate = _norm_call(x, gain, wt, bf, cm, place, ones)
    proj = _proj_call(h, wt)
    a_conv = _conv_call(h, wt, g_meta, conv_w[0].astype(F32), seq)
    a_att = _attn_call(proj, k_meta, v_meta, gate, gate_meta, eq, ek, batch, seq)
    out = _out_call(a_att, a_conv, proj, x.reshape(batch * seq, D_MODEL), w_o, fgain)
    return out.reshape(batch, seq, D_MODEL)
```

```python
import functools

import numpy as np

import jax
import jax.numpy as jnp
from jax import lax
from jax.experimental import pallas as pl
from jax.experimental.pallas import tpu as pltpu

D_MODEL = 2048
N_META = 16
HEADS = 16
HEAD_DIM = 128
CONV_K = 3
EPS = 1e-6
LANES = 128
SUBLANES = 8
NEG_BIG = -1e30
LOG2E = 1.4426950408889634
GATE_LANES = LANES // HEADS
VMEM_LIMIT = 56 * 1024 * 1024

_GROUPS = ("q", "k", "v", "f", "z_att", "u", "gate_b", "gate_c", "z_conv", "m_att", "m_conv")
_WIDTHS = (D_MODEL, D_MODEL, D_MODEL, HEADS) + (D_MODEL,) * 7
COL = {name: sum(_WIDTHS[:n]) for n, name in enumerate(_GROUPS)}
N_IN = sum(_WIDTHS)

F32 = jnp.float32
BF16 = jnp.bfloat16


def _dot(a, b):
    return jnp.dot(a, b, preferred_element_type=F32)


def _dot_nt(a, b):
    return lax.dot_general(a, b, (((1,), (1,)), ((), ())), preferred_element_type=F32)


def _rmsnorm(x, gain):
    return x * lax.rsqrt(jnp.mean(x * x, axis=-1, keepdims=True) + EPS) * gain


def _log_sigmoid(x):
    return jnp.minimum(x, 0.0) - jnp.log1p(jnp.exp(-jnp.abs(x)))


def _silu(x):
    return x * jax.nn.sigmoid(x)


def _split3(x):
    hi = x.astype(BF16)
    r1 = x - hi.astype(F32)
    mid = r1.astype(BF16)
    lo = (r1 - mid.astype(F32)).astype(BF16)
    return hi, mid, lo


def _cumsum_rows(x):
    n = x.shape[0]
    tri = (lax.broadcasted_iota(jnp.int32, (n, n), 0)
           >= lax.broadcasted_iota(jnp.int32, (n, n), 1)).astype(BF16)
    hi, mid, lo = _split3(x)
    return _dot(tri, hi) + _dot(tri, mid) + _dot(tri, lo)


def _gate_features(c, place_ref, ones_ref):
    hi, mid, lo = _split3(c * LOG2E)
    cat = jnp.concatenate([hi, mid, lo], axis=1)
    return (_dot(cat, place_ref[...]) + ones_ref[...]).astype(BF16)


def _feature_constants():
    place = np.zeros((3 * LANES, LANES), np.float32)
    ones = np.zeros((1, LANES), np.float32)
    eq = np.zeros((HEADS, LANES, LANES), np.float32)
    ek = np.zeros((HEADS, LANES, LANES), np.float32)
    for h in range(HEADS):
        base = GATE_LANES * h
        ones[0, base + 3] = 1.0
        for f in range(3):
            place[f * LANES + h, base + f] = 1.0
            eq[h, base + f, f] = 1.0
            eq[h, base + 3, 3 + f] = 1.0
            ek[h, base + 3, f] = 1.0
            ek[h, base + f, 3 + f] = -1.0
    return (jnp.asarray(place, BF16), jnp.asarray(ones, F32),
            jnp.asarray(np.concatenate([eq, ek], axis=2), BF16))


def _w_tile_spec(tn, row_of):
    assert all(c % HEADS == 0 for c in COL.values()) and tn % HEADS == 0
    return pl.BlockSpec((pl.Element(tn), pl.Element(D_MODEL)),
                        lambda *ids: (pl.multiple_of(row_of(*ids), HEADS), 0))


def _forget_w_spec():
    assert COL["f"] % LANES == 0
    return pl.BlockSpec((LANES, D_MODEL), lambda *ids: (COL["f"] // LANES, 0))


def _stage_weight(src_ref, dst_ref, chunk=128):
    for r in range(0, src_ref.shape[0], chunk):
        dst_ref[r:r + chunk, :] = src_ref[r:r + chunk, :].astype(BF16)


def _meta_kernel(meta_ref, gain_ref, wk_ref, wv_ref, wu_ref, wgc_ref, wf_ref, bf_ref,
                 place_ref, ones_ref, k_ref, v_ref, g_ref, cm_ref, gm_ref):
    h = _rmsnorm(meta_ref[...], gain_ref[...]).astype(BF16)
    proj = lambda w_ref: _dot_nt(h, w_ref[...].astype(BF16))
    k_ref[...] = proj(wk_ref).astype(BF16)
    v_ref[...] = proj(wv_ref).astype(BF16)
    g_ref[...] = proj(wgc_ref) * proj(wu_ref)
    cm = _cumsum_rows(_log_sigmoid(proj(wf_ref) + bf_ref[...]))
    cm_ref[...] = cm
    gm_ref[...] = _gate_features(cm, place_ref, ones_ref)


def _meta_call(meta, gain, wt, bf, place, ones, tn=512):
    nt = D_MODEL // tn
    const = lambda shape: pl.BlockSpec(shape, lambda j: (0,) * len(shape))
    group = lambda name: _w_tile_spec(tn, lambda j: COL[name] + j * tn)
    return pl.pallas_call(
        _meta_kernel,
        grid=(nt,),
        in_specs=[
            const((N_META, D_MODEL)), const((1, D_MODEL)),
            group("k"), group("v"), group("u"), group("gate_c"), _forget_w_spec(),
            const((1, LANES)), const((3 * LANES, LANES)), const((1, LANES)),
        ],
        out_specs=[
            pl.BlockSpec((N_META, tn), lambda j: (0, j)),
            pl.BlockSpec((N_META, tn), lambda j: (0, j)),
            pl.BlockSpec((N_META, tn), lambda j: (0, j)),
            const((N_META, LANES)),
            const((N_META, LANES)),
        ],
        out_shape=[
            jax.ShapeDtypeStruct((N_META, D_MODEL), BF16),
            jax.ShapeDtypeStruct((N_META, D_MODEL), BF16),
            jax.ShapeDtypeStruct((N_META, D_MODEL), F32),
            jax.ShapeDtypeStruct((N_META, LANES), F32),
            jax.ShapeDtypeStruct((N_META, LANES), BF16),
        ],
        compiler_params=pltpu.CompilerParams(
            dimension_semantics=("arbitrary",), vmem_limit_bytes=VMEM_LIMIT),
        name="meta_proj",
    )(meta, gain, wt, wt, wt, wt, wt, bf, place, ones)


def _norm_kernel(x_ref, gain_ref, wf_ref, bf_ref, cm_ref, place_ref, ones_ref,
                 h_ref, gate_ref, carry_ref, *, chunk):
    @pl.when(pl.program_id(1) == 0)
    def _():
        carry_ref[...] = cm_ref[N_META - 1:N_META, :]

    h = _rmsnorm(x_ref[...], gain_ref[...]).astype(BF16)
    h_ref[...] = h
    logf = _log_sigmoid(_dot_nt(h, wf_ref[...].astype(BF16)) + bf_ref[...])
    tm = logf.shape[0]
    run = carry_ref[...]
    parts = []
    for j in range(tm // chunk):
        c = _cumsum_rows(logf[j * chunk:(j + 1) * chunk]) + run
        run = c[chunk - 1:chunk, :]
        parts.append(c)
    carry_ref[...] = run
    gate_ref[...] = _gate_features(jnp.concatenate(parts, axis=0), place_ref, ones_ref)


def _norm_call(x, gain, wt, bf, cm, place, ones, tm=512, chunk=256):
    b, s, _ = x.shape
    nt = s // tm
    const = lambda shape: pl.BlockSpec(shape, lambda bi, i: (0,) * len(shape))
    return pl.pallas_call(
        functools.partial(_norm_kernel, chunk=chunk),
        grid=(b, nt),
        in_specs=[
            pl.BlockSpec((None, tm, D_MODEL), lambda bi, i: (bi, i, 0)),
            const((1, D_MODEL)),
            _forget_w_spec(),
            const((1, LANES)),
            const((N_META, LANES)),
            const((3 * LANES, LANES)),
            const((1, LANES)),
        ],
        out_specs=[
            pl.BlockSpec((tm, D_MODEL), lambda bi, i: (bi * nt + i, 0)),
            pl.BlockSpec((tm, LANES), lambda bi, i: (bi * nt + i, 0)),
        ],
        out_shape=[
            jax.ShapeDtypeStruct((b * s, D_MODEL), BF16),
            jax.ShapeDtypeStruct((b * s, LANES), BF16),
        ],
        scratch_shapes=[pltpu.VMEM((1, LANES), F32)],
        compiler_params=pltpu.CompilerParams(
            dimension_semantics=("arbitrary", "arbitrary"), vmem_limit_bytes=VMEM_LIMIT),
        name="norm_forget",
    )(x, gain, wt, bf, cm, place, ones)


_PROJ_GROUPS = ("q", "k", "v", "z_att", "m_att", "m_conv")


def _proj_group_row(g):
    row = jnp.int32(COL[_PROJ_GROUPS[0]])
    for n in range(1, len(_PROJ_GROUPS)):
        row = jnp.where(g == n, COL[_PROJ_GROUPS[n]], row)
    return row


def _proj_kernel(h_ref, w_ref, o_ref, w_s, *, row_chunk):
    @pl.when(pl.program_id(2) == 0)
    def _():
        _stage_weight(w_ref, w_s)

    scale = jnp.where(pl.program_id(0) == 0, HEAD_DIM ** -0.5 * LOG2E, 1.0).astype(F32)
    for r in range(0, h_ref.shape[0], row_chunk):
        y = _dot_nt(h_ref[r:r + row_chunk, :], w_s[...])
        o_ref[r:r + row_chunk, :] = (y * scale).astype(BF16)


def _proj_call(h, wt, tm=2048, tn=1024, row_chunk=512):
    r = h.shape[0]
    ng = len(_PROJ_GROUPS)
    return pl.pallas_call(
        functools.partial(_proj_kernel, row_chunk=row_chunk),
        grid=(ng, D_MODEL // tn, r // tm),
        in_specs=[
            pl.BlockSpec((tm, D_MODEL), lambda g, j, i: (i, 0)),
            _w_tile_spec(tn, lambda g, j, i: _proj_group_row(g) + j * tn),
        ],
        out_specs=pl.BlockSpec((None, tm, tn), lambda g, j, i: (g, i, j)),
        out_shape=jax.ShapeDtypeStruct((ng, r, D_MODEL), BF16),
        scratch_shapes=[pltpu.VMEM((tn, D_MODEL), BF16)],
        compiler_params=pltpu.CompilerParams(
            dimension_semantics=("arbitrary", "arbitrary", "arbitrary"),
            vmem_limit_bytes=VMEM_LIMIT),
        name="plain_proj",
    )(h, wt)


_CONV_GROUPS = ("u", "gate_b", "gate_c", "z_conv")


def _conv_kernel(h_ref, wu_ref, wgb_ref, wgc_ref, wzc_ref, gm_ref, cw_ref, o_ref,
                 w_s, gbuf_ref, *, tiles_per_batch):
    i = pl.program_id(1)
    tm = h_ref.shape[0]

    @pl.when(i == 0)
    def _():
        for n, w_ref in enumerate((wu_ref, wgb_ref, wgc_ref, wzc_ref)):
            _stage_weight(w_ref, w_s.at[n])

    @pl.when(i % tiles_per_batch == 0)
    def _():
        gbuf_ref[0:SUBLANES, :] = gm_ref[N_META - SUBLANES:N_META, :]

    h = h_ref[...]
    u = _dot_nt(h, w_s[0])
    gc = _dot_nt(h, w_s[2])
    g = gc * u
    gbuf_ref[SUBLANES:SUBLANES + tm, :] = g
    conv = (gbuf_ref[SUBLANES - 2:SUBLANES - 2 + tm, :] * cw_ref[0:1, :]
            + gbuf_ref[SUBLANES - 1:SUBLANES - 1 + tm, :] * cw_ref[1:2, :]
            + g * cw_ref[2:3, :])
    gbuf_ref[0:SUBLANES, :] = g[tm - SUBLANES:tm, :]
    gb = _dot_nt(h, w_s[1])
    zc = _dot_nt(h, w_s[3])
    o_ref[...] = (gb * conv * _silu(zc)).astype(BF16)


def _conv_call(h, wt, g_meta, conv_w, rows_per_batch, tm=2048, tc=256):
    r = h.shape[0]
    w_specs = [_w_tile_spec(tc, lambda j, i, name=name: COL[name] + j * tc)
               for name in _CONV_GROUPS]
    return pl.pallas_call(
        functools.partial(_conv_kernel, tiles_per_batch=rows_per_batch // tm),
        grid=(D_MODEL // tc, r // tm),
        in_specs=[pl.BlockSpec((tm, D_MODEL), lambda j, i: (i, 0))] + w_specs + [
            pl.BlockSpec((N_META, tc), lambda j, i: (0, j)),
            pl.BlockSpec((CONV_K, tc), lambda j, i: (0, j)),
        ],
        out_specs=pl.BlockSpec((tm, tc), lambda j, i: (i, j)),
        out_shape=jax.ShapeDtypeStruct((r, D_MODEL), BF16),
        scratch_shapes=[pltpu.VMEM((len(_CONV_GROUPS), tc, D_MODEL), BF16),
                        pltpu.VMEM((SUBLANES + tm, tc), F32)],
        compiler_params=pltpu.CompilerParams(
            dimension_semantics=("arbitrary", "arbitrary"), vmem_limit_bytes=VMEM_LIMIT),
        name="conv_proj",
    )(h, *([wt] * len(_CONV_GROUPS)), g_meta, conv_w)


def _attn_kernel(q_ref, k_ref, v_ref, z_ref, km_ref, vm_ref, g_ref, gm_ref, eqk_ref,
                 o_ref, qa_ref, kf_ref, va_ref, tri_ref, s_ref, tmax_ref, m_ref, acc_ref,
                 *, tq, tk, ts, rb):
    seq = q_ref.shape[0]
    nq = seq // tq
    assert tq == 2 * tk
    nblk = tk // LANES

    tri_ref[...] = jnp.where(
        lax.broadcasted_iota(jnp.int32, (tk, tk), 1) <= lax.broadcasted_iota(jnp.int32, (tk, tk), 0),
        0.0, NEG_BIG)
    kma = jnp.concatenate(
        [km_ref[...], _dot(gm_ref[...], eqk_ref[:, LANES:]).astype(BF16)], axis=1)
    vma = jnp.concatenate([vm_ref[...], jnp.ones((N_META, LANES), BF16)], axis=1)

    def setup(t, carry):
        rows = pl.ds(pl.multiple_of(t * ts, ts), ts)
        feat = _dot(g_ref[rows, :], eqk_ref[...]).astype(BF16)
        kf_ref[rows, :] = feat[:, LANES:]
        qa = jnp.concatenate([q_ref[rows, :], feat[:, :LANES]], axis=1)
        qa_ref[rows, :] = qa
        va_ref[rows, :] = jnp.concatenate([v_ref[rows, :], jnp.ones((ts, LANES), BF16)], axis=1)
        sm = _dot_nt(qa, kma)
        m0 = jnp.max(sm, axis=1, keepdims=True)
        m_ref[rows, :] = jnp.broadcast_to(m0, (ts, LANES))
        acc_ref[rows, :] = _dot(jnp.exp2(sm - m0).astype(BF16), vma)
        return carry

    lax.fori_loop(0, seq // ts, setup, 0, unroll=4)

    def stage_a(row0, nrows, j, slot, tri):
        rows = pl.ds(pl.multiple_of(row0, tk), nrows)
        keys = pl.ds(pl.multiple_of(j * tk, tk), tk)
        ka = jnp.concatenate([k_ref[keys, :], kf_ref[keys, :]], axis=1)
        s = _dot_nt(qa_ref[rows, :], ka)
        for r0 in range(0, nrows, rb):
            part = s[r0:r0 + rb]
            if tri and r0 < tk:
                part = part + tri_ref[r0:r0 + rb, :]
            s_ref[slot, r0:r0 + rb, :] = part
            t = part[:, 0:LANES]
            for c in range(1, nblk):
                t = jnp.maximum(t, part[:, c * LANES:(c + 1) * LANES])
            tmax_ref[slot, r0:r0 + rb, :] = jnp.broadcast_to(
                jnp.max(t, axis=1, keepdims=True), (rb, LANES))

    def stage_b(row0, nrows, j, slot):
        keys = pl.ds(pl.multiple_of(j * tk, tk), tk)
        row0 = pl.multiple_of(row0, tk)
        alphas, ps = [], []
        for r0 in range(0, nrows, rb):
            rows = pl.ds(row0 + r0, rb)
            m_old = m_ref[rows, :]
            m_new = jnp.maximum(m_old, tmax_ref[slot, r0:r0 + rb, :])
            m_ref[rows, :] = m_new
            alphas.append(jnp.exp2(m_old - m_new))
            ps.append(jnp.concatenate(
                [jnp.exp2(s_ref[slot, r0:r0 + rb, c * LANES:(c + 1) * LANES] - m_new).astype(BF16)
                 for c in range(nblk)], axis=1))
        pv = _dot(jnp.concatenate(ps, axis=0), va_ref[keys, :])
        for n, r0 in enumerate(range(0, nrows, rb)):
            rows = pl.ds(row0 + r0, rb)
            alpha = jnp.concatenate([alphas[n], alphas[n]], axis=1)
            acc_ref[rows, :] = alpha * acc_ref[rows, :] + pv[r0:r0 + rb]

    stage_a(0, tq, 0, 0, True)

    def pair(item_b, item_a, tri, slot):
        stage_a(*item_a, 1 - slot, tri)
        stage_b(*item_b, slot)

    def tile(i, carry):
        full = 2 * i
        top = i * tq
        low = top + tk

        def body(t, c):
            j = 2 * t
            pair((top, tq, j), (top, tq, j + 1), False, 0)
            pair((top, tq, j + 1), (top, tq, j + 2), False, 1)
            return c

        lax.fori_loop(0, jnp.maximum(i - 1, 0), body, 0)

        @pl.when(i >= 1)
        def _():
            pair((top, tq, full - 2), (top, tq, full - 1), False, 0)
            pair((top, tq, full - 1), (top, tq, full), True, 1)

        pair((top, tq, full), (low, tk, full + 1), True, 0)
        pair((low, tk, full + 1), (jnp.minimum(i + 1, nq - 1) * tq, tq, 0), False, 1)
        return carry

    lax.fori_loop(0, nq, tile, 0)

    def finalize(t, carry):
        rows = pl.ds(pl.multiple_of(t * ts, ts), ts)
        o = acc_ref[rows, :LANES] / acc_ref[rows, LANES:] * _silu(z_ref[rows, :].astype(F32))
        o_ref[rows, :] = o.astype(BF16)
        return carry

    lax.fori_loop(0, seq // ts, finalize, 0)


def _attn_call(proj, k_meta, v_meta, gate, gate_meta, eqk, batch, seq, tq=1024, tk=512, ts=512, rb=128):
    r = batch * seq
    head = lambda g: pl.BlockSpec((None, seq, HEAD_DIM), lambda b, h, g=g: (g, b, h))
    return pl.pallas_call(
        functools.partial(_attn_kernel, tq=tq, tk=tk, ts=ts, rb=rb),
        grid=(batch, HEADS),
        in_specs=[
            head(0), head(1), head(2), head(3),
            pl.BlockSpec((N_META, HEAD_DIM), lambda b, h: (0, h)),
            pl.BlockSpec((N_META, HEAD_DIM), lambda b, h: (0, h)),
            pl.BlockSpec((seq, LANES), lambda b, h: (b, 0)),
            pl.BlockSpec((N_META, LANES), lambda b, h: (0, 0)),
            pl.BlockSpec((None, LANES, 2 * LANES), lambda b, h: (h, 0, 0)),
        ],
        out_specs=pl.BlockSpec((seq, HEAD_DIM), lambda b, h: (b, h)),
        out_shape=jax.ShapeDtypeStruct((r, D_MODEL), BF16),
        scratch_shapes=[
            pltpu.VMEM((seq, 2 * LANES), BF16),
            pltpu.VMEM((seq, LANES), BF16),
            pltpu.VMEM((seq, 2 * LANES), BF16),
            pltpu.VMEM((tk, tk), F32),
            pltpu.VMEM((2, tq, tk), F32),
            pltpu.VMEM((2, tq, LANES), F32),
            pltpu.VMEM((seq, LANES), F32),
            pltpu.VMEM((seq, 2 * LANES), F32),
        ],
        compiler_params=pltpu.CompilerParams(
            dimension_semantics=("arbitrary", "arbitrary"), vmem_limit_bytes=VMEM_LIMIT),
        name="fox_attention",
    )(proj, proj, proj, proj, k_meta, v_meta, gate, gate_meta, eqk)


def _out_kernel(aa_ref, ac_ref, ma_ref, mc_ref, x_ref, w_ref, gain_ref, o_ref):
    ya = _dot(aa_ref[...], w_ref[0])
    yc = _dot(ac_ref[...], w_ref[1])
    merged = (jax.nn.sigmoid(ma_ref[...].astype(F32)) * ya
              + jax.nn.sigmoid(mc_ref[...].astype(F32)) * yc)
    y = x_ref[...] + _dot(merged.astype(BF16), w_ref[2])
    o_ref[...] = _rmsnorm(y, gain_ref[...])


def _out_call(a_att, a_conv, proj, x2d, w_o, gain, tm=256):
    r = x2d.shape[0]
    row = lambda i: (i, 0)
    return pl.pallas_call(
        _out_kernel,
        grid=(r // tm,),
        in_specs=[
            pl.BlockSpec((tm, D_MODEL), row),
            pl.BlockSpec((tm, D_MODEL), row),
            pl.BlockSpec((None, tm, D_MODEL), lambda i: (4, i, 0)),
            pl.BlockSpec((None, tm, D_MODEL), lambda i: (5, i, 0)),
            pl.BlockSpec((tm, D_MODEL), row),
            pl.BlockSpec((3, D_MODEL, D_MODEL), lambda i: (0, 0, 0),
                         pipeline_mode=pl.Buffered(1)),
            pl.BlockSpec((1, D_MODEL), lambda i: (0, 0)),
        ],
        out_specs=pl.BlockSpec((tm, D_MODEL), row),
        out_shape=jax.ShapeDtypeStruct((r, D_MODEL), F32),
        compiler_params=pltpu.CompilerParams(
            dimension_semantics=("arbitrary",), vmem_limit_bytes=VMEM_LIMIT),
        name="out_proj",
    )(a_att, a_conv, proj, proj, x2d, w_o, gain)


def kernel(x, meta_tokens, norm_gain, w_in, b_f, conv_w, w_att_o, w_conv_o, w_out, final_gain):
    batch, seq, _ = x.shape
    assert norm_gain.shape[0] == 1, "single-layer block"
    assert w_in.shape == (1, D_MODEL, N_IN) and w_in.dtype == F32
    wt = jnp.transpose(w_in[0])
    bf = jnp.pad(b_f[0].astype(F32), (0, LANES - HEADS)).reshape(1, LANES)
    w_o = jnp.stack([w_att_o[0], w_conv_o[0], w_out[0]]).astype(BF16)
    gain = norm_gain[0].reshape(1, D_MODEL).astype(F32)
    fgain = final_gain.reshape(1, D_MODEL).astype(F32)
    place, ones, eqk = _feature_constants()

    k_meta, v_meta, g_meta, cm, gate_meta = _meta_call(
        meta_tokens.astype(F32), gain, wt, bf, place, ones)
    h, gate = _norm_call(x, gain, wt, bf, cm, place, ones)
    proj = _proj_call(h, wt)
    a_conv = _conv_call(h, wt, g_meta, conv_w[0].astype(F32), seq)
    a_att = _attn_call(proj, k_meta, v_meta, gate, gate_meta, eqk, batch, seq)
    out = _out_call(a_att, a_conv, proj, x.reshape(batch * seq, D_MODEL), w_o, fgain)
    return out.reshape(batch, seq, D_MODEL)
```

```python
import functools

import numpy as np

import jax
import jax.numpy as jnp
from jax import lax
from jax.experimental import pallas as pl
from jax.experimental.pallas import tpu as pltpu

D_MODEL = 2048
N_META = 16
HEADS = 16
HEAD_DIM = 128
CONV_K = 3
EPS = 1e-6
LANES = 128
SUBLANES = 8
NEG_BIG = -1e30
LOG2E = 1.4426950408889634
GATE_LANES = LANES // HEADS
VMEM_LIMIT = 56 * 1024 * 1024

_GROUPS = ("q", "k", "v", "f", "z_att", "u", "gate_b", "gate_c", "z_conv", "m_att", "m_conv")
_WIDTHS = (D_MODEL, D_MODEL, D_MODEL, HEADS) + (D_MODEL,) * 7
COL = {name: sum(_WIDTHS[:n]) for n, name in enumerate(_GROUPS)}
N_IN = sum(_WIDTHS)

F32 = jnp.float32
BF16 = jnp.bfloat16


def _dot(a, b):
    return jnp.dot(a, b, preferred_element_type=F32)


def _dot_nt(a, b):
    return lax.dot_general(a, b, (((1,), (1,)), ((), ())), preferred_element_type=F32)


def _rmsnorm(x, gain):
    return x * lax.rsqrt(jnp.mean(x * x, axis=-1, keepdims=True) + EPS) * gain


def _log_sigmoid(x):
    return jnp.minimum(x, 0.0) - jnp.log1p(jnp.exp(-jnp.abs(x)))


def _silu(x):
    return x * jax.nn.sigmoid(x)


def _split3(x):
    hi = x.astype(BF16)
    r1 = x - hi.astype(F32)
    mid = r1.astype(BF16)
    lo = (r1 - mid.astype(F32)).astype(BF16)
    return hi, mid, lo


def _cumsum_rows(x):
    n = x.shape[0]
    tri = (lax.broadcasted_iota(jnp.int32, (n, n), 0)
           >= lax.broadcasted_iota(jnp.int32, (n, n), 1)).astype(BF16)
    hi, mid, lo = _split3(x)
    return _dot(tri, hi) + _dot(tri, mid) + _dot(tri, lo)


def _gate_features(c, place_ref, ones_ref):
    hi, mid, lo = _split3(c * LOG2E)
    cat = jnp.concatenate([hi, mid, lo], axis=1)
    return (_dot(cat, place_ref[...]) + ones_ref[...]).astype(BF16)


def _feature_constants():
    place = np.zeros((3 * LANES, LANES), np.float32)
    ones = np.zeros((1, LANES), np.float32)
    eq = np.zeros((HEADS, LANES, LANES), np.float32)
    ek = np.zeros((HEADS, LANES, LANES), np.float32)
    for h in range(HEADS):
        base = GATE_LANES * h
        ones[0, base + 3] = 1.0
        for f in range(3):
            place[f * LANES + h, base + f] = 1.0
            eq[h, base + f, f] = 1.0
            eq[h, base + 3, 3 + f] = 1.0
            ek[h, base + 3, f] = 1.0
            ek[h, base + f, 3 + f] = -1.0
    return (jnp.asarray(place, BF16), jnp.asarray(ones, F32),
            jnp.asarray(np.concatenate([eq, ek], axis=2), BF16))


def _w_tile_spec(tn, row_of):
    assert all(c % HEADS == 0 for c in COL.values()) and tn % HEADS == 0
    return pl.BlockSpec((pl.Element(tn), pl.Element(D_MODEL)),
                        lambda *ids: (pl.multiple_of(row_of(*ids), HEADS), 0))


def _forget_w_spec():
    assert COL["f"] % LANES == 0
    return pl.BlockSpec((LANES, D_MODEL), lambda *ids: (COL["f"] // LANES, 0))


def _stage_weight(src_ref, dst_ref, chunk=128):
    for r in range(0, src_ref.shape[0], chunk):
        dst_ref[r:r + chunk, :] = src_ref[r:r + chunk, :].astype(BF16)


def _meta_kernel(meta_ref, gain_ref, wk_ref, wv_ref, wu_ref, wgc_ref, wf_ref, bf_ref,
                 place_ref, ones_ref, k_ref, v_ref, g_ref, cm_ref, gm_ref):
    h = _rmsnorm(meta_ref[...], gain_ref[...]).astype(BF16)
    proj = lambda w_ref: _dot_nt(h, w_ref[...].astype(BF16))
    k_ref[...] = proj(wk_ref).astype(BF16)
    v_ref[...] = proj(wv_ref).astype(BF16)
    g_ref[...] = proj(wgc_ref) * proj(wu_ref)
    cm = _cumsum_rows(_log_sigmoid(proj(wf_ref) + bf_ref[...]))
    cm_ref[...] = cm
    gm_ref[...] = _gate_features(cm, place_ref, ones_ref)


def _meta_call(meta, gain, wt, bf, place, ones, tn=512):
    nt = D_MODEL // tn
    const = lambda shape: pl.BlockSpec(shape, lambda j: (0,) * len(shape))
    group = lambda name: _w_tile_spec(tn, lambda j: COL[name] + j * tn)
    return pl.pallas_call(
        _meta_kernel,
        grid=(nt,),
        in_specs=[
            const((N_META, D_MODEL)), const((1, D_MODEL)),
            group("k"), group("v"), group("u"), group("gate_c"), _forget_w_spec(),
            const((1, LANES)), const((3 * LANES, LANES)), const((1, LANES)),
        ],
        out_specs=[
            pl.BlockSpec((N_META, tn), lambda j: (0, j)),
            pl.BlockSpec((N_META, tn), lambda j: (0, j)),
            pl.BlockSpec((N_META, tn), lambda j: (0, j)),
            const((N_META, LANES)),
            const((N_META, LANES)),
        ],
        out_shape=[
            jax.ShapeDtypeStruct((N_META, D_MODEL), BF16),
            jax.ShapeDtypeStruct((N_META, D_MODEL), BF16),
            jax.ShapeDtypeStruct((N_META, D_MODEL), F32),
            jax.ShapeDtypeStruct((N_META, LANES), F32),
            jax.ShapeDtypeStruct((N_META, LANES), BF16),
        ],
        compiler_params=pltpu.CompilerParams(
            dimension_semantics=("arbitrary",), vmem_limit_bytes=VMEM_LIMIT),
        name="meta_proj",
    )(meta, gain, wt, wt, wt, wt, wt, bf, place, ones)


def _norm_kernel(x_ref, gain_ref, wf_ref, bf_ref, cm_ref, place_ref, ones_ref,
                 h_ref, gate_ref, carry_ref, *, chunk):
    @pl.when(pl.program_id(1) == 0)
    def _():
        carry_ref[...] = cm_ref[N_META - 1:N_META, :]

    h = _rmsnorm(x_ref[...], gain_ref[...]).astype(BF16)
    h_ref[...] = h
    logf = _log_sigmoid(_dot_nt(h, wf_ref[...].astype(BF16)) + bf_ref[...])
    tm = logf.shape[0]
    run = carry_ref[...]
    parts = []
    for j in range(tm // chunk):
        c = _cumsum_rows(logf[j * chunk:(j + 1) * chunk]) + run
        run = c[chunk - 1:chunk, :]
        parts.append(c)
    carry_ref[...] = run
    gate_ref[...] = _gate_features(jnp.concatenate(parts, axis=0), place_ref, ones_ref)


def _norm_call(x, gain, wt, bf, cm, place, ones, tm=512, chunk=256):
    b, s, _ = x.shape
    nt = s // tm
    const = lambda shape: pl.BlockSpec(shape, lambda bi, i: (0,) * len(shape))
    return pl.pallas_call(
        functools.partial(_norm_kernel, chunk=chunk),
        grid=(b, nt),
        in_specs=[
            pl.BlockSpec((None, tm, D_MODEL), lambda bi, i: (bi, i, 0)),
            const((1, D_MODEL)),
            _forget_w_spec(),
            const((1, LANES)),
            const((N_META, LANES)),
            const((3 * LANES, LANES)),
            const((1, LANES)),
        ],
        out_specs=[
            pl.BlockSpec((tm, D_MODEL), lambda bi, i: (bi * nt + i, 0)),
            pl.BlockSpec((tm, LANES), lambda bi, i: (bi * nt + i, 0)),
        ],
        out_shape=[
            jax.ShapeDtypeStruct((b * s, D_MODEL), BF16),
            jax.ShapeDtypeStruct((b * s, LANES), BF16),
        ],
        scratch_shapes=[pltpu.VMEM((1, LANES), F32)],
        compiler_params=pltpu.CompilerParams(
            dimension_semantics=("arbitrary", "arbitrary"), vmem_limit_bytes=VMEM_LIMIT),
        name="norm_forget",
    )(x, gain, wt, bf, cm, place, ones)


_PROJ_GROUPS = ("q", "k", "v", "z_att", "m_att", "m_conv")


def _proj_group_row(g):
    row = jnp.int32(COL[_PROJ_GROUPS[0]])
    for n in range(1, len(_PROJ_GROUPS)):
        row = jnp.where(g == n, COL[_PROJ_GROUPS[n]], row)
    return row


def _proj_kernel(h_ref, w_ref, o_ref, w_s, *, row_chunk):
    @pl.when(pl.program_id(2) == 0)
    def _():
        _stage_weight(w_ref, w_s)

    scale = jnp.where(pl.program_id(0) == 0, HEAD_DIM ** -0.5 * LOG2E, 1.0).astype(F32)
    for r in range(0, h_ref.shape[0], row_chunk):
        y = _dot_nt(h_ref[r:r + row_chunk, :], w_s[...])
        o_ref[r:r + row_chunk, :] = (y * scale).astype(BF16)


def _proj_call(h, wt, tm=2048, tn=1024, row_chunk=512):
    r = h.shape[0]
    ng = len(_PROJ_GROUPS)
    return pl.pallas_call(
        functools.partial(_proj_kernel, row_chunk=row_chunk),
        grid=(ng, D_MODEL // tn, r // tm),
        in_specs=[
            pl.BlockSpec((tm, D_MODEL), lambda g, j, i: (i, 0)),
            _w_tile_spec(tn, lambda g, j, i: _proj_group_row(g) + j * tn),
        ],
        out_specs=pl.BlockSpec((None, tm, tn), lambda g, j, i: (g, i, j)),
        out_shape=jax.ShapeDtypeStruct((ng, r, D_MODEL), BF16),
        scratch_shapes=[pltpu.VMEM((tn, D_MODEL), BF16)],
        compiler_params=pltpu.CompilerParams(
            dimension_semantics=("arbitrary", "arbitrary", "arbitrary"),
            vmem_limit_bytes=VMEM_LIMIT),
        name="plain_proj",
    )(h, wt)


_CONV_GROUPS = ("u", "gate_b", "gate_c", "z_conv")


def _conv_kernel(h_ref, wu_ref, wgb_ref, wgc_ref, wzc_ref, gm_ref, cw_ref, o_ref,
                 w_s, gbuf_ref, *, tiles_per_batch):
    i = pl.program_id(1)
    tm = h_ref.shape[0]

    @pl.when(i == 0)
    def _():
        for n, w_ref in enumerate((wu_ref, wgb_ref, wgc_ref, wzc_ref)):
            _stage_weight(w_ref, w_s.at[n])

    @pl.when(i % tiles_per_batch == 0)
    def _():
        gbuf_ref[0:SUBLANES, :] = gm_ref[N_META - SUBLANES:N_META, :]

    h = h_ref[...]
    u = _dot_nt(h, w_s[0])
    gc = _dot_nt(h, w_s[2])
    g = gc * u
    gbuf_ref[SUBLANES:SUBLANES + tm, :] = g
    conv = (gbuf_ref[SUBLANES - 2:SUBLANES - 2 + tm, :] * cw_ref[0:1, :]
            + gbuf_ref[SUBLANES - 1:SUBLANES - 1 + tm, :] * cw_ref[1:2, :]
            + g * cw_ref[2:3, :])
    gbuf_ref[0:SUBLANES, :] = g[tm - SUBLANES:tm, :]
    gb = _dot_nt(h, w_s[1])
    zc = _dot_nt(h, w_s[3])
    o_ref[...] = (gb * conv * _silu(zc)).astype(BF16)


def _conv_call(h, wt, g_meta, conv_w, rows_per_batch, tm=2048, tc=256):
    r = h.shape[0]
    w_specs = [_w_tile_spec(tc, lambda j, i, name=name: COL[name] + j * tc)
               for name in _CONV_GROUPS]
    return pl.pallas_call(
        functools.partial(_conv_kernel, tiles_per_batch=rows_per_batch // tm),
        grid=(D_MODEL // tc, r // tm),
        in_specs=[pl.BlockSpec((tm, D_MODEL), lambda j, i: (i, 0))] + w_specs + [
            pl.BlockSpec((N_META, tc), lambda j, i: (0, j)),
            pl.BlockSpec((CONV_K, tc), lambda j, i: (0, j)),
        ],
        out_specs=pl.BlockSpec((tm, tc), lambda j, i: (i, j)),
        out_shape=jax.ShapeDtypeStruct((r, D_MODEL), BF16),
        scratch_shapes=[pltpu.VMEM((len(_CONV_GROUPS), tc, D_MODEL), BF16),
                        pltpu.VMEM((SUBLANES + tm, tc), F32)],
        compiler_params=pltpu.CompilerParams(
            dimension_semantics=("arbitrary", "arbitrary"), vmem_limit_bytes=VMEM_LIMIT),
        name="conv_proj",
    )(h, *([wt] * len(_CONV_GROUPS)), g_meta, conv_w)


def _attn_kernel(q_ref, k_ref, v_ref, z_ref, km_ref, vm_ref, g_ref, gm_ref, eqk_ref,
                 o_ref, qa_ref, kf_ref, va_ref, tri_ref, s_ref, tmax_ref, m_ref, acc_ref,
                 *, tq, tk, ts, rb):
    seq = q_ref.shape[0]
    nq = seq // tq
    assert tq == 2 * tk
    nblk = tk // LANES

    tri_ref[...] = jnp.where(
        lax.broadcasted_iota(jnp.int32, (tk, tk), 1) <= lax.broadcasted_iota(jnp.int32, (tk, tk), 0),
        0.0, NEG_BIG)
    kma = jnp.concatenate(
        [km_ref[...], _dot(gm_ref[...], eqk_ref[:, LANES:]).astype(BF16)], axis=1)
    vma = jnp.concatenate([vm_ref[...], jnp.ones((N_META, LANES), BF16)], axis=1)

    def setup(t, carry):
        rows = pl.ds(pl.multiple_of(t * ts, ts), ts)
        feat = _dot(g_ref[rows, :], eqk_ref[...]).astype(BF16)
        kf_ref[rows, :] = feat[:, LANES:]
        qa = jnp.concatenate([q_ref[rows, :], feat[:, :LANES]], axis=1)
        qa_ref[rows, :] = qa
        va_ref[rows, :] = jnp.concatenate([v_ref[rows, :], jnp.ones((ts, LANES), BF16)], axis=1)
        sm = _dot_nt(qa, kma)
        m0 = jnp.max(sm, axis=1, keepdims=True)
        m_ref[rows, :] = jnp.broadcast_to(m0, (ts, LANES))
        acc_ref[rows, :] = _dot(jnp.exp2(sm - m0).astype(BF16), vma)
        return carry

    lax.fori_loop(0, seq // ts, setup, 0, unroll=8)

    def stage_a(row0, nrows, j, slot, tri):
        rows = pl.ds(pl.multiple_of(row0, tk), nrows)
        keys = pl.ds(pl.multiple_of(j * tk, tk), tk)
        ka = jnp.concatenate([k_ref[keys, :], kf_ref[keys, :]], axis=1)
        s = _dot_nt(qa_ref[rows, :], ka)
        for r0 in range(0, nrows, rb):
            part = s[r0:r0 + rb]
            if tri and r0 < tk:
                part = part + tri_ref[r0:r0 + rb, :]
            s_ref[slot, r0:r0 + rb, :] = part
            t = part[:, 0:LANES]
            for c in range(1, nblk):
                t = jnp.maximum(t, part[:, c * LANES:(c + 1) * LANES])
            tmax_ref[slot, r0:r0 + rb, :] = jnp.broadcast_to(
                jnp.max(t, axis=1, keepdims=True), (rb, LANES))

    def stage_b(row0, nrows, j, slot):
        keys = pl.ds(pl.multiple_of(j * tk, tk), tk)
        row0 = pl.multiple_of(row0, tk)
        alphas, ps = [], []
        for r0 in range(0, nrows, rb):
            rows = pl.ds(row0 + r0, rb)
            m_old = m_ref[rows, :]
            m_new = jnp.maximum(m_old, tmax_ref[slot, r0:r0 + rb, :])
            m_ref[rows, :] = m_new
            alphas.append(jnp.exp2(m_old - m_new))
            ps.append(jnp.concatenate(
                [jnp.exp2(s_ref[slot, r0:r0 + rb, c * LANES:(c + 1) * LANES] - m_new).astype(BF16)
                 for c in range(nblk)], axis=1))
        pv = _dot(jnp.concatenate(ps, axis=0), va_ref[keys, :])
        for n, r0 in enumerate(range(0, nrows, rb)):
            rows = pl.ds(row0 + r0, rb)
            alpha = jnp.concatenate([alphas[n], alphas[n]], axis=1)
            acc_ref[rows, :] = alpha * acc_ref[rows, :] + pv[r0:r0 + rb]

    items = []
    for i in range(nq):
        items += [(i * tq, tq, j, False) for j in range(2 * i)]
        items += [(i * tq, tq, 2 * i, True), (i * tq + tk, tk, 2 * i + 1, True)]

    stage_a(*items[0][:3], 0, items[0][3])
    for n, item in enumerate(items):
        if n + 1 < len(items):
            nxt = items[n + 1]
            stage_a(*nxt[:3], 1 - n % 2, nxt[3])
        stage_b(*item[:3], n % 2)

    def finalize(t, carry):
        rows = pl.ds(pl.multiple_of(t * ts, ts), ts)
        o = acc_ref[rows, :LANES] / acc_ref[rows, LANES:] * _silu(z_ref[rows, :].astype(F32))
        o_ref[rows, :] = o.astype(BF16)
        return carry

    lax.fori_loop(0, seq // ts, finalize, 0)


def _attn_call(proj, k_meta, v_meta, gate, gate_meta, eqk, batch, seq, tq=1024, tk=512, ts=512, rb=128):
    r = batch * seq
    head = lambda g: pl.BlockSpec((None, seq, HEAD_DIM), lambda b, h, g=g: (g, b, h))
    return pl.pallas_call(
        functools.partial(_attn_kernel, tq=tq, tk=tk, ts=ts, rb=rb),
        grid=(batch, HEADS),
        in_specs=[
            head(0), head(1), head(2), head(3),
            pl.BlockSpec((N_META, HEAD_DIM), lambda b, h: (0, h)),
            pl.BlockSpec((N_META, HEAD_DIM), lambda b, h: (0, h)),
            pl.BlockSpec((seq, LANES), lambda b, h: (b, 0)),
            pl.BlockSpec((N_META, LANES), lambda b, h: (0, 0)),
            pl.BlockSpec((None, LANES, 2 * LANES), lambda b, h: (h, 0, 0)),
        ],
        out_specs=pl.BlockSpec((seq, HEAD_DIM), lambda b, h: (b, h)),
        out_shape=jax.ShapeDtypeStruct((r, D_MODEL), BF16),
        scratch_shapes=[
            pltpu.VMEM((seq, 2 * LANES), BF16),
            pltpu.VMEM((seq, LANES), BF16),
            pltpu.VMEM((seq, 2 * LANES), BF16),
            pltpu.VMEM((tk, tk), F32),
            pltpu.VMEM((2, tq, tk), F32),
            pltpu.VMEM((2, tq, LANES), F32),
            pltpu.VMEM((seq, LANES), F32),
            pltpu.VMEM((seq, 2 * LANES), F32),
        ],
        compiler_params=pltpu.CompilerParams(
            dimension_semantics=("arbitrary", "arbitrary"), vmem_limit_bytes=VMEM_LIMIT),
        name="fox_attention",
    )(proj, proj, proj, proj, k_meta, v_meta, gate, gate_meta, eqk)


def _out_kernel(aa_ref, ac_ref, ma_ref, mc_ref, x_ref, w_ref, gain_ref, o_ref):
    ya = _dot(aa_ref[...], w_ref[0])
    yc = _dot(ac_ref[...], w_ref[1])
    merged = (jax.nn.sigmoid(ma_ref[...].astype(F32)) * ya
              + jax.nn.sigmoid(mc_ref[...].astype(F32)) * yc)
    y = x_ref[...] + _dot(merged.astype(BF16), w_ref[2])
    o_ref[...] = _rmsnorm(y, gain_ref[...])


def _out_call(a_att, a_conv, proj, x2d, w_o, gain, tm=256):
    r = x2d.shape[0]
    row = lambda i: (i, 0)
    return pl.pallas_call(
        _out_kernel,
        grid=(r // tm,),
        in_specs=[
            pl.BlockSpec((tm, D_MODEL), row),
            pl.BlockSpec((tm, D_MODEL), row),
            pl.BlockSpec((None, tm, D_MODEL), lambda i: (4, i, 0)),
            pl.BlockSpec((None, tm, D_MODEL), lambda i: (5, i, 0)),
            pl.BlockSpec((tm, D_MODEL), row),
            pl.BlockSpec((3, D_MODEL, D_MODEL), lambda i: (0, 0, 0),
                         pipeline_mode=pl.Buffered(1)),
            pl.BlockSpec((1, D_MODEL), lambda i: (0, 0)),
        ],
        out_specs=pl.BlockSpec((tm, D_MODEL), row),
        out_shape=jax.ShapeDtypeStruct((r, D_MODEL), F32),
        compiler_params=pltpu.CompilerParams(
            dimension_semantics=("arbitrary",), vmem_limit_bytes=VMEM_LIMIT),
        name="out_proj",
    )(a_att, a_conv, proj, proj, x2d, w_o, gain)


def kernel(x, meta_tokens, norm_gain, w_in, b_f, conv_w, w_att_o, w_conv_o, w_out, final_gain):
    batch, seq, _ = x.shape
    assert norm_gain.shape[0] == 1, "single-layer block"
    assert w_in.shape == (1, D_MODEL, N_IN) and w_in.dtype == F32
    wt = jnp.transpose(w_in[0])
    bf = jnp.pad(b_f[0].astype(F32), (0, LANES - HEADS)).reshape(1, LANES)
    w_o = jnp.stack([w_att_o[0], w_conv_o[0], w_out[0]]).astype(BF16)
    gain = norm_gain[0].reshape(1, D_MODEL).astype(F32)
    fgain = final_gain.reshape(1, D_MODEL).astype(F32)
    place, ones, eqk = _feature_constants()

    k_meta, v_meta, g_meta, cm, gate_meta = _meta_call(
        meta_tokens.astype(F32), gain, wt, bf, place, ones)
    h, gate = _norm_call(x, gain, wt, bf, cm, place, ones)
    proj = _proj_call(h, wt)
    a_conv = _conv_call(h, wt, g_meta, conv_w[0].astype(F32), seq)
    a_att = _attn_call(proj, k_meta, v_meta, gate, gate_meta, eqk, batch, seq)
    out = _out_call(a_att, a_conv, proj, x.reshape(batch * seq, D_MODEL), w_o, fgain)
    return out.reshape(batch, seq, D_MODEL)
```

```python
import functools

import numpy as np

import jax
import jax.numpy as jnp
from jax import lax
from jax.experimental import pallas as pl
from jax.experimental.pallas import tpu as pltpu

D_MODEL = 2048
N_META = 16
HEADS = 16
HEAD_DIM = 128
CONV_K = 3
EPS = 1e-6
LANES = 128
SUBLANES = 8
NEG_BIG = -1e30
LOG2E = 1.4426950408889634
GATE_LANES = LANES // HEADS
VMEM_LIMIT = 56 * 1024 * 1024

_GROUPS = ("q", "k", "v", "f", "z_att", "u", "gate_b", "gate_c", "z_conv", "m_att", "m_conv")
_WIDTHS = (D_MODEL, D_MODEL, D_MODEL, HEADS) + (D_MODEL,) * 7
COL = {name: sum(_WIDTHS[:n]) for n, name in enumerate(_GROUPS)}
N_IN = sum(_WIDTHS)

F32 = jnp.float32
BF16 = jnp.bfloat16


def _dot(a, b):
    return jnp.dot(a, b, preferred_element_type=F32)


def _dot_nt(a, b):
    return lax.dot_general(a, b, (((1,), (1,)), ((), ())), preferred_element_type=F32)


def _rmsnorm(x, gain):
    return x * lax.rsqrt(jnp.mean(x * x, axis=-1, keepdims=True) + EPS) * gain


def _log_sigmoid(x):
    return jnp.minimum(x, 0.0) - jnp.log1p(jnp.exp(-jnp.abs(x)))


def _silu(x):
    return x * jax.nn.sigmoid(x)


def _split3(x):
    hi = x.astype(BF16)
    r1 = x - hi.astype(F32)
    mid = r1.astype(BF16)
    lo = (r1 - mid.astype(F32)).astype(BF16)
    return hi, mid, lo


def _cumsum_rows(x):
    n = x.shape[0]
    tri = (lax.broadcasted_iota(jnp.int32, (n, n), 0)
           >= lax.broadcasted_iota(jnp.int32, (n, n), 1)).astype(BF16)
    hi, mid, lo = _split3(x)
    return _dot(tri, hi) + _dot(tri, mid) + _dot(tri, lo)


def _gate_features(c, place_ref, ones_ref):
    hi, mid, lo = _split3(c * LOG2E)
    cat = jnp.concatenate([hi, mid, lo], axis=1)
    return (_dot(cat, place_ref[...]) + ones_ref[...]).astype(BF16)


def _feature_constants():
    place = np.zeros((3 * LANES, LANES), np.float32)
    ones = np.zeros((1, LANES), np.float32)
    eq = np.zeros((HEADS, LANES, LANES), np.float32)
    ek = np.zeros((HEADS, LANES, LANES), np.float32)
    for h in range(HEADS):
        base = GATE_LANES * h
        ones[0, base + 3] = 1.0
        for f in range(3):
            place[f * LANES + h, base + f] = 1.0
            eq[h, base + f, f] = 1.0
            eq[h, base + 3, 3 + f] = 1.0
            ek[h, base + 3, f] = 1.0
            ek[h, base + f, 3 + f] = -1.0
    return (jnp.asarray(place, BF16), jnp.asarray(ones, F32),
            jnp.asarray(np.concatenate([eq, ek], axis=2), BF16))


def _w_tile_spec(tn, row_of):
    assert all(c % HEADS == 0 for c in COL.values()) and tn % HEADS == 0
    return pl.BlockSpec((pl.Element(tn), pl.Element(D_MODEL)),
                        lambda *ids: (pl.multiple_of(row_of(*ids), HEADS), 0))


def _forget_w_spec():
    assert COL["f"] % LANES == 0
    return pl.BlockSpec((LANES, D_MODEL), lambda *ids: (COL["f"] // LANES, 0))


def _stage_weight(src_ref, dst_ref, chunk=128):
    for r in range(0, src_ref.shape[0], chunk):
        dst_ref[r:r + chunk, :] = src_ref[r:r + chunk, :].astype(BF16)


def _meta_kernel(meta_ref, gain_ref, wk_ref, wv_ref, wu_ref, wgc_ref, wf_ref, bf_ref,
                 place_ref, ones_ref, k_ref, v_ref, g_ref, cm_ref, gm_ref):
    h = _rmsnorm(meta_ref[...], gain_ref[...]).astype(BF16)
    proj = lambda w_ref: _dot_nt(h, w_ref[...].astype(BF16))
    k_ref[...] = proj(wk_ref).astype(BF16)
    v_ref[...] = proj(wv_ref).astype(BF16)
    g_ref[...] = proj(wgc_ref) * proj(wu_ref)
    cm = _cumsum_rows(_log_sigmoid(proj(wf_ref) + bf_ref[...]))
    cm_ref[...] = cm
    gm_ref[...] = _gate_features(cm, place_ref, ones_ref)


def _meta_call(meta, gain, wt, bf, place, ones, tn=512):
    nt = D_MODEL // tn
    const = lambda shape: pl.BlockSpec(shape, lambda j: (0,) * len(shape))
    group = lambda name: _w_tile_spec(tn, lambda j: COL[name] + j * tn)
    return pl.pallas_call(
        _meta_kernel,
        grid=(nt,),
        in_specs=[
            const((N_META, D_MODEL)), const((1, D_MODEL)),
            group("k"), group("v"), group("u"), group("gate_c"), _forget_w_spec(),
            const((1, LANES)), const((3 * LANES, LANES)), const((1, LANES)),
        ],
        out_specs=[
            pl.BlockSpec((N_META, tn), lambda j: (0, j)),
            pl.BlockSpec((N_META, tn), lambda j: (0, j)),
            pl.BlockSpec((N_META, tn), lambda j: (0, j)),
            const((N_META, LANES)),
            const((N_META, LANES)),
        ],
        out_shape=[
            jax.ShapeDtypeStruct((N_META, D_MODEL), BF16),
            jax.ShapeDtypeStruct((N_META, D_MODEL), BF16),
            jax.ShapeDtypeStruct((N_META, D_MODEL), F32),
            jax.ShapeDtypeStruct((N_META, LANES), F32),
            jax.ShapeDtypeStruct((N_META, LANES), BF16),
        ],
        compiler_params=pltpu.CompilerParams(
            dimension_semantics=("arbitrary",), vmem_limit_bytes=VMEM_LIMIT),
        name="meta_proj",
    )(meta, gain, wt, wt, wt, wt, wt, bf, place, ones)


def _norm_kernel(x_ref, gain_ref, wf_ref, bf_ref, cm_ref, place_ref, ones_ref,
                 h_ref, gate_ref, carry_ref, *, chunk):
    @pl.when(pl.program_id(1) == 0)
    def _():
        carry_ref[...] = cm_ref[N_META - 1:N_META, :]

    h = _rmsnorm(x_ref[...], gain_ref[...]).astype(BF16)
    h_ref[...] = h
    logf = _log_sigmoid(_dot_nt(h, wf_ref[...].astype(BF16)) + bf_ref[...])
    tm = logf.shape[0]
    run = carry_ref[...]
    parts = []
    for j in range(tm // chunk):
        c = _cumsum_rows(logf[j * chunk:(j + 1) * chunk]) + run
        run = c[chunk - 1:chunk, :]
        parts.append(c)
    carry_ref[...] = run
    gate_ref[...] = _gate_features(jnp.concatenate(parts, axis=0), place_ref, ones_ref)


def _norm_call(x, gain, wt, bf, cm, place, ones, tm=512, chunk=256):
    b, s, _ = x.shape
    nt = s // tm
    const = lambda shape: pl.BlockSpec(shape, lambda bi, i: (0,) * len(shape))
    return pl.pallas_call(
        functools.partial(_norm_kernel, chunk=chunk),
        grid=(b, nt),
        in_specs=[
            pl.BlockSpec((None, tm, D_MODEL), lambda bi, i: (bi, i, 0)),
            const((1, D_MODEL)),
            _forget_w_spec(),
            const((1, LANES)),
            const((N_META, LANES)),
            const((3 * LANES, LANES)),
            const((1, LANES)),
        ],
        out_specs=[
            pl.BlockSpec((tm, D_MODEL), lambda bi, i: (bi * nt + i, 0)),
            pl.BlockSpec((tm, LANES), lambda bi, i: (bi * nt + i, 0)),
        ],
        out_shape=[
            jax.ShapeDtypeStruct((b * s, D_MODEL), BF16),
            jax.ShapeDtypeStruct((b * s, LANES), BF16),
        ],
        scratch_shapes=[pltpu.VMEM((1, LANES), F32)],
        compiler_params=pltpu.CompilerParams(
            dimension_semantics=("arbitrary", "arbitrary"), vmem_limit_bytes=VMEM_LIMIT),
        name="norm_forget",
    )(x, gain, wt, bf, cm, place, ones)


_PROJ_GROUPS = ("q", "k", "v", "z_att", "m_att", "m_conv")


def _proj_group_row(g):
    row = jnp.int32(COL[_PROJ_GROUPS[0]])
    for n in range(1, len(_PROJ_GROUPS)):
        row = jnp.where(g == n, COL[_PROJ_GROUPS[n]], row)
    return row


def _proj_kernel(h_ref, w_ref, o_ref, w_s, *, row_chunk):
    @pl.when(pl.program_id(2) == 0)
    def _():
        _stage_weight(w_ref, w_s)

    scale = jnp.where(pl.program_id(0) == 0, HEAD_DIM ** -0.5 * LOG2E, 1.0).astype(F32)
    for r in range(0, h_ref.shape[0], row_chunk):
        y = _dot_nt(h_ref[r:r + row_chunk, :], w_s[...])
        o_ref[r:r + row_chunk, :] = (y * scale).astype(BF16)


def _proj_call(h, wt, tm=2048, tn=1024, row_chunk=512):
    r = h.shape[0]
    ng = len(_PROJ_GROUPS)
    return pl.pallas_call(
        functools.partial(_proj_kernel, row_chunk=row_chunk),
        grid=(ng, D_MODEL // tn, r // tm),
        in_specs=[
            pl.BlockSpec((tm, D_MODEL), lambda g, j, i: (i, 0)),
            _w_tile_spec(tn, lambda g, j, i: _proj_group_row(g) + j * tn),
        ],
        out_specs=pl.BlockSpec((None, tm, tn), lambda g, j, i: (g, i, j)),
        out_shape=jax.ShapeDtypeStruct((ng, r, D_MODEL), BF16),
        scratch_shapes=[pltpu.VMEM((tn, D_MODEL), BF16)],
        compiler_params=pltpu.CompilerParams(
            dimension_semantics=("arbitrary", "arbitrary", "arbitrary"),
            vmem_limit_bytes=VMEM_LIMIT),
        name="plain_proj",
    )(h, wt)


_CONV_GROUPS = ("u", "gate_b", "gate_c", "z_conv")


def _conv_kernel(h_ref, wu_ref, wgb_ref, wgc_ref, wzc_ref, gm_ref, cw_ref, o_ref,
                 w_s, gbuf_ref, *, tiles_per_batch):
    i = pl.program_id(1)
    tm = h_ref.shape[0]

    @pl.when(i == 0)
    def _():
        for n, w_ref in enumerate((wu_ref, wgb_ref, wgc_ref, wzc_ref)):
            _stage_weight(w_ref, w_s.at[n])

    @pl.when(i % tiles_per_batch == 0)
    def _():
        gbuf_ref[0:SUBLANES, :] = gm_ref[N_META - SUBLANES:N_META, :]

    h = h_ref[...]
    u = _dot_nt(h, w_s[0])
    gc = _dot_nt(h, w_s[2])
    g = gc * u
    gbuf_ref[SUBLANES:SUBLANES + tm, :] = g
    conv = (gbuf_ref[SUBLANES - 2:SUBLANES - 2 + tm, :] * cw_ref[0:1, :]
            + gbuf_ref[SUBLANES - 1:SUBLANES - 1 + tm, :] * cw_ref[1:2, :]
            + g * cw_ref[2:3, :])
    gbuf_ref[0:SUBLANES, :] = g[tm - SUBLANES:tm, :]
    gb = _dot_nt(h, w_s[1])
    zc = _dot_nt(h, w_s[3])
    o_ref[...] = (gb * conv * _silu(zc)).astype(BF16)


def _conv_call(h, wt, g_meta, conv_w, rows_per_batch, tm=2048, tc=256):
    r = h.shape[0]
    w_specs = [_w_tile_spec(tc, lambda j, i, name=name: COL[name] + j * tc)
               for name in _CONV_GROUPS]
    return pl.pallas_call(
        functools.partial(_conv_kernel, tiles_per_batch=rows_per_batch // tm),
        grid=(D_MODEL // tc, r // tm),
        in_specs=[pl.BlockSpec((tm, D_MODEL), lambda j, i: (i, 0))] + w_specs + [
            pl.BlockSpec((N_META, tc), lambda j, i: (0, j)),
            pl.BlockSpec((CONV_K, tc), lambda j, i: (0, j)),
        ],
        out_specs=pl.BlockSpec((tm, tc), lambda j, i: (i, j)),
        out_shape=jax.ShapeDtypeStruct((r, D_MODEL), BF16),
        scratch_shapes=[pltpu.VMEM((len(_CONV_GROUPS), tc, D_MODEL), BF16),
                        pltpu.VMEM((SUBLANES + tm, tc), F32)],
        compiler_params=pltpu.CompilerParams(
            dimension_semantics=("arbitrary", "arbitrary"), vmem_limit_bytes=VMEM_LIMIT),
        name="conv_proj",
    )(h, *([wt] * len(_CONV_GROUPS)), g_meta, conv_w)


def _attn_kernel(q_ref, k_ref, v_ref, z_ref, km_ref, vm_ref, g_ref, gm_ref, eqk_ref,
                 o_ref, qa_ref, kf_ref, va_ref, tri_ref, s_ref, tmax_ref, m_ref, acc_ref,
                 *, tq, tk, ts, rb):
    seq = q_ref.shape[0]
    nq = seq // tq
    assert tq == 2 * tk
    nblk = tk // LANES

    tri_ref[...] = jnp.where(
        lax.broadcasted_iota(jnp.int32, (tk, tk), 1) <= lax.broadcasted_iota(jnp.int32, (tk, tk), 0),
        0.0, NEG_BIG)
    kma = jnp.concatenate(
        [km_ref[...], _dot(gm_ref[...], eqk_ref[:, LANES:]).astype(BF16)], axis=1)
    vma = jnp.concatenate([vm_ref[...], jnp.ones((N_META, LANES), BF16)], axis=1)

    def setup(t, carry):
        rows = pl.ds(pl.multiple_of(t * ts, ts), ts)
        feat = _dot(g_ref[rows, :], eqk_ref[...]).astype(BF16)
        kf_ref[rows, :] = feat[:, LANES:]
        qa = jnp.concatenate([q_ref[rows, :], feat[:, :LANES]], axis=1)
        qa_ref[rows, :] = qa
        va_ref[rows, :] = jnp.concatenate([v_ref[rows, :], jnp.ones((ts, LANES), BF16)], axis=1)
        sm = _dot_nt(qa, kma)
        m0 = jnp.max(sm, axis=1, keepdims=True)
        m_ref[rows, :] = jnp.broadcast_to(m0, (ts, LANES))
        acc_ref[rows, :] = _dot(jnp.exp2(sm - m0).astype(BF16), vma)
        return carry

    lax.fori_loop(0, seq // ts, setup, 0, unroll=8)

    def stage_a(row0, nrows, j, slot, tri):
        rows = pl.ds(pl.multiple_of(row0, tk), nrows)
        keys = pl.ds(pl.multiple_of(j * tk, tk), tk)
        ka = jnp.concatenate([k_ref[keys, :], kf_ref[keys, :]], axis=1)
        s = _dot_nt(qa_ref[rows, :], ka)
        for r0 in range(0, nrows, rb):
            part = s[r0:r0 + rb]
            if tri and r0 < tk:
                part = part + tri_ref[r0:r0 + rb, :]
            s_ref[slot, r0:r0 + rb, :] = part
            t = part[:, 0:LANES]
            for c in range(1, nblk):
                t = jnp.maximum(t, part[:, c * LANES:(c + 1) * LANES])
            tmax_ref[slot, r0:r0 + rb, :] = jnp.broadcast_to(
                jnp.max(t, axis=1, keepdims=True), (rb, LANES))

    def stage_b(row0, nrows, j, slot):
        keys = pl.ds(pl.multiple_of(j * tk, tk), tk)
        row0 = pl.multiple_of(row0, tk)
        alphas, ps = [], []
        for r0 in range(0, nrows, rb):
            rows = pl.ds(row0 + r0, rb)
            m_old = m_ref[rows, :]
            m_new = jnp.maximum(m_old, tmax_ref[slot, r0:r0 + rb, :])
            m_ref[rows, :] = m_new
            alphas.append(jnp.exp2(m_old - m_new))
            ps.append(jnp.concatenate(
                [jnp.exp2(s_ref[slot, r0:r0 + rb, c * LANES:(c + 1) * LANES] - m_new).astype(BF16)
                 for c in range(nblk)], axis=1))
        pv = _dot(jnp.concatenate(ps, axis=0), va_ref[keys, :])
        for n, r0 in enumerate(range(0, nrows, rb)):
            rows = pl.ds(row0 + r0, rb)
            alpha = jnp.concatenate([alphas[n], alphas[n]], axis=1)
            acc_ref[rows, :] = alpha * acc_ref[rows, :] + pv[r0:r0 + rb]

    items = []
    for i in range(nq):
        items += [(i * tq, tq, j, False) for j in range(2 * i)]
        items += [(i * tq, tq, 2 * i, True), (i * tq + tk, tk, 2 * i + 1, True)]

    def finalize(row0, nrows):
        for r0 in range(row0, row0 + nrows, rb):
            o = (acc_ref[r0:r0 + rb, :LANES] / acc_ref[r0:r0 + rb, LANES:]
                 * _silu(z_ref[r0:r0 + rb, :].astype(F32)))
            o_ref[r0:r0 + rb, :] = o.astype(BF16)

    stage_a(*items[0][:3], 0, items[0][3])
    for n, item in enumerate(items):
        if n + 1 < len(items):
            nxt = items[n + 1]
            stage_a(*nxt[:3], 1 - n % 2, nxt[3])
        stage_b(*item[:3], n % 2)
        if item[3]:
            finalize(item[0], tk)


def _attn_call(proj, k_meta, v_meta, gate, gate_meta, eqk, batch, seq, tq=1024, tk=512, ts=512, rb=128):
    r = batch * seq
    head = lambda g: pl.BlockSpec((None, seq, HEAD_DIM), lambda b, h, g=g: (g, b, h))
    return pl.pallas_call(
        functools.partial(_attn_kernel, tq=tq, tk=tk, ts=ts, rb=rb),
        grid=(batch, HEADS),
        in_specs=[
            head(0), head(1), head(2), head(3),
            pl.BlockSpec((N_META, HEAD_DIM), lambda b, h: (0, h)),
            pl.BlockSpec((N_META, HEAD_DIM), lambda b, h: (0, h)),
            pl.BlockSpec((seq, LANES), lambda b, h: (b, 0)),
            pl.BlockSpec((N_META, LANES), lambda b, h: (0, 0)),
            pl.BlockSpec((None, LANES, 2 * LANES), lambda b, h: (h, 0, 0)),
        ],
        out_specs=pl.BlockSpec((seq, HEAD_DIM), lambda b, h: (b, h)),
        out_shape=jax.ShapeDtypeStruct((r, D_MODEL), BF16),
        scratch_shapes=[
            pltpu.VMEM((seq, 2 * LANES), BF16),
            pltpu.VMEM((seq, LANES), BF16),
            pltpu.VMEM((seq, 2 * LANES), BF16),
            pltpu.VMEM((tk, tk), F32),
            pltpu.VMEM((2, tq, tk), F32),
            pltpu.VMEM((2, tq, LANES), F32),
            pltpu.VMEM((seq, LANES), F32),
            pltpu.VMEM((seq, 2 * LANES), F32),
        ],
        compiler_params=pltpu.CompilerParams(
            dimension_semantics=("arbitrary", "arbitrary"), vmem_limit_bytes=VMEM_LIMIT),
        name="fox_attention",
    )(proj, proj, proj, proj, k_meta, v_meta, gate, gate_meta, eqk)


def _out_kernel(aa_ref, ac_ref, ma_ref, mc_ref, x_ref, w_ref, gain_ref, o_ref):
    ya = _dot(aa_ref[...], w_ref[0])
    yc = _dot(ac_ref[...], w_ref[1])
    merged = (jax.nn.sigmoid(ma_ref[...].astype(F32)) * ya
              + jax.nn.sigmoid(mc_ref[...].astype(F32)) * yc)
    y = x_ref[...] + _dot(merged.astype(BF16), w_ref[2])
    o_ref[...] = _rmsnorm(y, gain_ref[...])


def _out_call(a_att, a_conv, proj, x2d, w_o, gain, tm=256):
    r = x2d.shape[0]
    row = lambda i: (i, 0)
    return pl.pallas_call(
        _out_kernel,
        grid=(r // tm,),
        in_specs=[
            pl.BlockSpec((tm, D_MODEL), row),
            pl.BlockSpec((tm, D_MODEL), row),
            pl.BlockSpec((None, tm, D_MODEL), lambda i: (4, i, 0)),
            pl.BlockSpec((None, tm, D_MODEL), lambda i: (5, i, 0)),
            pl.BlockSpec((tm, D_MODEL), row),
            pl.BlockSpec((3, D_MODEL, D_MODEL), lambda i: (0, 0, 0),
                         pipeline_mode=pl.Buffered(1)),
            pl.BlockSpec((1, D_MODEL), lambda i: (0, 0)),
        ],
        out_specs=pl.BlockSpec((tm, D_MODEL), row),
        out_shape=jax.ShapeDtypeStruct((r, D_MODEL), F32),
        compiler_params=pltpu.CompilerParams(
            dimension_semantics=("arbitrary",), vmem_limit_bytes=VMEM_LIMIT),
        name="out_proj",
    )(a_att, a_conv, proj, proj, x2d, w_o, gain)


def kernel(x, meta_tokens, norm_gain, w_in, b_f, conv_w, w_att_o, w_conv_o, w_out, final_gain):
    batch, seq, _ = x.shape
    assert norm_gain.shape[0] == 1, "single-layer block"
    assert w_in.shape == (1, D_MODEL, N_IN) and w_in.dtype == F32
    wt = jnp.transpose(w_in[0])
    bf = jnp.pad(b_f[0].astype(F32), (0, LANES - HEADS)).reshape(1, LANES)
    w_o = jnp.stack([w_att_o[0], w_conv_o[0], w_out[0]]).astype(BF16)
    gain = norm_gain[0].reshape(1, D_MODEL).astype(F32)
    fgain = final_gain.reshape(1, D_MODEL).astype(F32)
    place, ones, eqk = _feature_constants()

    k_meta, v_meta, g_meta, cm, gate_meta = _meta_call(
        meta_tokens.astype(F32), gain, wt, bf, place, ones)
    h, gate = _norm_call(x, gain, wt, bf, cm, place, ones)
    proj = _proj_call(h, wt)
    a_conv = _conv_call(h, wt, g_meta, conv_w[0].astype(F32), seq)
    a_att = _attn_call(proj, k_meta, v_meta, gate, gate_meta, eqk, batch, seq)
    out = _out_call(a_att, a_conv, proj, x.reshape(batch * seq, D_MODEL), w_o, fgain)
    return out.reshape(batch, seq, D_MODEL)
```

```python
import functools

import numpy as np

import jax
import jax.numpy as jnp
from jax import lax
from jax.experimental import pallas as pl
from jax.experimental.pallas import tpu as pltpu

D_MODEL = 2048
N_META = 16
HEADS = 16
HEAD_DIM = 128
CONV_K = 3
EPS = 1e-6
LANES = 128
SUBLANES = 8
NEG_BIG = -1e30
LOG2E = 1.4426950408889634
GATE_LANES = LANES // HEADS
VMEM_LIMIT = 56 * 1024 * 1024

_GROUPS = ("q", "k", "v", "f", "z_att", "u", "gate_b", "gate_c", "z_conv", "m_att", "m_conv")
_WIDTHS = (D_MODEL, D_MODEL, D_MODEL, HEADS) + (D_MODEL,) * 7
COL = {name: sum(_WIDTHS[:n]) for n, name in enumerate(_GROUPS)}
N_IN = sum(_WIDTHS)

F32 = jnp.float32
BF16 = jnp.bfloat16


def _dot(a, b):
    return jnp.dot(a, b, preferred_element_type=F32)


def _dot_nt(a, b):
    return lax.dot_general(a, b, (((1,), (1,)), ((), ())), preferred_element_type=F32)


def _rmsnorm(x, gain):
    return x * lax.rsqrt(jnp.mean(x * x, axis=-1, keepdims=True) + EPS) * gain


def _log_sigmoid(x):
    return jnp.minimum(x, 0.0) - jnp.log1p(jnp.exp(-jnp.abs(x)))


def _silu(x):
    return x * jax.nn.sigmoid(x)


def _split3(x):
    hi = x.astype(BF16)
    r1 = x - hi.astype(F32)
    mid = r1.astype(BF16)
    lo = (r1 - mid.astype(F32)).astype(BF16)
    return hi, mid, lo


def _cumsum_rows(x):
    n = x.shape[0]
    tri = (lax.broadcasted_iota(jnp.int32, (n, n), 0)
           >= lax.broadcasted_iota(jnp.int32, (n, n), 1)).astype(BF16)
    hi, mid, lo = _split3(x)
    return _dot(tri, hi) + _dot(tri, mid) + _dot(tri, lo)


def _gate_features(c, place_ref, ones_ref):
    hi, mid, lo = _split3(c * LOG2E)
    cat = jnp.concatenate([hi, mid, lo], axis=1)
    return (_dot(cat, place_ref[...]) + ones_ref[...]).astype(BF16)


def _feature_constants():
    place = np.zeros((3 * LANES, LANES), np.float32)
    ones = np.zeros((1, LANES), np.float32)
    eq = np.zeros((HEADS, LANES, LANES), np.float32)
    ek = np.zeros((HEADS, LANES, LANES), np.float32)
    for h in range(HEADS):
        base = GATE_LANES * h
        ones[0, base + 3] = 1.0
        for f in range(3):
            place[f * LANES + h, base + f] = 1.0
            eq[h, base + f, f] = 1.0
            eq[h, base + 3, 3 + f] = 1.0
            ek[h, base + 3, f] = 1.0
            ek[h, base + f, 3 + f] = -1.0
    return (jnp.asarray(place, BF16), jnp.asarray(ones, F32),
            jnp.asarray(np.concatenate([eq, ek], axis=2), BF16))


def _w_tile_spec(tn, row_of):
    assert all(c % HEADS == 0 for c in COL.values()) and tn % HEADS == 0
    return pl.BlockSpec((pl.Element(tn), pl.Element(D_MODEL)),
                        lambda *ids: (pl.multiple_of(row_of(*ids), HEADS), 0))


def _forget_w_spec():
    assert COL["f"] % LANES == 0
    return pl.BlockSpec((LANES, D_MODEL), lambda *ids: (COL["f"] // LANES, 0))


def _stage_weight(src_ref, dst_ref, chunk=128):
    for r in range(0, src_ref.shape[0], chunk):
        dst_ref[r:r + chunk, :] = src_ref[r:r + chunk, :].astype(BF16)


def _meta_kernel(meta_ref, gain_ref, wk_ref, wv_ref, wu_ref, wgc_ref, wf_ref, bf_ref,
                 place_ref, ones_ref, k_ref, v_ref, g_ref, cm_ref, gm_ref):
    h = _rmsnorm(meta_ref[...], gain_ref[...]).astype(BF16)
    proj = lambda w_ref: _dot_nt(h, w_ref[...].astype(BF16))
    k_ref[...] = proj(wk_ref).astype(BF16)
    v_ref[...] = proj(wv_ref).astype(BF16)
    g_ref[...] = proj(wgc_ref) * proj(wu_ref)
    cm = _cumsum_rows(_log_sigmoid(proj(wf_ref) + bf_ref[...]))
    cm_ref[...] = cm
    gm_ref[...] = _gate_features(cm, place_ref, ones_ref)


def _meta_call(meta, gain, wt, bf, place, ones, tn=512):
    nt = D_MODEL // tn
    const = lambda shape: pl.BlockSpec(shape, lambda j: (0,) * len(shape))
    group = lambda name: _w_tile_spec(tn, lambda j: COL[name] + j * tn)
    return pl.pallas_call(
        _meta_kernel,
        grid=(nt,),
        in_specs=[
            const((N_META, D_MODEL)), const((1, D_MODEL)),
            group("k"), group("v"), group("u"), group("gate_c"), _forget_w_spec(),
            const((1, LANES)), const((3 * LANES, LANES)), const((1, LANES)),
        ],
        out_specs=[
            pl.BlockSpec((N_META, tn), lambda j: (0, j)),
            pl.BlockSpec((N_META, tn), lambda j: (0, j)),
            pl.BlockSpec((N_META, tn), lambda j: (0, j)),
            const((N_META, LANES)),
            const((N_META, LANES)),
        ],
        out_shape=[
            jax.ShapeDtypeStruct((N_META, D_MODEL), BF16),
            jax.ShapeDtypeStruct((N_META, D_MODEL), BF16),
            jax.ShapeDtypeStruct((N_META, D_MODEL), F32),
            jax.ShapeDtypeStruct((N_META, LANES), F32),
            jax.ShapeDtypeStruct((N_META, LANES), BF16),
        ],
        compiler_params=pltpu.CompilerParams(
            dimension_semantics=("arbitrary",), vmem_limit_bytes=VMEM_LIMIT),
        name="meta_proj",
    )(meta, gain, wt, wt, wt, wt, wt, bf, place, ones)


def _norm_kernel(x_ref, gain_ref, wf_ref, bf_ref, cm_ref, place_ref, ones_ref,
                 h_ref, gate_ref, carry_ref, *, chunk):
    @pl.when(pl.program_id(1) == 0)
    def _():
        carry_ref[...] = cm_ref[N_META - 1:N_META, :]

    h = _rmsnorm(x_ref[...], gain_ref[...]).astype(BF16)
    h_ref[...] = h
    logf = _log_sigmoid(_dot_nt(h, wf_ref[...].astype(BF16)) + bf_ref[...])
    tm = logf.shape[0]
    run = carry_ref[...]
    parts = []
    for j in range(tm // chunk):
        c = _cumsum_rows(logf[j * chunk:(j + 1) * chunk]) + run
        run = c[chunk - 1:chunk, :]
        parts.append(c)
    carry_ref[...] = run
    gate_ref[...] = _gate_features(jnp.concatenate(parts, axis=0), place_ref, ones_ref)


def _norm_call(x, gain, wt, bf, cm, place, ones, tm=512, chunk=256):
    b, s, _ = x.shape
    nt = s // tm
    const = lambda shape: pl.BlockSpec(shape, lambda bi, i: (0,) * len(shape))
    return pl.pallas_call(
        functools.partial(_norm_kernel, chunk=chunk),
        grid=(b, nt),
        in_specs=[
            pl.BlockSpec((None, tm, D_MODEL), lambda bi, i: (bi, i, 0)),
            const((1, D_MODEL)),
            _forget_w_spec(),
            const((1, LANES)),
            const((N_META, LANES)),
            const((3 * LANES, LANES)),
            const((1, LANES)),
        ],
        out_specs=[
            pl.BlockSpec((tm, D_MODEL), lambda bi, i: (bi * nt + i, 0)),
            pl.BlockSpec((tm, LANES), lambda bi, i: (bi * nt + i, 0)),
        ],
        out_shape=[
            jax.ShapeDtypeStruct((b * s, D_MODEL), BF16),
            jax.ShapeDtypeStruct((b * s, LANES), BF16),
        ],
        scratch_shapes=[pltpu.VMEM((1, LANES), F32)],
        compiler_params=pltpu.CompilerParams(
            dimension_semantics=("arbitrary", "arbitrary"), vmem_limit_bytes=VMEM_LIMIT),
        name="norm_forget",
    )(x, gain, wt, bf, cm, place, ones)


_PROJ_GROUPS = ("q", "k", "v", "z_att", "m_att", "m_conv")


def _proj_group_row(g):
    row = jnp.int32(COL[_PROJ_GROUPS[0]])
    for n in range(1, len(_PROJ_GROUPS)):
        row = jnp.where(g == n, COL[_PROJ_GROUPS[n]], row)
    return row


def _proj_kernel(h_ref, w_ref, o_ref, w_s, *, row_chunk):
    @pl.when(pl.program_id(2) == 0)
    def _():
        _stage_weight(w_ref, w_s)

    scale = jnp.where(pl.program_id(0) == 0, HEAD_DIM ** -0.5 * LOG2E, 1.0).astype(F32)
    for r in range(0, h_ref.shape[0], row_chunk):
        y = _dot_nt(h_ref[r:r + row_chunk, :], w_s[...])
        o_ref[r:r + row_chunk, :] = (y * scale).astype(BF16)


def _proj_call(h, wt, tm=2048, tn=1024, row_chunk=512):
    r = h.shape[0]
    ng = len(_PROJ_GROUPS)
    return pl.pallas_call(
        functools.partial(_proj_kernel, row_chunk=row_chunk),
        grid=(ng, D_MODEL // tn, r // tm),
        in_specs=[
            pl.BlockSpec((tm, D_MODEL), lambda g, j, i: (i, 0)),
            _w_tile_spec(tn, lambda g, j, i: _proj_group_row(g) + j * tn),
        ],
        out_specs=pl.BlockSpec((None, tm, tn), lambda g, j, i: (g, i, j)),
        out_shape=jax.ShapeDtypeStruct((ng, r, D_MODEL), BF16),
        scratch_shapes=[pltpu.VMEM((tn, D_MODEL), BF16)],
        compiler_params=pltpu.CompilerParams(
            dimension_semantics=("arbitrary", "arbitrary", "arbitrary"),
            vmem_limit_bytes=VMEM_LIMIT),
        name="plain_proj",
    )(h, wt)


_CONV_GROUPS = ("u", "gate_b", "gate_c", "z_conv")


def _conv_kernel(h_ref, wu_ref, wgb_ref, wgc_ref, wzc_ref, gm_ref, cw_ref, o_ref,
                 w_s, gbuf_ref, *, tiles_per_batch, row_chunk):
    i = pl.program_id(1)
    tm = h_ref.shape[0]

    @pl.when(i == 0)
    def _():
        for n, w_ref in enumerate((wu_ref, wgb_ref, wgc_ref, wzc_ref)):
            _stage_weight(w_ref, w_s.at[n])

    @pl.when(i % tiles_per_batch == 0)
    def _():
        gbuf_ref[0:SUBLANES, :] = gm_ref[N_META - SUBLANES:N_META, :]

    for r in range(0, tm, row_chunk):
        h = h_ref[r:r + row_chunk, :]
        g = _dot_nt(h, w_s[2]) * _dot_nt(h, w_s[0])
        first = SUBLANES + r
        gbuf_ref[first:first + row_chunk, :] = g
        conv = (gbuf_ref[first - 2:first - 2 + row_chunk, :] * cw_ref[0:1, :]
                + gbuf_ref[first - 1:first - 1 + row_chunk, :] * cw_ref[1:2, :]
                + g * cw_ref[2:3, :])
        gb = _dot_nt(h, w_s[1])
        zc = _dot_nt(h, w_s[3])
        o_ref[r:r + row_chunk, :] = (gb * conv * _silu(zc)).astype(BF16)
    gbuf_ref[0:SUBLANES, :] = gbuf_ref[tm:tm + SUBLANES, :]


def _conv_call(h, wt, g_meta, conv_w, rows_per_batch, tm=2048, tc=256, row_chunk=512):
    r = h.shape[0]
    w_specs = [_w_tile_spec(tc, lambda j, i, name=name: COL[name] + j * tc)
               for name in _CONV_GROUPS]
    return pl.pallas_call(
        functools.partial(_conv_kernel, tiles_per_batch=rows_per_batch // tm, row_chunk=row_chunk),
        grid=(D_MODEL // tc, r // tm),
        in_specs=[pl.BlockSpec((tm, D_MODEL), lambda j, i: (i, 0))] + w_specs + [
            pl.BlockSpec((N_META, tc), lambda j, i: (0, j)),
            pl.BlockSpec((CONV_K, tc), lambda j, i: (0, j)),
        ],
        out_specs=pl.BlockSpec((tm, tc), lambda j, i: (i, j)),
        out_shape=jax.ShapeDtypeStruct((r, D_MODEL), BF16),
        scratch_shapes=[pltpu.VMEM((len(_CONV_GROUPS), tc, D_MODEL), BF16),
                        pltpu.VMEM((SUBLANES + tm, tc), F32)],
        compiler_params=pltpu.CompilerParams(
            dimension_semantics=("arbitrary", "arbitrary"), vmem_limit_bytes=VMEM_LIMIT),
        name="conv_proj",
    )(h, *([wt] * len(_CONV_GROUPS)), g_meta, conv_w)


def _attn_kernel(q_ref, k_ref, v_ref, z_ref, km_ref, vm_ref, g_ref, gm_ref, eqk_ref,
                 o_ref, qa_ref, kf_ref, va_ref, tri_ref, s_ref, tmax_ref, m_ref, acc_ref,
                 *, tq, tk, ts, rb):
    seq = q_ref.shape[0]
    nq = seq // tq
    assert tq in (tk, 2 * tk)
    nblk = tk // LANES

    tri_ref[...] = jnp.where(
        lax.broadcasted_iota(jnp.int32, (tk, tk), 1) <= lax.broadcasted_iota(jnp.int32, (tk, tk), 0),
        0.0, NEG_BIG)
    kma = jnp.concatenate(
        [km_ref[...], _dot(gm_ref[...], eqk_ref[:, LANES:]).astype(BF16)], axis=1)
    vma = jnp.concatenate([vm_ref[...], jnp.ones((N_META, LANES), BF16)], axis=1)

    def setup(t, carry):
        rows = pl.ds(pl.multiple_of(t * ts, ts), ts)
        feat = _dot(g_ref[rows, :], eqk_ref[...]).astype(BF16)
        kf_ref[rows, :] = feat[:, LANES:]
        qa = jnp.concatenate([q_ref[rows, :], feat[:, :LANES]], axis=1)
        qa_ref[rows, :] = qa
        va_ref[rows, :] = jnp.concatenate([v_ref[rows, :], jnp.ones((ts, LANES), BF16)], axis=1)
        sm = _dot_nt(qa, kma)
        m0 = jnp.max(sm, axis=1, keepdims=True)
        m_ref[rows, :] = jnp.broadcast_to(m0, (ts, LANES))
        acc_ref[rows, :] = _dot(jnp.exp2(sm - m0).astype(BF16), vma)
        return carry

    lax.fori_loop(0, seq // ts, setup, 0, unroll=8)

    def stage_a(row0, nrows, j, slot, tri):
        rows = pl.ds(pl.multiple_of(row0, tk), nrows)
        keys = pl.ds(pl.multiple_of(j * tk, tk), tk)
        ka = jnp.concatenate([k_ref[keys, :], kf_ref[keys, :]], axis=1)
        s = _dot_nt(qa_ref[rows, :], ka)
        for r0 in range(0, nrows, rb):
            part = s[r0:r0 + rb]
            if tri and r0 < tk:
                part = part + tri_ref[r0:r0 + rb, :]
            s_ref[slot, r0:r0 + rb, :] = part
            t = part[:, 0:LANES]
            for c in range(1, nblk):
                t = jnp.maximum(t, part[:, c * LANES:(c + 1) * LANES])
            tmax_ref[slot, r0:r0 + rb, :] = jnp.broadcast_to(
                jnp.max(t, axis=1, keepdims=True), (rb, LANES))

    def stage_b(row0, nrows, j, slot):
        keys = pl.ds(pl.multiple_of(j * tk, tk), tk)
        row0 = pl.multiple_of(row0, tk)
        alphas, ps = [], []
        for r0 in range(0, nrows, rb):
            rows = pl.ds(row0 + r0, rb)
            m_old = m_ref[rows, :]
            m_new = jnp.maximum(m_old, tmax_ref[slot, r0:r0 + rb, :])
            m_ref[rows, :] = m_new
            alphas.append(jnp.exp2(m_old - m_new))
            ps.append(jnp.concatenate(
                [jnp.exp2(s_ref[slot, r0:r0 + rb, c * LANES:(c + 1) * LANES] - m_new).astype(BF16)
                 for c in range(nblk)], axis=1))
        pv = _dot(jnp.concatenate(ps, axis=0), va_ref[keys, :])
        for n, r0 in enumerate(range(0, nrows, rb)):
            rows = pl.ds(row0 + r0, rb)
            alpha = jnp.concatenate([alphas[n], alphas[n]], axis=1)
            acc_ref[rows, :] = alpha * acc_ref[rows, :] + pv[r0:r0 + rb]

    items = []
    for i in range(nq):
        if tq == tk:
            items += [(i * tq, tq, j, False) for j in range(i)] + [(i * tq, tq, i, True)]
        else:
            items += [(i * tq, tq, j, False) for j in range(2 * i)]
            items += [(i * tq, tq, 2 * i, True), (i * tq + tk, tk, 2 * i + 1, True)]

    def finalize(row0, nrows):
        for r0 in range(row0, row0 + nrows, rb):
            o = (acc_ref[r0:r0 + rb, :LANES] / acc_ref[r0:r0 + rb, LANES:]
                 * _silu(z_ref[r0:r0 + rb, :].astype(F32)))
            o_ref[r0:r0 + rb, :] = o.astype(BF16)

    stage_a(*items[0][:3], 0, items[0][3])
    for n, item in enumerate(items):
        if n + 1 < len(items):
            nxt = items[n + 1]
            stage_a(*nxt[:3], 1 - n % 2, nxt[3])
        stage_b(*item[:3], n % 2)
        if item[3]:
            finalize(item[0], tk)


def _attn_call(proj, k_meta, v_meta, gate, gate_meta, eqk, batch, seq, tq=1024, tk=512, ts=512, rb=128):
    r = batch * seq
    head = lambda g: pl.BlockSpec((None, seq, HEAD_DIM), lambda b, h, g=g: (g, b, h))
    return pl.pallas_call(
        functools.partial(_attn_kernel, tq=tq, tk=tk, ts=ts, rb=rb),
        grid=(batch, HEADS),
        in_specs=[
            head(0), head(1), head(2), head(3),
            pl.BlockSpec((N_META, HEAD_DIM), lambda b, h: (0, h)),
            pl.BlockSpec((N_META, HEAD_DIM), lambda b, h: (0, h)),
            pl.BlockSpec((seq, LANES), lambda b, h: (b, 0)),
            pl.BlockSpec((N_META, LANES), lambda b, h: (0, 0)),
            pl.BlockSpec((None, LANES, 2 * LANES), lambda b, h: (h, 0, 0)),
        ],
        out_specs=pl.BlockSpec((seq, HEAD_DIM), lambda b, h: (b, h)),
        out_shape=jax.ShapeDtypeStruct((r, D_MODEL), BF16),
        scratch_shapes=[
            pltpu.VMEM((seq, 2 * LANES), BF16),
            pltpu.VMEM((seq, LANES), BF16),
            pltpu.VMEM((seq, 2 * LANES), BF16),
            pltpu.VMEM((tk, tk), F32),
            pltpu.VMEM((2, tq, tk), F32),
            pltpu.VMEM((2, tq, LANES), F32),
            pltpu.VMEM((seq, LANES), F32),
            pltpu.VMEM((seq, 2 * LANES), F32),
        ],
        compiler_params=pltpu.CompilerParams(
            dimension_semantics=("arbitrary", "arbitrary"), vmem_limit_bytes=VMEM_LIMIT),
        name="fox_attention",
    )(proj, proj, proj, proj, k_meta, v_meta, gate, gate_meta, eqk)


def _stage_cast(src_ref, dst_ref, chunk=256):
    for r in range(0, src_ref.shape[0], chunk):
        dst_ref[r:r + chunk, :] = src_ref[r:r + chunk, :].astype(BF16)


def _merge_kernel(aa_ref, ac_ref, ma_ref, mc_ref, wa_ref, wc_ref, o_ref, wa_s, wc_s, *, row_chunk):
    @pl.when(pl.program_id(1) == 0)
    def _():
        _stage_cast(wa_ref, wa_s)
        _stage_cast(wc_ref, wc_s)

    for r in range(0, aa_ref.shape[0], row_chunk):
        rows = slice(r, r + row_chunk)
        ya = _dot(aa_ref[rows, :], wa_s[...])
        yc = _dot(ac_ref[rows, :], wc_s[...])
        o_ref[rows, :] = (jax.nn.sigmoid(ma_ref[rows, :].astype(F32)) * ya
                          + jax.nn.sigmoid(mc_ref[rows, :].astype(F32)) * yc).astype(BF16)


def _merge_call(a_att, a_conv, proj, w_att_o, w_conv_o, tm=1024, tn=512, row_chunk=512):
    r = a_att.shape[0]
    act = pl.BlockSpec((tm, D_MODEL), lambda j, i: (i, 0))
    gate = lambda g: pl.BlockSpec((None, tm, tn), lambda j, i, g=g: (g, i, j))
    weight = pl.BlockSpec((None, D_MODEL, tn), lambda j, i: (0, 0, j))
    return pl.pallas_call(
        functools.partial(_merge_kernel, row_chunk=row_chunk),
        grid=(D_MODEL // tn, r // tm),
        in_specs=[act, act, gate(_PROJ_GROUPS.index("m_att")), gate(_PROJ_GROUPS.index("m_conv")),
                  weight, weight],
        out_specs=pl.BlockSpec((tm, tn), lambda j, i: (i, j)),
        out_shape=jax.ShapeDtypeStruct((r, D_MODEL), BF16),
        scratch_shapes=[pltpu.VMEM((D_MODEL, tn), BF16), pltpu.VMEM((D_MODEL, tn), BF16)],
        compiler_params=pltpu.CompilerParams(
            dimension_semantics=("arbitrary", "arbitrary"), vmem_limit_bytes=VMEM_LIMIT),
        name="gated_merge",
    )(a_att, a_conv, proj, proj, w_att_o, w_conv_o)


def _final_kernel(m_ref, x_ref, w_ref, gain_ref, o_ref, w_s, *, row_chunk):
    @pl.when(pl.program_id(0) == 0)
    def _():
        _stage_cast(w_ref, w_s)

    for r in range(0, m_ref.shape[0], row_chunk):
        rows = slice(r, r + row_chunk)
        y = x_ref[rows, :] + _dot(m_ref[rows, :], w_s[...])
        o_ref[rows, :] = _rmsnorm(y, gain_ref[...])


def _final_call(merged, x2d, w_out, gain, tm=512, row_chunk=256):
    r = x2d.shape[0]
    row = lambda i: (i, 0)
    return pl.pallas_call(
        functools.partial(_final_kernel, row_chunk=row_chunk),
        grid=(r // tm,),
        in_specs=[
            pl.BlockSpec((tm, D_MODEL), row),
            pl.BlockSpec((tm, D_MODEL), row),
            pl.BlockSpec((None, D_MODEL, D_MODEL), lambda i: (0, 0, 0),
                         pipeline_mode=pl.Buffered(1)),
            pl.BlockSpec((1, D_MODEL), lambda i: (0, 0)),
        ],
        out_specs=pl.BlockSpec((tm, D_MODEL), row),
        out_shape=jax.ShapeDtypeStruct((r, D_MODEL), F32),
        scratch_shapes=[pltpu.VMEM((D_MODEL, D_MODEL), BF16)],
        compiler_params=pltpu.CompilerParams(
            dimension_semantics=("arbitrary",), vmem_limit_bytes=VMEM_LIMIT),
        name="out_proj",
    )(merged, x2d, w_out, gain)


def kernel(x, meta_tokens, norm_gain, w_in, b_f, conv_w, w_att_o, w_conv_o, w_out, final_gain):
    batch, seq, _ = x.shape
    assert norm_gain.shape[0] == 1, "single-layer block"
    assert w_in.shape == (1, D_MODEL, N_IN) and w_in.dtype == F32
    wt = jnp.transpose(w_in[0])
    bf = jnp.pad(b_f[0].astype(F32), (0, LANES - HEADS)).reshape(1, LANES)
    gain =norm_gain[0].reshape(1, D_MODEL).astype(F32)
    fgain = final_gain.reshape(1, D_MODEL).astype(F32)
    place, ones, eqk = _feature_constants()

    k_meta, v_meta, g_meta, cm, gate_meta = _meta_call(
        meta_tokens.astype(F32), gain, wt, bf, place, ones)
    h, gate = _norm_call(x, gain, wt, bf, cm, place, ones)
    proj = _proj_call(h, wt)
    a_conv = _conv_call(h, wt, g_meta, conv_w[0].astype(F32), seq)
    a_att = _attn_call(proj, k_meta, v_meta, gate, gate_meta, eqk, batch, seq)
    merged = _merge_call(a_att, a_conv, proj, w_att_o, w_conv_o)
    out = _final_call(merged, x.reshape(batch * seq, D_MODEL), w_out, fgain)
    return out.reshape(batch, seq, D_MODEL)
```

```python
import functools

import numpy as np

import jax
import jax.numpy as jnp
from jax import lax
from jax.experimental import pallas as pl
from jax.experimental.pallas import tpu as pltpu

D_MODEL = 2048
N_META = 16
HEADS = 16
HEAD_DIM = 128
CONV_K = 3
EPS = 1e-6
LANES = 128
SUBLANES = 8
NEG_BIG = -1e30
LOG2E = 1.4426950408889634
GATE_LANES = LANES // HEADS
VMEM_LIMIT = 56 * 1024 * 1024

_GROUPS = ("q", "k", "v", "f", "z_att", "u", "gate_b", "gate_c", "z_conv", "m_att", "m_conv")
_WIDTHS = (D_MODEL, D_MODEL, D_MODEL, HEADS) + (D_MODEL,) * 7
COL = {name: sum(_WIDTHS[:n]) for n, name in enumerate(_GROUPS)}
N_IN = sum(_WIDTHS)

F32 = jnp.float32
BF16 = jnp.bfloat16


def _dot(a, b):
    return jnp.dot(a, b, preferred_element_type=F32)


def _dot_nt(a, b):
    return lax.dot_general(a, b, (((1,), (1,)), ((), ())), preferred_element_type=F32)


def _rmsnorm(x, gain):
    return x * lax.rsqrt(jnp.mean(x * x, axis=-1, keepdims=True) + EPS) * gain


def _log_sigmoid(x):
    return jnp.minimum(x, 0.0) - jnp.log1p(jnp.exp(-jnp.abs(x)))


def _silu(x):
    return x * jax.nn.sigmoid(x)


def _split3(x):
    hi = x.astype(BF16)
    r1 = x - hi.astype(F32)
    mid = r1.astype(BF16)
    lo = (r1 - mid.astype(F32)).astype(BF16)
    return hi, mid, lo


def _cumsum_rows(x):
    n = x.shape[0]
    tri = (lax.broadcasted_iota(jnp.int32, (n, n), 0)
           >= lax.broadcasted_iota(jnp.int32, (n, n), 1)).astype(BF16)
    hi, mid, lo = _split3(x)
    return _dot(tri, hi) + _dot(tri, mid) + _dot(tri, lo)


def _gate_features(c, place_ref, ones_ref):
    hi, mid, lo = _split3(c * LOG2E)
    cat = jnp.concatenate([hi, mid, lo], axis=1)
    return (_dot(cat, place_ref[...]) + ones_ref[...]).astype(BF16)


def _feature_constants():
    place = np.zeros((3 * LANES, LANES), np.float32)
    ones = np.zeros((1, LANES), np.float32)
    eq = np.zeros((HEADS, LANES, LANES), np.float32)
    ek = np.zeros((HEADS, LANES, LANES), np.float32)
    for h in range(HEADS):
        base = GATE_LANES * h
        ones[0, base + 3] = 1.0
        for f in range(3):
            place[f * LANES + h, base + f] = 1.0
            eq[h, base + f, f] = 1.0
            eq[h, base + 3, 3 + f] = 1.0
            ek[h, base + 3, f] = 1.0
            ek[h, base + f, 3 + f] = -1.0
    return (jnp.asarray(place, BF16), jnp.asarray(ones, F32),
            jnp.asarray(np.concatenate([eq, ek], axis=2), BF16))


def _w_tile_spec(tn, row_of):
    assert all(c % HEADS == 0 for c in COL.values()) and tn % HEADS == 0
    return pl.BlockSpec((pl.Element(tn), pl.Element(D_MODEL)),
                        lambda *ids: (pl.multiple_of(row_of(*ids), HEADS), 0))


def _forget_w_spec():
    assert COL["f"] % LANES == 0
    return pl.BlockSpec((LANES, D_MODEL), lambda *ids: (COL["f"] // LANES, 0))


def _stage_weight(src_ref, dst_ref, chunk=128):
    for r in range(0, src_ref.shape[0], chunk):
        dst_ref[r:r + chunk, :] = src_ref[r:r + chunk, :].astype(BF16)


def _meta_kernel(meta_ref, gain_ref, wk_ref, wv_ref, wu_ref, wgc_ref, wf_ref, bf_ref,
                 place_ref, ones_ref, k_ref, v_ref, g_ref, cm_ref, gm_ref):
    h = _rmsnorm(meta_ref[...], gain_ref[...]).astype(BF16)
    proj = lambda w_ref: _dot_nt(h, w_ref[...].astype(BF16))
    k_ref[...] = proj(wk_ref).astype(BF16)
    v_ref[...] = proj(wv_ref).astype(BF16)
    g_ref[...] = proj(wgc_ref) * proj(wu_ref)
    cm = _cumsum_rows(_log_sigmoid(proj(wf_ref) + bf_ref[...]))
    cm_ref[...] = cm
    gm_ref[...] = _gate_features(cm, place_ref, ones_ref)


def _meta_call(meta, gain, wt, bf, place, ones, tn=512):
    nt = D_MODEL // tn
    const = lambda shape: pl.BlockSpec(shape, lambda j: (0,) * len(shape))
    group = lambda name: _w_tile_spec(tn, lambda j: COL[name] + j * tn)
    return pl.pallas_call(
        _meta_kernel,
        grid=(nt,),
        in_specs=[
            const((N_META, D_MODEL)), const((1, D_MODEL)),
            group("k"), group("v"), group("u"), group("gate_c"), _forget_w_spec(),
            const((1, LANES)), const((3 * LANES, LANES)), const((1, LANES)),
        ],
        out_specs=[
            pl.BlockSpec((N_META, tn), lambda j: (0, j)),
            pl.BlockSpec((N_META, tn), lambda j: (0, j)),
            pl.BlockSpec((N_META, tn), lambda j: (0, j)),
            const((N_META, LANES)),
            const((N_META, LANES)),
        ],
        out_shape=[
            jax.ShapeDtypeStruct((N_META, D_MODEL), BF16),
            jax.ShapeDtypeStruct((N_META, D_MODEL), BF16),
            jax.ShapeDtypeStruct((N_META, D_MODEL), F32),
            jax.ShapeDtypeStruct((N_META, LANES), F32),
            jax.ShapeDtypeStruct((N_META, LANES), BF16),
        ],
        compiler_params=pltpu.CompilerParams(
            dimension_semantics=("arbitrary",), vmem_limit_bytes=VMEM_LIMIT),
        name="meta_proj",
    )(meta, gain, wt, wt, wt, wt, wt, bf, place, ones)


def _norm_kernel(x_ref, gain_ref, wf_ref, bf_ref, cm_ref, place_ref, ones_ref,
                 h_ref, gate_ref, carry_ref, *, chunk):
    @pl.when(pl.program_id(1) == 0)
    def _():
        carry_ref[...] = cm_ref[N_META - 1:N_META, :]

    h = _rmsnorm(x_ref[...], gain_ref[...]).astype(BF16)
    h_ref[...] = h
    logf = _log_sigmoid(_dot_nt(h, wf_ref[...].astype(BF16)) + bf_ref[...])
    tm = logf.shape[0]
    run = carry_ref[...]
    parts = []
    for j in range(tm // chunk):
        c = _cumsum_rows(logf[j * chunk:(j + 1) * chunk]) + run
        run = c[chunk - 1:chunk, :]
        parts.append(c)
    carry_ref[...] = run
    gate_ref[...] = _gate_features(jnp.concatenate(parts, axis=0), place_ref, ones_ref)


def _norm_call(x, gain, wt, bf, cm, place, ones, tm=512, chunk=256):
    b, s, _ = x.shape
    nt = s // tm
    const = lambda shape: pl.BlockSpec(shape, lambda bi, i: (0,) * len(shape))
    return pl.pallas_call(
        functools.partial(_norm_kernel, chunk=chunk),
        grid=(b, nt),
        in_specs=[
            pl.BlockSpec((None, tm, D_MODEL), lambda bi, i: (bi, i, 0)),
            const((1, D_MODEL)),
            _forget_w_spec(),
            const((1, LANES)),
            const((N_META, LANES)),
            const((3 * LANES, LANES)),
            const((1, LANES)),
        ],
        out_specs=[
            pl.BlockSpec((tm, D_MODEL), lambda bi, i: (bi * nt + i, 0)),
            pl.BlockSpec((tm, LANES), lambda bi, i: (bi * nt + i, 0)),
        ],
        out_shape=[
            jax.ShapeDtypeStruct((b * s, D_MODEL), BF16),
            jax.ShapeDtypeStruct((b * s, LANES), BF16),
        ],
        scratch_shapes=[pltpu.VMEM((1, LANES), F32)],
        compiler_params=pltpu.CompilerParams(
            dimension_semantics=("arbitrary", "arbitrary"), vmem_limit_bytes=VMEM_LIMIT),
        name="norm_forget",
    )(x, gain, wt, bf, cm, place, ones)


_PROJ_GROUPS = ("q", "k", "v", "z_att", "m_att", "m_conv")


def _proj_group_row(g):
    row = jnp.int32(COL[_PROJ_GROUPS[0]])
    for n in range(1, len(_PROJ_GROUPS)):
        row = jnp.where(g == n, COL[_PROJ_GROUPS[n]], row)
    return row


def _proj_kernel(h_ref, w_ref, o_ref, w_s, *, row_chunk):
    @pl.when(pl.program_id(2) == 0)
    def _():
        _stage_weight(w_ref, w_s)

    scale = jnp.where(pl.program_id(0) == 0, HEAD_DIM ** -0.5 * LOG2E, 1.0).astype(F32)
    for r in range(0, h_ref.shape[0], row_chunk):
        y = _dot_nt(h_ref[r:r + row_chunk, :], w_s[...])
        o_ref[r:r + row_chunk, :] = (y * scale).astype(BF16)


def _proj_call(h, wt, tm=2048, tn=1024, row_chunk=512):
    r = h.shape[0]
    ng = len(_PROJ_GROUPS)
    return pl.pallas_call(
        functools.partial(_proj_kernel, row_chunk=row_chunk),
        grid=(ng, D_MODEL // tn, r // tm),
        in_specs=[
            pl.BlockSpec((tm, D_MODEL), lambda g, j, i: (i, 0)),
            _w_tile_spec(tn, lambda g, j, i: _proj_group_row(g) + j * tn),
        ],
        out_specs=pl.BlockSpec((None, tm, tn), lambda g, j, i: (g, i, j)),
        out_shape=jax.ShapeDtypeStruct((ng, r, D_MODEL), BF16),
        scratch_shapes=[pltpu.VMEM((tn, D_MODEL), BF16)],
        compiler_params=pltpu.CompilerParams(
            dimension_semantics=("arbitrary", "arbitrary", "arbitrary"),
            vmem_limit_bytes=VMEM_LIMIT),
        name="plain_proj",
    )(h, wt)


_CONV_GROUPS = ("u", "gate_b", "gate_c", "z_conv")


def _conv_kernel(h_ref, wu_ref, wgb_ref, wgc_ref, wzc_ref, gm_ref, cw_ref,
                 wa_ref, wc_ref, wo_ref, o_ref, wa_out, wc_out, wo_out,
                 w_s, gbuf_ref, *, tiles_per_batch, row_chunk):
    i = pl.program_id(1)
    tm = h_ref.shape[0]
    for src, dst in ((wa_ref, wa_out), (wc_ref, wc_out), (wo_ref, wo_out)):
        dst[...] = src[...].astype(BF16)

    @pl.when(i == 0)
    def _():
        for n, w_ref in enumerate((wu_ref, wgb_ref, wgc_ref, wzc_ref)):
            _stage_weight(w_ref, w_s.at[n])

    @pl.when(i % tiles_per_batch == 0)
    def _():
        gbuf_ref[0:SUBLANES, :] = gm_ref[N_META - SUBLANES:N_META, :]

    for r in range(0, tm, row_chunk):
        h = h_ref[r:r + row_chunk, :]
        g = _dot_nt(h, w_s[2]) * _dot_nt(h, w_s[0])
        first = SUBLANES + r
        gbuf_ref[first:first + row_chunk, :] = g
        conv = (gbuf_ref[first - 2:first - 2 + row_chunk, :] * cw_ref[0:1, :]
                + gbuf_ref[first - 1:first - 1 + row_chunk, :] * cw_ref[1:2, :]
                + g * cw_ref[2:3, :])
        gb = _dot_nt(h, w_s[1])
        zc = _dot_nt(h, w_s[3])
        o_ref[r:r + row_chunk, :] = (gb * conv * _silu(zc)).astype(BF16)
    gbuf_ref[0:SUBLANES, :] = gbuf_ref[tm:tm + SUBLANES, :]


def _conv_call(h, wt, g_meta, conv_w, out_weights, rows_per_batch, tm=2048, tc=256, row_chunk=512):
    r = h.shape[0]
    nrow = r // tm
    steps = (D_MODEL // tc) * nrow
    slab = D_MODEL // steps
    assert slab * steps == D_MODEL and slab % (2 * SUBLANES) == 0
    w_specs = [_w_tile_spec(tc, lambda j, i, name=name: COL[name] + j * tc)
               for name in _CONV_GROUPS]
    slab_in = pl.BlockSpec((None, slab, D_MODEL), lambda j, i: (0, j * nrow + i, 0))
    slab_out = pl.BlockSpec((slab, D_MODEL), lambda j, i: (j * nrow + i, 0))
    cast_shape = jax.ShapeDtypeStruct((D_MODEL, D_MODEL), BF16)
    return pl.pallas_call(
        functools.partial(_conv_kernel, tiles_per_batch=rows_per_batch // tm, row_chunk=row_chunk),
        grid=(D_MODEL // tc, nrow),
        in_specs=[pl.BlockSpec((tm, D_MODEL), lambda j, i: (i, 0))] + w_specs + [
            pl.BlockSpec((N_META, tc), lambda j, i: (0, j)),
            pl.BlockSpec((CONV_K, tc), lambda j, i: (0, j)),
            slab_in, slab_in, slab_in,
        ],
        out_specs=[pl.BlockSpec((tm, tc), lambda j, i: (i, j)), slab_out, slab_out, slab_out],
        out_shape=[jax.ShapeDtypeStruct((r, D_MODEL), BF16), cast_shape, cast_shape, cast_shape],
        scratch_shapes=[pltpu.VMEM((len(_CONV_GROUPS), tc, D_MODEL), BF16),
                        pltpu.VMEM((SUBLANES + tm, tc), F32)],
        compiler_params=pltpu.CompilerParams(
            dimension_semantics=("arbitrary", "arbitrary"), vmem_limit_bytes=VMEM_LIMIT),
        name="conv_proj",
    )(h, *([wt] * len(_CONV_GROUPS)), g_meta, conv_w, *out_weights)


def _attn_kernel(q_ref, k_ref, v_ref, z_ref, km_ref, vm_ref, g_ref, gm_ref, eqk_ref,
                 o_ref, qa_ref, kf_ref, va_ref, tri_ref, s_ref, tmax_ref, m_ref, acc_ref,
                 *, tq, tk, ts, rb):
    seq = q_ref.shape[0]
    nq = seq // tq
    assert tq in (tk, 2 * tk)
    nblk = tk // LANES

    tri_ref[...] = jnp.where(
        lax.broadcasted_iota(jnp.int32, (tk, tk), 1) <= lax.broadcasted_iota(jnp.int32, (tk, tk), 0),
        0.0, NEG_BIG)
    kma = jnp.concatenate(
        [km_ref[...], _dot(gm_ref[...], eqk_ref[:, LANES:]).astype(BF16)], axis=1)
    vma = jnp.concatenate([vm_ref[...], jnp.ones((N_META, LANES), BF16)], axis=1)

    def setup(t, carry):
        rows = pl.ds(pl.multiple_of(t * ts, ts), ts)
        feat = _dot(g_ref[rows, :], eqk_ref[...]).astype(BF16)
        kf_ref[rows, :] = feat[:, LANES:]
        qa = jnp.concatenate([q_ref[rows, :], feat[:, :LANES]], axis=1)
        qa_ref[rows, :] = qa
        va_ref[rows, :] = jnp.concatenate([v_ref[rows, :], jnp.ones((ts, LANES), BF16)], axis=1)
        sm = _dot_nt(qa, kma)
        m0 = jnp.max(sm, axis=1, keepdims=True)
        m_ref[rows, :] = jnp.broadcast_to(m0, (ts, LANES))
        acc_ref[rows, :] = _dot(jnp.exp2(sm - m0).astype(BF16), vma)
        return carry

    lax.fori_loop(0, seq // ts, setup, 0, unroll=8)

    def stage_a(row0, nrows, j, slot, tri):
        rows = pl.ds(pl.multiple_of(row0, tk), nrows)
        keys = pl.ds(pl.multiple_of(j * tk, tk), tk)
        ka = jnp.concatenate([k_ref[keys, :], kf_ref[keys, :]], axis=1)
        s = _dot_nt(qa_ref[rows, :], ka)
        for r0 in range(0, nrows, rb):
            part = s[r0:r0 + rb]
            if tri and r0 < tk:
                part = part + tri_ref[r0:r0 + rb, :]
            s_ref[slot, r0:r0 + rb, :] = part
            t = part[:, 0:LANES]
            for c in range(1, nblk):
                t = jnp.maximum(t, part[:, c * LANES:(c + 1) * LANES])
            tmax_ref[slot, r0:r0 + rb, :] = jnp.broadcast_to(
                jnp.max(t, axis=1, keepdims=True), (rb, LANES))

    def stage_b(row0, nrows, j, slot):
        keys = pl.ds(pl.multiple_of(j * tk, tk), tk)
        row0 = pl.multiple_of(row0, tk)
        alphas, ps = [], []
        for r0 in range(0, nrows, rb):
            rows = pl.ds(row0 + r0, rb)
            m_old = m_ref[rows, :]
            m_new = jnp.maximum(m_old, tmax_ref[slot, r0:r0 + rb, :])
            m_ref[rows, :] = m_new
            alphas.append(jnp.exp2(m_old - m_new))
            ps.append(jnp.concatenate(
                [jnp.exp2(s_ref[slot, r0:r0 + rb, c * LANES:(c + 1) * LANES] - m_new).astype(BF16)
                 for c in range(nblk)], axis=1))
        pv = _dot(jnp.concatenate(ps, axis=0), va_ref[keys, :])
        for n, r0 in enumerate(range(0, nrows, rb)):
            rows = pl.ds(row0 + r0, rb)
            alpha = jnp.concatenate([alphas[n], alphas[n]], axis=1)
            acc_ref[rows, :] = alpha * acc_ref[rows, :] + pv[r0:r0 + rb]

    items = []
    for i in range(nq):
        if tq == tk:
            items += [(i * tq, tq, j, False) for j in range(i)] + [(i * tq, tq, i, True)]
        else:
            items += [(i * tq, tq, j, False) for j in range(2 * i)]
            items += [(i * tq, tq, 2 * i, True), (i * tq + tk, tk, 2 * i + 1, True)]

    def finalize(row0, nrows):
        for r0 in range(row0, row0 + nrows, rb):
            o = (acc_ref[r0:r0 + rb, :LANES] / acc_ref[r0:r0 + rb, LANES:]
                 * _silu(z_ref[r0:r0 + rb, :].astype(F32)))
            o_ref[r0:r0 + rb, :] = o.astype(BF16)

    stage_a(*items[0][:3], 0, items[0][3])
    for n, item in enumerate(items):
        if n + 1 < len(items):
            nxt = items[n + 1]
            stage_a(*nxt[:3], 1 - n % 2, nxt[3])
        stage_b(*item[:3], n % 2)
        if item[3]:
            finalize(item[0], tk)


def _attn_call(proj, k_meta, v_meta, gate, gate_meta, eqk, batch, seq, tq=1024, tk=512, ts=512, rb=128):
    r = batch * seq
    head = lambda g: pl.BlockSpec((None, seq, HEAD_DIM), lambda b, h, g=g: (g, b, h))
    return pl.pallas_call(
        functools.partial(_attn_kernel, tq=tq, tk=tk, ts=ts, rb=rb),
        grid=(batch, HEADS),
        in_specs=[
            head(0), head(1), head(2), head(3),
            pl.BlockSpec((N_META, HEAD_DIM), lambda b, h: (0, h)),
            pl.BlockSpec((N_META, HEAD_DIM), lambda b, h: (0, h)),
            pl.BlockSpec((seq, LANES), lambda b, h: (b, 0)),
            pl.BlockSpec((N_META, LANES), lambda b, h: (0, 0)),
            pl.BlockSpec((None, LANES, 2 * LANES), lambda b, h: (h, 0, 0)),
        ],
        out_specs=pl.BlockSpec((seq, HEAD_DIM), lambda b, h: (b, h)),
        out_shape=jax.ShapeDtypeStruct((r, D_MODEL), BF16),
        scratch_shapes=[
            pltpu.VMEM((seq, 2 * LANES), BF16),
            pltpu.VMEM((seq, LANES), BF16),
            pltpu.VMEM((seq, 2 * LANES), BF16),
            pltpu.VMEM((tk, tk), F32),
            pltpu.VMEM((2, tq, tk), F32),
            pltpu.VMEM((2, tq, LANES), F32),
            pltpu.VMEM((seq, LANES), F32),
            pltpu.VMEM((seq, 2 * LANES), F32),
        ],
        compiler_params=pltpu.CompilerParams(
            dimension_semantics=("arbitrary", "arbitrary"), vmem_limit_bytes=VMEM_LIMIT),
        name="fox_attention",
    )(proj, proj, proj, proj, k_meta, v_meta, gate, gate_meta, eqk)


def _out_kernel(aa_ref, ac_ref, ma_ref, mc_ref, x_ref, wa_ref, wc_ref, wo_ref, gain_ref, o_ref):
    ya = _dot(aa_ref[...], wa_ref[...])
    yc = _dot(ac_ref[...], wc_ref[...])
    merged = (jax.nn.sigmoid(ma_ref[...].astype(F32)) * ya
              + jax.nn.sigmoid(mc_ref[...].astype(F32)) * yc)
    y = x_ref[...] + _dot(merged.astype(BF16), wo_ref[...])
    o_ref[...] = _rmsnorm(y, gain_ref[...])


def _out_call(a_att, a_conv, proj, x2d, wa, wc, wo, gain, tm=256):
    r = x2d.shape[0]
    row = lambda i: (i, 0)
    act = pl.BlockSpec((tm, D_MODEL), row)
    gate = lambda name: pl.BlockSpec((None, tm, D_MODEL),
                                     lambda i, g=_PROJ_GROUPS.index(name): (g, i, 0))
    weight = pl.BlockSpec((D_MODEL, D_MODEL), lambda i: (0, 0), pipeline_mode=pl.Buffered(1))
    return pl.pallas_call(
        _out_kernel,
        grid=(r // tm,),
        in_specs=[act, act, gate("m_att"), gate("m_conv"), act, weight, weight, weight,
                  pl.BlockSpec((1, D_MODEL), lambda i: (0, 0))],
        out_specs=act,
        out_shape=jax.ShapeDtypeStruct((r, D_MODEL), F32),
        compiler_params=pltpu.CompilerParams(
            dimension_semantics=("arbitrary",), vmem_limit_bytes=VMEM_LIMIT),
        name="out_proj",
    )(a_att, a_conv, proj, proj, x2d, wa, wc, wo, gain)


def kernel(x, meta_tokens, norm_gain, w_in, b_f, conv_w, w_att_o, w_conv_o, w_out, final_gain):
    batch, seq, _ = x.shape
    assert norm_gain.shape[0] == 1, "single-layer block"
    assert w_in.shape == (1, D_MODEL, N_IN) and w_in.dtype == F32
    wt = jnp.transpose(w_in[0])
    bf = jnp.pad(b_f[0].astype(F32), (0, LANES - HEADS)).reshape(1, LANES)
    gain = norm_gain[0].reshape(1, D_MODEL).astype(F32)
    fgain = final_gain.reshape(1, D_MODEL).astype(F32)
    place, ones, eqk = _feature_constants()

    k_meta, v_meta, g_meta, cm, gate_meta = _meta_call(
        meta_tokens.astype(F32), gain, wt, bf, place, ones)
    h, gate = _norm_call(x, gain, wt, bf, cm, place, ones)
    proj = _proj_call(h, wt)
    a_conv, wa, wc, wo = _conv_call(h, wt, g_meta, conv_w[0].astype(F32),
                                    (w_att_o, w_conv_o, w_out), seq)
    a_att = _attn_call(proj, k_meta, v_meta, gate, gate_meta, eqk, batch, seq)
    out = _out_call(a_att, a_conv, proj, x.reshape(batch * seq, D_MODEL), wa, wc, wo, fgain)
    return out.reshape(batch, seq, D_MODEL)
```

```python
import functools

import numpy as np

import jax
import jax.numpy as jnp
from jax import lax
from jax.experimental import pallas as pl
from jax.experimental.pallas import tpu as pltpu

D_MODEL = 2048
N_META = 16
HEADS = 16
HEAD_DIM = 128
CONV_K = 3
EPS = 1e-6
LANES = 128
SUBLANES = 8
NEG_BIG = -1e30
LOG2E = 1.4426950408889634
GATE_LANES = LANES // HEADS
VMEM_LIMIT = 56 * 1024 * 1024

_GROUPS = ("q", "k", "v", "f", "z_att", "u", "gate_b", "gate_c", "z_conv", "m_att", "m_conv")
_WIDTHS = (D_MODEL, D_MODEL, D_MODEL, HEADS) + (D_MODEL,) * 7
COL = {name: sum(_WIDTHS[:n]) for n, name in enumerate(_GROUPS)}
N_IN = sum(_WIDTHS)

F32 = jnp.float32
BF16 = jnp.bfloat16


def _dot(a, b):
    return jnp.dot(a, b, preferred_element_type=F32)


def _dot_nt(a, b):
    return lax.dot_general(a, b, (((1,), (1,)), ((), ())), preferred_element_type=F32)


def _rmsnorm(x, gain):
    return x * lax.rsqrt(jnp.mean(x * x, axis=-1, keepdims=True) + EPS) * gain


def _log_sigmoid(x):
    return jnp.minimum(x, 0.0) - jnp.log1p(jnp.exp(-jnp.abs(x)))


def _silu(x):
    return x * jax.nn.sigmoid(x)


def _split3(x):
    hi = x.astype(BF16)
    r1 = x - hi.astype(F32)
    mid = r1.astype(BF16)
    lo = (r1 - mid.astype(F32)).astype(BF16)
    return hi, mid, lo


def _cumsum_rows(x):
    n = x.shape[0]
    tri = (lax.broadcasted_iota(jnp.int32, (n, n), 0)
           >= lax.broadcasted_iota(jnp.int32, (n, n), 1)).astype(BF16)
    hi, mid, lo = _split3(x)
    return _dot(tri, hi) + _dot(tri, mid) + _dot(tri, lo)


def _gate_features(c, place_ref, ones_ref):
    hi, mid, lo = _split3(c * LOG2E)
    cat = jnp.concatenate([hi, mid, lo], axis=1)
    return (_dot(cat, place_ref[...]) + ones_ref[...]).astype(BF16)


def _feature_constants():
    place = np.zeros((3 * LANES, LANES), np.float32)
    ones = np.zeros((1, LANES), np.float32)
    eq = np.zeros((HEADS, LANES, LANES), np.float32)
    ek = np.zeros((HEADS, LANES, LANES), np.float32)
    for h in range(HEADS):
        base = GATE_LANES * h
        ones[0, base + 3] = 1.0
        for f in range(3):
            place[f * LANES + h, base + f] = 1.0
            eq[h, base + f, f] = 1.0
            eq[h, base + 3, 3 + f] = 1.0
            ek[h, base + 3, f] = 1.0
            ek[h, base + f, 3 + f] = -1.0
    return (jnp.asarray(place, BF16), jnp.asarray(ones, F32),
            jnp.asarray(np.concatenate([eq, ek], axis=2), BF16))


def _w_tile_spec(tn, row_of):
    assert all(c % HEADS == 0 for c in COL.values()) and tn % HEADS == 0
    return pl.BlockSpec((pl.Element(tn), pl.Element(D_MODEL)),
                        lambda *ids: (pl.multiple_of(row_of(*ids), HEADS), 0))


def _forget_w_spec():
    assert COL["f"] % LANES == 0
    return pl.BlockSpec((LANES, D_MODEL), lambda *ids: (COL["f"] // LANES, 0))


def _stage_weight(src_ref, dst_ref, chunk=128):
    for r in range(0, src_ref.shape[0], chunk):
        dst_ref[r:r + chunk, :] = src_ref[r:r + chunk, :].astype(BF16)


def _meta_kernel(meta_ref, gain_ref, wk_ref, wv_ref, wu_ref, wgc_ref, wf_ref, bf_ref,
                 place_ref, ones_ref, k_ref, v_ref, g_ref, cm_ref, gm_ref):
    h = _rmsnorm(meta_ref[...], gain_ref[...]).astype(BF16)
    proj = lambda w_ref: _dot_nt(h, w_ref[...].astype(BF16))
    k_ref[...] = proj(wk_ref).astype(BF16)
    v_ref[...] = proj(wv_ref).astype(BF16)
    g_ref[...] = proj(wgc_ref) * proj(wu_ref)
    cm = _cumsum_rows(_log_sigmoid(proj(wf_ref) + bf_ref[...]))
    cm_ref[...] = cm
    gm_ref[...] = _gate_features(cm, place_ref, ones_ref)


def _meta_call(meta, gain, wt, bf, place, ones, tn=512):
    nt = D_MODEL // tn
    const = lambda shape: pl.BlockSpec(shape, lambda j: (0,) * len(shape))
    group = lambda name: _w_tile_spec(tn, lambda j: COL[name] + j * tn)
    return pl.pallas_call(
        _meta_kernel,
        grid=(nt,),
        in_specs=[
            const((N_META, D_MODEL)), const((1, D_MODEL)),
            group("k"), group("v"), group("u"), group("gate_c"), _forget_w_spec(),
            const((1, LANES)), const((3 * LANES, LANES)), const((1, LANES)),
        ],
        out_specs=[
            pl.BlockSpec((N_META, tn), lambda j: (0, j)),
            pl.BlockSpec((N_META, tn), lambda j: (0, j)),
            pl.BlockSpec((N_META, tn), lambda j: (0, j)),
            const((N_META, LANES)),
            const((N_META, LANES)),
        ],
        out_shape=[
            jax.ShapeDtypeStruct((N_META, D_MODEL), BF16),
            jax.ShapeDtypeStruct((N_META, D_MODEL), BF16),
            jax.ShapeDtypeStruct((N_META, D_MODEL), F32),
            jax.ShapeDtypeStruct((N_META, LANES), F32),
            jax.ShapeDtypeStruct((N_META, LANES), BF16),
        ],
        compiler_params=pltpu.CompilerParams(
            dimension_semantics=("arbitrary",), vmem_limit_bytes=VMEM_LIMIT),
        name="meta_proj",
    )(meta, gain, wt, wt, wt, wt, wt, bf, place, ones)


def _norm_kernel(x_ref, gain_ref, wf_ref, bf_ref, cm_ref, place_ref, ones_ref,
                 h_ref, gate_ref, carry_ref, *, chunk):
    @pl.when(pl.program_id(1) == 0)
    def _():
        carry_ref[...] = cm_ref[N_META - 1:N_META, :]

    h = _rmsnorm(x_ref[...], gain_ref[...]).astype(BF16)
    h_ref[...] = h
    logf = _log_sigmoid(_dot_nt(h, wf_ref[...].astype(BF16)) + bf_ref[...])
    tm = logf.shape[0]
    run = carry_ref[...]
    parts = []
    for j in range(tm // chunk):
        c = _cumsum_rows(logf[j * chunk:(j + 1) * chunk]) + run
        run = c[chunk - 1:chunk, :]
        parts.append(c)
    carry_ref[...] = run
    gate_ref[...] = _gate_features(jnp.concatenate(parts, axis=0), place_ref, ones_ref)


def _norm_call(x, gain, wt, bf, cm, place, ones, tm=1024, chunk=256):
    b, s, _ = x.shape
    nt = s // tm
    const = lambda shape: pl.BlockSpec(shape, lambda bi, i: (0,) * len(shape))
    return pl.pallas_call(
        functools.partial(_norm_kernel, chunk=chunk),
        grid=(b, nt),
        in_specs=[
            pl.BlockSpec((None, tm, D_MODEL), lambda bi, i: (bi, i, 0)),
            const((1, D_MODEL)),
            _forget_w_spec(),
            const((1, LANES)),
            const((N_META, LANES)),
            const((3 * LANES, LANES)),
            const((1, LANES)),
        ],
        out_specs=[
            pl.BlockSpec((tm, D_MODEL), lambda bi, i: (bi * nt + i, 0)),
            pl.BlockSpec((tm, LANES), lambda bi, i: (bi * nt + i, 0)),
        ],
        out_shape=[
            jax.ShapeDtypeStruct((b * s, D_MODEL), BF16),
            jax.ShapeDtypeStruct((b * s, LANES), BF16),
        ],
        scratch_shapes=[pltpu.VMEM((1, LANES), F32)],
        compiler_params=pltpu.CompilerParams(
            dimension_semantics=("arbitrary", "arbitrary"), vmem_limit_bytes=VMEM_LIMIT),
        name="norm_forget",
    )(x, gain, wt, bf, cm, place, ones)


_PROJ_GROUPS = ("q", "k", "v", "z_att", "m_att", "m_conv")


def _proj_group_row(g):
    row = jnp.int32(COL[_PROJ_GROUPS[0]])
    for n in range(1, len(_PROJ_GROUPS)):
        row = jnp.where(g == n, COL[_PROJ_GROUPS[n]], row)
    return row


def _proj_kernel(h_ref, w_ref, o_ref, w_s, *, row_chunk):
    @pl.when(pl.program_id(2) == 0)
    def _():
        _stage_weight(w_ref, w_s)

    scale = jnp.where(pl.program_id(0) == 0, HEAD_DIM ** -0.5 * LOG2E, 1.0).astype(F32)
    for r in range(0, h_ref.shape[0], row_chunk):
        y = _dot_nt(h_ref[r:r + row_chunk, :], w_s[...])
        o_ref[r:r + row_chunk, :] = (y * scale).astype(BF16)


def _proj_call(h, wt, tm=2048, tn=1024, row_chunk=512):
    r = h.shape[0]
    ng = len(_PROJ_GROUPS)
    return pl.pallas_call(
        functools.partial(_proj_kernel, row_chunk=row_chunk),
        grid=(ng, D_MODEL // tn, r // tm),
        in_specs=[
            pl.BlockSpec((tm, D_MODEL), lambda g, j, i: (i, 0)),
            _w_tile_spec(tn, lambda g, j, i: _proj_group_row(g) + j * tn),
        ],
        out_specs=pl.BlockSpec((None, tm, tn), lambda g, j, i: (g, i, j)),
        out_shape=jax.ShapeDtypeStruct((ng, r, D_MODEL), BF16),
        scratch_shapes=[pltpu.VMEM((tn, D_MODEL), BF16)],
        compiler_params=pltpu.CompilerParams(
            dimension_semantics=("arbitrary", "arbitrary", "arbitrary"),
            vmem_limit_bytes=VMEM_LIMIT),
        name="plain_proj",
    )(h, wt)


_CONV_GROUPS = ("u", "gate_b", "gate_c", "z_conv")


def _conv_kernel(h_ref, wu_ref, wgb_ref, wgc_ref, wzc_ref, gm_ref, cw_ref,
                 wa_ref, wc_ref, wo_ref, o_ref, wa_out, wc_out, wo_out,
                 w_s, gbuf_ref, *, tiles_per_batch, row_chunk):
    i = pl.program_id(1)
    tm = h_ref.shape[0]
    for src, dst in ((wa_ref, wa_out), (wc_ref, wc_out), (wo_ref, wo_out)):
        dst[...] = src[...].astype(BF16)

    @pl.when(i == 0)
    def _():
        for n, w_ref in enumerate((wu_ref, wgb_ref, wgc_ref, wzc_ref)):
            _stage_weight(w_ref, w_s.at[n])

    @pl.when(i % tiles_per_batch == 0)
    def _():
        gbuf_ref[0:SUBLANES, :] = gm_ref[N_META - SUBLANES:N_META, :]

    for r in range(0, tm, row_chunk):
        h = h_ref[r:r + row_chunk, :]
        g = _dot_nt(h, w_s[2]) * _dot_nt(h, w_s[0])
        first = SUBLANES + r
        gbuf_ref[first:first + row_chunk, :] = g
        conv = (gbuf_ref[first - 2:first - 2 + row_chunk, :] * cw_ref[0:1, :]
                + gbuf_ref[first - 1:first - 1 + row_chunk, :] * cw_ref[1:2, :]
                + g * cw_ref[2:3, :])
        gb = _dot_nt(h, w_s[1])
        zc = _dot_nt(h, w_s[3])
        o_ref[r:r + row_chunk, :] = (gb * conv * _silu(zc)).astype(BF16)
    gbuf_ref[0:SUBLANES, :] = gbuf_ref[tm:tm + SUBLANES, :]


def _conv_call(h, wt, g_meta, conv_w, out_weights, rows_per_batch, tm=2048, tc=256, row_chunk=512):
    r = h.shape[0]
    nrow = r // tm
    steps = (D_MODEL // tc) * nrow
    slab = D_MODEL // steps
    assert slab * steps == D_MODEL and slab % (2 * SUBLANES) == 0
    w_specs = [_w_tile_spec(tc, lambda j, i, name=name: COL[name] + j * tc)
               for name in _CONV_GROUPS]
    slab_in = pl.BlockSpec((None, slab, D_MODEL), lambda j, i: (0, j * nrow + i, 0))
    slab_out = pl.BlockSpec((slab, D_MODEL), lambda j, i: (j * nrow + i, 0))
    cast_shape = jax.ShapeDtypeStruct((D_MODEL, D_MODEL), BF16)
    return pl.pallas_call(
        functools.partial(_conv_kernel, tiles_per_batch=rows_per_batch // tm, row_chunk=row_chunk),
        grid=(D_MODEL // tc, nrow),
        in_specs=[pl.BlockSpec((tm, D_MODEL), lambda j, i: (i, 0))] + w_specs + [
            pl.BlockSpec((N_META, tc), lambda j, i: (0, j)),
            pl.BlockSpec((CONV_K, tc), lambda j, i: (0, j)),
            slab_in, slab_in, slab_in,
        ],
        out_specs=[pl.BlockSpec((tm, tc), lambda j, i: (i, j)), slab_out, slab_out, slab_out],
        out_shape=[jax.ShapeDtypeStruct((r, D_MODEL), BF16), cast_shape, cast_shape, cast_shape],
        scratch_shapes=[pltpu.VMEM((len(_CONV_GROUPS), tc, D_MODEL), BF16),
                        pltpu.VMEM((SUBLANES + tm, tc), F32)],
        compiler_params=pltpu.CompilerParams(
            dimension_semantics=("arbitrary", "arbitrary"), vmem_limit_bytes=VMEM_LIMIT),
        name="conv_proj",
    )(h, *([wt] * len(_CONV_GROUPS)), g_meta, conv_w, *out_weights)


def _attn_kernel(q_ref, k_ref, v_ref, z_ref, km_ref, vm_ref, g_ref, gm_ref, eqk_ref,
                 o_ref, qa_ref, kf_ref, va_ref, tri_ref, s_ref, tmax_ref, m_ref, acc_ref,
                 *, tq, tk, ts, rb):
    seq = q_ref.shape[0]
    nq = seq // tq
    assert tq == 2 * tk and seq % tq == 0
    half = tk // 2

    tri_ref[...] = jnp.where(
        lax.broadcasted_iota(jnp.int32, (tk, tk), 1) <= lax.broadcasted_iota(jnp.int32, (tk, tk), 0),
        0.0, NEG_BIG)
    kma = jnp.concatenate(
        [km_ref[...], _dot(gm_ref[...], eqk_ref[:, LANES:]).astype(BF16)], axis=1)
    vma = jnp.concatenate([vm_ref[...], jnp.ones((N_META, LANES), BF16)], axis=1)

    def setup(t, carry):
        rows = pl.ds(pl.multiple_of(t * ts, ts), ts)
        feat = _dot(g_ref[rows, :], eqk_ref[...]).astype(BF16)
        kf_ref[rows, :] = feat[:, LANES:]
        qa = jnp.concatenate([q_ref[rows, :], feat[:, :LANES]], axis=1)
        qa_ref[rows, :] = qa
        va_ref[rows, :] = jnp.concatenate([v_ref[rows, :], jnp.ones((ts, LANES), BF16)], axis=1)
        sm = _dot_nt(qa, kma)
        m0 = jnp.max(sm, axis=1, keepdims=True)
        m_ref[rows, :] = jnp.broadcast_to(m0, (ts, LANES))
        acc_ref[rows, :] = _dot(jnp.exp2(sm - m0).astype(BF16), vma)
        return carry

    lax.fori_loop(0, seq // ts, setup, 0, unroll=8)

    def stage_a(row0, nrows, key0, nkeys, tri, slot):
        keys = slice(key0, key0 + nkeys)
        ka = jnp.concatenate([k_ref[keys, :], kf_ref[keys, :]], axis=1)
        s = _dot_nt(qa_ref[row0:row0 + nrows, :], ka)
        for r0 in range(0, nrows, rb):
            part = s[r0:r0 + rb]
            if tri:
                part = part + tri_ref[r0:r0 + rb, 0:nkeys]
            s_ref[slot, r0:r0 + rb, 0:nkeys] = part
            t = part[:, 0:LANES]
            for c in range(1, nkeys // LANES):
                t = jnp.maximum(t, part[:, c * LANES:(c + 1) * LANES])
            tmax_ref[slot, r0:r0 + rb, :] = jnp.broadcast_to(
                jnp.max(t, axis=1, keepdims=True), (rb, LANES))

    def stage_b(row0, nrows, key0, nkeys, tri, slot):
        del tri
        alphas, ps = [], []
        for r0 in range(0, nrows, rb):
            rows = slice(row0 + r0, row0 + r0 + rb)
            m_old = m_ref[rows, :]
            m_new = jnp.maximum(m_old, tmax_ref[slot, r0:r0 + rb, :])
            m_ref[rows, :] = m_new
            alphas.append(jnp.exp2(m_old - m_new))
            ps.append(jnp.concatenate(
                [jnp.exp2(s_ref[slot, r0:r0 + rb, c * LANES:(c + 1) * LANES] - m_new).astype(BF16)
                 for c in range(nkeys // LANES)], axis=1))
        pv = _dot(jnp.concatenate(ps, axis=0), va_ref[key0:key0 + nkeys, :])
        for n, r0 in enumerate(range(0, nrows, rb)):
            rows = slice(row0 + r0, row0 + r0 + rb)
            alpha = jnp.concatenate([alphas[n], alphas[n]], axis=1)
            acc_ref[rows, :] = alpha * acc_ref[rows, :] + pv[r0:r0 + rb]

    def diagonal(row0):
        return [(row0, half, row0, half, True),
                (row0 + half, half, row0, half, False),
                (row0 + half, half, row0 + half, half, True)]

    items = []
    for i in range(nq):
        top = i * tq
        items += [(top, tq, j * tk, tk, False) for j in range(2 * i)]
        items += [(top + tk, tk, top, tk, False)]
        items += diagonal(top) + diagonal(top + tk)

    def finalize(row0, nrows):
        for r0 in range(row0, row0 + nrows, rb):
            o = (acc_ref[r0:r0 + rb, :LANES] / acc_ref[r0:r0 + rb, LANES:]
                 * _silu(z_ref[r0:r0 + rb, :].astype(F32)))
            o_ref[r0:r0 + rb, :] = o.astype(BF16)

    stage_a(*items[0], 0)
    for n, item in enumerate(items):
        if n + 1 < len(items):
            stage_a(*items[n + 1], 1 - n % 2)
        stage_b(*item, n % 2)
        if item[4]:
            finalize(item[0], item[1])


def _attn_call(proj, k_meta, v_meta, gate, gate_meta, eqk, batch, seq, tq=1024, tk=512, ts=512, rb=128):
    r = batch * seq
    head = lambda g: pl.BlockSpec((None, seq, HEAD_DIM), lambda b, h, g=g: (g, b, h))
    return pl.pallas_call(
        functools.partial(_attn_kernel, tq=tq, tk=tk, ts=ts, rb=rb),
        grid=(batch, HEADS),
        in_specs=[
            head(0), head(1), head(2), head(3),
            pl.BlockSpec((N_META, HEAD_DIM), lambda b, h: (0, h)),
            pl.BlockSpec((N_META, HEAD_DIM), lambda b, h: (0, h)),
            pl.BlockSpec((seq, LANES), lambda b, h: (b, 0)),
            pl.BlockSpec((N_META, LANES), lambda b, h: (0, 0)),
            pl.BlockSpec((None, LANES, 2 * LANES), lambda b, h: (h, 0, 0)),
        ],
        out_specs=pl.BlockSpec((seq, HEAD_DIM), lambda b, h: (b, h)),
        out_shape=jax.ShapeDtypeStruct((r, D_MODEL), BF16),
        scratch_shapes=[
            pltpu.VMEM((seq, 2 * LANES), BF16),
            pltpu.VMEM((seq, LANES), BF16),
            pltpu.VMEM((seq, 2 * LANES), BF16),
            pltpu.VMEM((tk, tk), F32),
            pltpu.VMEM((2, tq, tk), F32),
            pltpu.VMEM((2, tq, LANES), F32),
            pltpu.VMEM((seq, LANES), F32),
            pltpu.VMEM((seq, 2 * LANES), F32),
        ],
        compiler_params=pltpu.CompilerParams(
            dimension_semantics=("arbitrary", "arbitrary"), vmem_limit_bytes=VMEM_LIMIT),
        name="fox_attention",
    )(proj, proj, proj, proj, k_meta, v_meta, gate, gate_meta, eqk)


def _out_kernel(aa_ref, ac_ref, ma_ref, mc_ref, x_ref, wa_ref, wc_ref, wo_ref, gain_ref, o_ref):
    ya = _dot(aa_ref[...], wa_ref[...])
    yc = _dot(ac_ref[...], wc_ref[...])
    merged = (jax.nn.sigmoid(ma_ref[...].astype(F32)) * ya
              + jax.nn.sigmoid(mc_ref[...].astype(F32)) * yc)
    y = x_ref[...] + _dot(merged.astype(BF16), wo_ref[...])
    o_ref[...] = _rmsnorm(y, gain_ref[...])


def _out_call(a_att, a_conv, proj, x2d, wa, wc, wo, gain, tm=256):
    r = x2d.shape[0]
    row = lambda i: (i, 0)
    act = pl.BlockSpec((tm, D_MODEL), row)
    gate = lambda name: pl.BlockSpec((None, tm, D_MODEL),
                                     lambda i, g=_PROJ_GROUPS.index(name): (g, i, 0))
    weight = pl.BlockSpec((D_MODEL, D_MODEL), lambda i: (0, 0), pipeline_mode=pl.Buffered(1))
    return pl.pallas_call(
        _out_kernel,
        grid=(r // tm,),
        in_specs=[act, act, gate("m_att"), gate("m_conv"), act, weight, weight, weight,
                  pl.BlockSpec((1, D_MODEL), lambda i: (0, 0))],
        out_specs=act,
        out_shape=jax.ShapeDtypeStruct((r, D_MODEL), F32),
        compiler_params=pltpu.CompilerParams(
            dimension_semantics=("arbitrary",), vmem_limit_bytes=VMEM_LIMIT),
        name="out_proj",
    )(a_att, a_conv, proj, proj, x2d, wa, wc, wo, gain)


def kernel(x, meta_tokens, norm_gain, w_in, b_f, conv_w, w_att_o, w_conv_o, w_out, final_gain):
    batch, seq, _ = x.shape
    assert norm_gain.shape[0] == 1, "single-layer block"
    assert w_in.shape == (1, D_MODEL, N_IN) and w_in.dtype == F32
    wt = jnp.transpose(w_in[0])
    bf = jnp.pad(b_f[0].astype(F32), (0, LANES - HEADS)).reshape(1, LANES)
    gain = norm_gain[0].reshape(1, D_MODEL).astype(F32)
    fgain = final_gain.reshape(1, D_MODEL).astype(F32)
    place, ones, eqk = _feature_constants()

    k_meta, v_meta, g_meta, cm, gate_meta = _meta_call(
        meta_tokens.astype(F32), gain, wt, bf, place, ones)
    h, gate = _norm_call(x, gain, wt, bf, cm, place, ones)
    proj = _proj_call(h, wt)
    a_conv, wa, wc, wo = _conv_call(h, wt, g_meta, conv_w[0].astype(F32),
                                    (w_att_o, w_conv_o, w_out), seq)
    a_att = _attn_call(proj, k_meta, v_meta, gate, gate_meta, eqk, batch, seq)
    out = _out_call(a_att, a_conv, proj, x.reshape(batch * seq, D_MODEL), wa, wc, wo, fgain)
    return out.reshape(batch, seq, D_MODEL)
```

```python
import functools

import numpy as np

import jax
import jax.numpy as jnp
from jax import lax
from jax.experimental import pallas as pl
from jax.experimental.pallas import tpu as pltpu

D_MODEL = 2048
N_META = 16
HEADS = 16
HEAD_DIM = 128
CONV_K = 3
EPS = 1e-6
LANES = 128
SUBLANES = 8
NEG_BIG = -1e30
LOG2E = 1.4426950408889634
GATE_LANES = LANES // HEADS
VMEM_LIMIT = 56 * 1024 * 1024

_GROUPS = ("q", "k", "v", "f", "z_att", "u", "gate_b", "gate_c", "z_conv", "m_att", "m_conv")
_WIDTHS = (D_MODEL, D_MODEL, D_MODEL, HEADS) + (D_MODEL,) * 7
COL = {name: sum(_WIDTHS[:n]) for n, name in enumerate(_GROUPS)}
N_IN = sum(_WIDTHS)

F32 = jnp.float32
BF16 = jnp.bfloat16


def _dot(a, b):
    return jnp.dot(a, b, preferred_element_type=F32)


def _dot_nt(a, b):
    return lax.dot_general(a, b, (((1,), (1,)), ((), ())), preferred_element_type=F32)


def _rmsnorm(x, gain):
    return x * lax.rsqrt(jnp.mean(x * x, axis=-1, keepdims=True) + EPS) * gain


def _log_sigmoid(x):
    return jnp.minimum(x, 0.0) - jnp.log1p(jnp.exp(-jnp.abs(x)))


def _silu(x):
    return x * jax.nn.sigmoid(x)


def _split3(x):
    hi = x.astype(BF16)
    r1 = x - hi.astype(F32)
    mid = r1.astype(BF16)
    lo = (r1 - mid.astype(F32)).astype(BF16)
    return hi, mid, lo


def _cumsum_rows(x):
    n = x.shape[0]
    tri = (lax.broadcasted_iota(jnp.int32, (n, n), 0)
           >= lax.broadcasted_iota(jnp.int32, (n, n), 1)).astype(BF16)
    hi, mid, lo = _split3(x)
    return _dot(tri, hi) + _dot(tri, mid) + _dot(tri, lo)


def _gate_features(c, place_ref, ones_ref):
    hi, mid, lo = _split3(c * LOG2E)
    cat = jnp.concatenate([hi, mid, lo], axis=1)
    return (_dot(cat, place_ref[...]) + ones_ref[...]).astype(BF16)


def _feature_constants():
    place = np.zeros((3 * LANES, LANES), np.float32)
    ones = np.zeros((1, LANES), np.float32)
    eq = np.zeros((HEADS, LANES, LANES), np.float32)
    ek = np.zeros((HEADS, LANES, LANES), np.float32)
    for h in range(HEADS):
        base = GATE_LANES * h
        ones[0, base + 3] = 1.0
        for f in range(3):
            place[f * LANES + h, base + f] = 1.0
            eq[h, base + f, f] = 1.0
            eq[h, base + 3, 3 + f] = 1.0
            ek[h, base + 3, f] = 1.0
            ek[h, base + f, 3 + f] = -1.0
    return (jnp.asarray(place, BF16), jnp.asarray(ones, F32),
            jnp.asarray(np.concatenate([eq, ek], axis=2), BF16))


def _w_tile_spec(tn, row_of):
    assert all(c % HEADS == 0 for c in COL.values()) and tn % HEADS == 0
    return pl.BlockSpec((pl.Element(tn), pl.Element(D_MODEL)),
                        lambda *ids: (pl.multiple_of(row_of(*ids), HEADS), 0))


def _forget_w_spec():
    assert COL["f"] % LANES == 0
    return pl.BlockSpec((LANES, D_MODEL), lambda *ids: (COL["f"] // LANES, 0))


def _stage_weight(src_ref, dst_ref, chunk=128):
    for r in range(0, src_ref.shape[0], chunk):
        dst_ref[r:r + chunk, :] = src_ref[r:r + chunk, :].astype(BF16)


def _meta_kernel(meta_ref, gain_ref, wk_ref, wv_ref, wu_ref, wgc_ref, wf_ref, bf_ref,
                 place_ref, ones_ref, k_ref, v_ref, g_ref, cm_ref, gm_ref):
    h = _rmsnorm(meta_ref[...], gain_ref[...]).astype(BF16)
    proj = lambda w_ref: _dot_nt(h, w_ref[...].astype(BF16))
    k_ref[...] = proj(wk_ref).astype(BF16)
    v_ref[...] = proj(wv_ref).astype(BF16)
    g_ref[...] = proj(wgc_ref) * proj(wu_ref)
    cm = _cumsum_rows(_log_sigmoid(proj(wf_ref) + bf_ref[...]))
    cm_ref[...] = cm
    gm_ref[...] = _gate_features(cm, place_ref, ones_ref)


def _meta_call(meta, gain, wt, bf, place, ones, tn=512):
    nt = D_MODEL // tn
    const = lambda shape: pl.BlockSpec(shape, lambda j: (0,) * len(shape))
    group = lambda name: _w_tile_spec(tn, lambda j: COL[name] + j * tn)
    return pl.pallas_call(
        _meta_kernel,
        grid=(nt,),
        in_specs=[
            const((N_META, D_MODEL)), const((1, D_MODEL)),
            group("k"), group("v"), group("u"), group("gate_c"), _forget_w_spec(),
            const((1, LANES)), const((3 * LANES, LANES)), const((1, LANES)),
        ],
        out_specs=[
            pl.BlockSpec((N_META, tn), lambda j: (0, j)),
            pl.BlockSpec((N_META, tn), lambda j: (0, j)),
            pl.BlockSpec((N_META, tn), lambda j: (0, j)),
            const((N_META, LANES)),
            const((N_META, LANES)),
        ],
        out_shape=[
            jax.ShapeDtypeStruct((N_META, D_MODEL), BF16),
            jax.ShapeDtypeStruct((N_META, D_MODEL), BF16),
            jax.ShapeDtypeStruct((N_META, D_MODEL), F32),
            jax.ShapeDtypeStruct((N_META, LANES), F32),
            jax.ShapeDtypeStruct((N_META, LANES), BF16),
        ],
        compiler_params=pltpu.CompilerParams(
            dimension_semantics=("arbitrary",), vmem_limit_bytes=VMEM_LIMIT),
        name="meta_proj",
    )(meta, gain, wt, wt, wt, wt, wt, bf, place, ones)


def _norm_kernel(x_ref, gain_ref, wf_ref, bf_ref, cm_ref, place_ref, ones_ref,
                 h_ref, gate_ref, carry_ref, *, chunk):
    @pl.when(pl.program_id(1) == 0)
    def _():
        carry_ref[...] = cm_ref[N_META - 1:N_META, :]

    h = _rmsnorm(x_ref[...], gain_ref[...]).astype(BF16)
    h_ref[...] = h
    logf = _log_sigmoid(_dot_nt(h, wf_ref[...].astype(BF16)) + bf_ref[...])
    tm = logf.shape[0]
    run = carry_ref[...]
    parts = []
    for j in range(tm // chunk):
        c = _cumsum_rows(logf[j * chunk:(j + 1) * chunk]) + run
        run = c[chunk - 1:chunk, :]
        parts.append(c)
    carry_ref[...] = run
    gate_ref[...] = _gate_features(jnp.concatenate(parts, axis=0), place_ref, ones_ref)


def _norm_call(x, gain, wt, bf, cm, place, ones, tm=1024, chunk=256):
    b, s, _ = x.shape
    nt = s // tm
    const = lambda shape: pl.BlockSpec(shape, lambda bi, i: (0,) * len(shape))
    return pl.pallas_call(
        functools.partial(_norm_kernel, chunk=chunk),
        grid=(b, nt),
        in_specs=[
            pl.BlockSpec((None, tm, D_MODEL), lambda bi, i: (bi, i, 0)),
            const((1, D_MODEL)),
            _forget_w_spec(),
            const((1, LANES)),
            const((N_META, LANES)),
            const((3 * LANES, LANES)),
            const((1, LANES)),
        ],
        out_specs=[
            pl.BlockSpec((tm, D_MODEL), lambda bi, i: (bi * nt + i, 0)),
            pl.BlockSpec((tm, LANES), lambda bi, i: (bi * nt + i, 0)),
        ],
        out_shape=[
            jax.ShapeDtypeStruct((b * s, D_MODEL), BF16),
            jax.ShapeDtypeStruct((b * s, LANES), BF16),
        ],
        scratch_shapes=[pltpu.VMEM((1, LANES), F32)],
        compiler_params=pltpu.CompilerParams(
            dimension_semantics=("arbitrary", "arbitrary"), vmem_limit_bytes=VMEM_LIMIT),
        name="norm_forget",
    )(x, gain, wt, bf, cm, place, ones)


_PROJ_GROUPS = ("q", "k", "v", "z_att", "m_att", "m_conv")


def _proj_group_row(g):
    row = jnp.int32(COL[_PROJ_GROUPS[0]])
    for n in range(1, len(_PROJ_GROUPS)):
        row = jnp.where(g == n, COL[_PROJ_GROUPS[n]], row)
    return row


def _proj_kernel(h_ref, w_ref, o_ref, w_s, *, row_chunk):
    @pl.when(pl.program_id(2) == 0)
    def _():
        _stage_weight(w_ref, w_s)

    scale = jnp.where(pl.program_id(0) == 0, HEAD_DIM ** -0.5 * LOG2E, 1.0).astype(F32)
    for r in range(0, h_ref.shape[0], row_chunk):
        y = _dot_nt(h_ref[r:r + row_chunk, :], w_s[...])
        o_ref[r:r + row_chunk, :] = (y * scale).astype(BF16)


def _proj_call(h, wt, tm=2048, tn=1024, row_chunk=512):
    r = h.shape[0]
    ng = len(_PROJ_GROUPS)
    return pl.pallas_call(
        functools.partial(_proj_kernel, row_chunk=row_chunk),
        grid=(ng, D_MODEL // tn, r // tm),
        in_specs=[
            pl.BlockSpec((tm, D_MODEL), lambda g, j, i: (i, 0)),
            _w_tile_spec(tn, lambda g, j, i: _proj_group_row(g) + j * tn),
        ],
        out_specs=pl.BlockSpec((None, tm, tn), lambda g, j, i: (g, i, j)),
        out_shape=jax.ShapeDtypeStruct((ng, r, D_MODEL), BF16),
        scratch_shapes=[pltpu.VMEM((tn, D_MODEL), BF16)],
        compiler_params=pltpu.CompilerParams(
            dimension_semantics=("arbitrary", "arbitrary", "arbitrary"),
            vmem_limit_bytes=VMEM_LIMIT),
        name="plain_proj",
    )(h, wt)


_CONV_GROUPS = ("u", "gate_b", "gate_c", "z_conv")


def _conv_kernel(h_ref, wu_ref, wgb_ref, wgc_ref, wzc_ref, gm_ref, cw_ref,
                 wa_ref, wc_ref, wo_ref, o_ref, wa_out, wc_out, wo_out,
                 w_s, gbuf_ref, *, tiles_per_batch, row_chunk):
    i = pl.program_id(1)
    tm = h_ref.shape[0]
    for src, dst in ((wa_ref, wa_out), (wc_ref, wc_out), (wo_ref, wo_out)):
        dst[...] = src[...].astype(BF16)

    @pl.when(i == 0)
    def _():
        for n, w_ref in enumerate((wu_ref, wgb_ref, wgc_ref, wzc_ref)):
            _stage_weight(w_ref, w_s.at[n])

    @pl.when(i % tiles_per_batch == 0)
    def _():
        gbuf_ref[0:SUBLANES, :] = gm_ref[N_META - SUBLANES:N_META, :]

    for r in range(0, tm, row_chunk):
        h = h_ref[r:r + row_chunk, :]
        g = _dot_nt(h, w_s[2]) * _dot_nt(h, w_s[0])
        first = SUBLANES + r
        gbuf_ref[first:first + row_chunk, :] = g
        conv = (gbuf_ref[first - 2:first - 2 + row_chunk, :] * cw_ref[0:1, :]
                + gbuf_ref[first - 1:first - 1 + row_chunk, :] * cw_ref[1:2, :]
                + g * cw_ref[2:3, :])
        gb = _dot_nt(h, w_s[1])
        zc = _dot_nt(h, w_s[3])
        o_ref[r:r + row_chunk, :] = (gb * conv * _silu(zc)).astype(BF16)
    gbuf_ref[0:SUBLANES, :] = gbuf_ref[tm:tm + SUBLANES, :]


def _conv_call(h, wt, g_meta, conv_w, out_weights, rows_per_batch, tm=2048, tc=256, row_chunk=512):
    r = h.shape[0]
    nrow = r // tm
    steps = (D_MODEL // tc) * nrow
    slab = D_MODEL // steps
    assert slab * steps == D_MODEL and slab % (2 * SUBLANES) == 0
    w_specs = [_w_tile_spec(tc, lambda j, i, name=name: COL[name] + j * tc)
               for name in _CONV_GROUPS]
    slab_in = pl.BlockSpec((None, slab, D_MODEL), lambda j, i: (0, j * nrow + i, 0))
    slab_out = pl.BlockSpec((slab, D_MODEL), lambda j, i: (j * nrow + i, 0))
    cast_shape = jax.ShapeDtypeStruct((D_MODEL, D_MODEL), BF16)
    return pl.pallas_call(
        functools.partial(_conv_kernel, tiles_per_batch=rows_per_batch // tm, row_chunk=row_chunk),
        grid=(D_MODEL // tc, nrow),
        in_specs=[pl.BlockSpec((tm, D_MODEL), lambda j, i: (i, 0))] + w_specs + [
            pl.BlockSpec((N_META, tc), lambda j, i: (0, j)),
            pl.BlockSpec((CONV_K, tc), lambda j, i: (0, j)),
            slab_in, slab_in, slab_in,
        ],
        out_specs=[pl.BlockSpec((tm, tc), lambda j, i: (i, j)), slab_out, slab_out, slab_out],
        out_shape=[jax.ShapeDtypeStruct((r, D_MODEL), BF16), cast_shape, cast_shape, cast_shape],
        scratch_shapes=[pltpu.VMEM((len(_CONV_GROUPS), tc, D_MODEL), BF16),
                        pltpu.VMEM((SUBLANES + tm, tc), F32)],
        compiler_params=pltpu.CompilerParams(
            dimension_semantics=("arbitrary", "arbitrary"), vmem_limit_bytes=VMEM_LIMIT),
        name="conv_proj",
    )(h, *([wt] * len(_CONV_GROUPS)), g_meta, conv_w, *out_weights)


def _attn_kernel(q_ref, k_ref, v_ref, z_ref, km_ref, vm_ref, g_ref, gm_ref, eqk_ref,
                 o_ref, qa_ref, kf_ref, va_ref, tri_ref, s_ref, tmax_ref, m_ref, acc_ref,
                 *, tq, tk, ts, rb):
    seq = q_ref.shape[0]
    nq = seq // tq
    assert tq == 2 * tk and seq % tq == 0
    half = tk // 2

    tri_ref[...] = jnp.where(
        lax.broadcasted_iota(jnp.int32, (tk, tk), 1) <= lax.broadcasted_iota(jnp.int32, (tk, tk), 0),
        0.0, NEG_BIG)
    kma = jnp.concatenate(
        [km_ref[...], _dot(gm_ref[...], eqk_ref[:, LANES:]).astype(BF16)], axis=1)
    pad_lane = lax.broadcasted_iota(jnp.int32, (LANES - N_META, 2 * LANES), 1)
    k_meta_blk = jnp.concatenate(
        [kma, jnp.where(pad_lane == LANES + 3, NEG_BIG, 0.0).astype(BF16)], axis=0)
    v_meta_blk = jnp.concatenate(
        [vm_ref[...], jnp.ones((N_META, LANES), BF16)], axis=1)
    v_meta_blk = jnp.concatenate(
        [v_meta_blk, jnp.zeros((LANES - N_META, 2 * LANES), BF16)], axis=0)

    def setup(row0):
        rows = slice(row0, row0 + ts)
        feat = _dot(g_ref[rows, :], eqk_ref[...]).astype(BF16)
        kf_ref[rows, :] = feat[:, LANES:]
        qa_ref[rows, :] = jnp.concatenate([q_ref[rows, :], feat[:, :LANES]], axis=1)
        va_ref[rows, :] = jnp.concatenate([v_ref[rows, :], jnp.ones((ts, LANES), BF16)], axis=1)
        m_ref[rows, :] = jnp.full((ts, LANES), NEG_BIG, F32)
        acc_ref[rows, :] = jnp.zeros((ts, 2 * LANES), F32)

    def width(key0, nkeys):
        return nkeys + LANES if key0 == 0 else nkeys

    def stage_a(row0, nrows, key0, nkeys, tri, slot):
        keys = slice(key0, key0 + nkeys)
        ka = jnp.concatenate([k_ref[keys, :], kf_ref[keys, :]], axis=1)
        if key0 == 0:
            ka = jnp.concatenate([ka, k_meta_blk], axis=0)
        s = _dot_nt(qa_ref[row0:row0 + nrows, :], ka)
        w = width(key0, nkeys)
        for r0 in range(0, nrows, rb):
            part = s[r0:r0 + rb]
            if tri:
                masked = part[:, 0:nkeys] + tri_ref[r0:r0 + rb, 0:nkeys]
                part = jnp.concatenate([masked, part[:, nkeys:]], axis=1) if w > nkeys else masked
            s_ref[slot, r0:r0 + rb, 0:w] = part
            t = part[:, 0:LANES]
            for c in range(1, w // LANES):
                t = jnp.maximum(t, part[:, c * LANES:(c + 1) * LANES])
            tmax_ref[slot, r0:r0 + rb, :] = jnp.broadcast_to(
                jnp.max(t, axis=1, keepdims=True), (rb, LANES))

    def stage_b(row0, nrows, key0, nkeys, tri, slot):
        del tri
        w = width(key0, nkeys)
        alphas, ps = [], []
        for r0 in range(0, nrows, rb):
            rows = slice(row0 + r0, row0 + r0 + rb)
            m_old = m_ref[rows, :]
            m_new = jnp.maximum(m_old, tmax_ref[slot, r0:r0 + rb, :])
            m_ref[rows, :] = m_new
            alphas.append(jnp.exp2(m_old - m_new))
            ps.append(jnp.concatenate(
                [jnp.exp2(s_ref[slot, r0:r0 + rb, c * LANES:(c + 1) * LANES] - m_new).astype(BF16)
                 for c in range(w // LANES)], axis=1))
        va = va_ref[key0:key0 + nkeys, :]
        if key0 == 0:
            va = jnp.concatenate([va, v_meta_blk], axis=0)
        pv = _dot(jnp.concatenate(ps, axis=0), va)
        for n, r0 in enumerate(range(0, nrows, rb)):
            rows = slice(row0 + r0, row0 + r0 + rb)
            alpha = jnp.concatenate([alphas[n], alphas[n]], axis=1)
            acc_ref[rows, :] = alpha * acc_ref[rows, :] + pv[r0:r0 + rb]

    def diagonal(row0):
        return [(row0, half, row0, half, True),
                (row0 + half, half, row0, half, False),
                (row0 + half, half, row0 + half, half, True)]

    items = []
    for i in range(nq):
        top = i * tq
        items += [(top, tq, j * tk, tk, False) for j in range(2 * i)]
        items += [(top + tk, tk, top, tk, False)]
        items += diagonal(top) + diagonal(top + tk)

    def finalize(row0, nrows):
        for r0 in range(row0, row0 + nrows, rb):
            o = (acc_ref[r0:r0 + rb, :LANES] / acc_ref[r0:r0 + rb, LANES:]
                 * _silu(z_ref[r0:r0 + rb, :].astype(F32)))
            o_ref[r0:r0 + rb, :] = o.astype(BF16)

    rows_ready = [0]

    def scores(item, slot):
        row0, nrows, key0, nkeys, _ = item
        while rows_ready[0] < max(row0 + nrows, key0 + nkeys):
            setup(rows_ready[0])
            rows_ready[0] += ts
        stage_a(*item, slot)

    scores(items[0], 0)
    for n, item in enumerate(items):
        if n + 1 < len(items):
            scores(items[n + 1], 1 - n % 2)
        stage_b(*item, n % 2)
        if item[4]:
            finalize(item[0], item[1])


def _attn_call(proj, k_meta, v_meta, gate, gate_meta, eqk, batch, seq, tq=1024, tk=512, ts=512, rb=128):
    r = batch * seq
    head = lambda g: pl.BlockSpec((None, seq, HEAD_DIM), lambda b, h, g=g: (g, b, h))
    return pl.pallas_call(
        functools.partial(_attn_kernel, tq=tq, tk=tk, ts=ts, rb=rb),
        grid=(batch, HEADS),
        in_specs=[
            head(0), head(1), head(2), head(3),
            pl.BlockSpec((N_META, HEAD_DIM), lambda b, h: (0, h)),
            pl.BlockSpec((N_META, HEAD_DIM), lambda b, h: (0, h)),
            pl.BlockSpec((seq, LANES), lambda b, h: (b, 0)),
            pl.BlockSpec((N_META, LANES), lambda b, h: (0, 0)),
            pl.BlockSpec((None, LANES, 2 * LANES), lambda b, h: (h, 0, 0)),
        ],
        out_specs=pl.BlockSpec((seq, HEAD_DIM), lambda b, h: (b, h)),
        out_shape=jax.ShapeDtypeStruct((r, D_MODEL), BF16),
        scratch_shapes=[
            pltpu.VMEM((seq, 2 * LANES), BF16),
            pltpu.VMEM((seq, LANES), BF16),
            pltpu.VMEM((seq, 2 * LANES), BF16),
            pltpu.VMEM((tk, tk), F32),
            pltpu.VMEM((2, tq, tk + LANES), F32),
            pltpu.VMEM((2, tq, LANES), F32),
            pltpu.VMEM((seq, LANES), F32),
            pltpu.VMEM((seq, 2 * LANES), F32),
        ],
        compiler_params=pltpu.CompilerParams(
            dimension_semantics=("arbitrary", "arbitrary"), vmem_limit_bytes=VMEM_LIMIT),
        name="fox_attention",
    )(proj, proj, proj, proj, k_meta, v_meta, gate, gate_meta, eqk)


def _out_kernel(aa_ref, ac_ref, ma_ref, mc_ref, x_ref, wa_ref, wc_ref, wo_ref, gain_ref, o_ref):
    ya = _dot(aa_ref[...], wa_ref[...])
    yc = _dot(ac_ref[...], wc_ref[...])
    merged = (jax.nn.sigmoid(ma_ref[...].astype(F32)) * ya
              + jax.nn.sigmoid(mc_ref[...].astype(F32)) * yc)
    y = x_ref[...] + _dot(merged.astype(BF16), wo_ref[...])
    o_ref[...] = _rmsnorm(y, gain_ref[...])


def _out_call(a_att, a_conv, proj, x2d, wa, wc, wo, gain, tm=256):
    r = x2d.shape[0]
    row = lambda i: (i, 0)
    act = pl.BlockSpec((tm, D_MODEL), row)
    gate = lambda name: pl.BlockSpec((None, tm, D_MODEL),
                                     lambda i, g=_PROJ_GROUPS.index(name): (g, i, 0))
    weight = pl.BlockSpec((D_MODEL, D_MODEL), lambda i: (0, 0), pipeline_mode=pl.Buffered(1))
    return pl.pallas_call(
        _out_kernel,
        grid=(r // tm,),
        in_specs=[act, act, gate("m_att"), gate("m_conv"), act, weight, weight, weight,
                  pl.BlockSpec((1, D_MODEL), lambda i: (0, 0))],
        out_specs=act,
        out_shape=jax.ShapeDtypeStruct((r, D_MODEL), F32),
        compiler_params=pltpu.CompilerParams(
            dimension_semantics=("arbitrary",), vmem_limit_bytes=VMEM_LIMIT),
        name="out_proj",
    )(a_att, a_conv, proj, proj, x2d, wa, wc, wo, gain)


def kernel(x, meta_tokens, norm_gain, w_in, b_f, conv_w, w_att_o, w_conv_o, w_out, final_gain):
    batch, seq, _ = x.shape
    assert norm_gain.shape[0] == 1, "single-layer block"
    assert w_in.shape == (1, D_MODEL, N_IN) and w_in.dtype == F32
    wt = jnp.transpose(w_in[0])
    bf = jnp.pad(b_f[0].astype(F32), (0, LANES - HEADS)).reshape(1, LANES)
    gain = norm_gain[0].reshape(1, D_MODEL).astype(F32)
    fgain = final_gain.reshape(1, D_MODEL).astype(F32)
    place, ones, eqk = _feature_constants()

    k_meta, v_meta, g_meta, cm, gate_meta = _meta_call(
        meta_tokens.astype(F32), gain, wt, bf, place, ones)
    h, gate = _norm_call(x, gain, wt, bf, cm, place, ones)
    proj = _proj_call(h, wt)
    a_conv, wa, wc, wo = _conv_call(h, wt, g_meta, conv_w[0].astype(F32),
                                    (w_att_o, w_conv_o, w_out), seq)
    a_att = _attn_call(proj, k_meta, v_meta, gate, gate_meta, eqk, batch, seq)
    out = _out_call(a_att, a_conv, proj, x.reshape(batch * seq, D_MODEL), wa, wc, wo, fgain)
    return out.reshape(batch, seq, D_MODEL)
```

```python
import functools
from typing import NamedTuple

import numpy as np

import jax
import jax.numpy as jnp
from jax import lax
from jax.experimental import pallas as pl
from jax.experimental.pallas import tpu as pltpu

D_MODEL = 2048
N_META = 16
HEADS = 16
HEAD_DIM = 128
CONV_K = 3
EPS = 1e-6
LANES = 128
SUBLANES = 8
NEG_BIG = -1e30
LOG2E = 1.4426950408889634
GATE_LANES = LANES // HEADS
SPLIT_TERMS = 3
VMEM_LIMIT = 56 * 1024 * 1024


class Tiles(NamedTuple):
    meta_tn: int = 512
    norm_tm: int = 1024
    norm_chunk: int = 256
    proj_tm: int = 2048
    proj_tn: int = 1024
    proj_rows: int = 512
    conv_tm: int = 2048
    conv_tc: int = 256
    conv_rows: int = 512
    attn_tq: int = 1024
    attn_tk: int = 512
    attn_ts: int = 512
    attn_rb: int = 128
    out_tm: int = 256


TILES = Tiles()

_GROUPS = ("q", "k", "v", "f", "z_att", "u", "gate_b", "gate_c", "z_conv", "m_att", "m_conv")
_WIDTHS = (D_MODEL, D_MODEL, D_MODEL, HEADS) + (D_MODEL,) * 7
COL = {name: sum(_WIDTHS[:n]) for n, name in enumerate(_GROUPS)}
N_IN = sum(_WIDTHS)

F32 = jnp.float32
BF16 = jnp.bfloat16


def _dot(a, b):
    return jnp.dot(a, b, preferred_element_type=F32)


def _dot_nt(a, b):
    return lax.dot_general(a, b, (((1,), (1,)), ((), ())), preferred_element_type=F32)


def _rmsnorm(x, gain):
    return x * lax.rsqrt(jnp.mean(x * x, axis=-1, keepdims=True) + EPS) * gain


def _log_sigmoid(x):
    return jnp.minimum(x, 0.0) - jnp.log1p(jnp.exp(-jnp.abs(x)))


def _silu(x):
    return x * jax.nn.sigmoid(x)


def _split3(x):
    hi = x.astype(BF16)
    r1 = x - hi.astype(F32)
    mid = r1.astype(BF16)
    lo = (r1 - mid.astype(F32)).astype(BF16)
    return hi, mid, lo


def _cumsum_rows(x):
    n = x.shape[0]
    tri = (lax.broadcasted_iota(jnp.int32, (n, n), 0)
           >= lax.broadcasted_iota(jnp.int32, (n, n), 1)).astype(BF16)
    hi, mid, lo = _split3(x)
    return _dot(tri, hi) + _dot(tri, mid) + _dot(tri, lo)


def _gate_features(c, place_ref, ones_ref):
    hi, mid, lo = _split3(c * LOG2E)
    cat = jnp.concatenate([hi, mid, lo], axis=1)
    return (_dot(cat, place_ref[...]) + ones_ref[...]).astype(BF16)


def _feature_constants():
    n = SPLIT_TERMS
    assert n + 1 <= GATE_LANES
    place = np.zeros((n * LANES, LANES), np.float32)
    ones = np.zeros((1, LANES), np.float32)
    eq = np.zeros((HEADS, LANES, LANES), np.float32)
    ek = np.zeros((HEADS, LANES, LANES), np.float32)
    for h in range(HEADS):
        base = GATE_LANES * h
        ones[0, base + n] = 1.0
        for f in range(n):
            place[f * LANES + h, base + f] = 1.0
            eq[h, base + f, f] = 1.0
            eq[h, base + n, n + f] = 1.0
            ek[h, base + n, f] = 1.0
            ek[h, base + f, n + f] = -1.0
    return (jnp.asarray(place, BF16), jnp.asarray(ones, F32),
            jnp.asarray(np.concatenate([eq, ek], axis=2), BF16))


def _w_tile_spec(tn, row_of):
    assert all(c % HEADS == 0 for c in COL.values()) and tn % HEADS == 0
    return pl.BlockSpec((pl.Element(tn), pl.Element(D_MODEL)),
                        lambda *ids: (pl.multiple_of(row_of(*ids), HEADS), 0))


def _forget_w_spec():
    assert COL["f"] % LANES == 0
    return pl.BlockSpec((LANES, D_MODEL), lambda *ids: (COL["f"] // LANES, 0))


def _stage_weight(src_ref, dst_ref, chunk=128):
    for r in range(0, src_ref.shape[0], chunk):
        dst_ref[r:r + chunk, :] = src_ref[r:r + chunk, :].astype(BF16)


def _meta_kernel(meta_ref, gain_ref, wk_ref, wv_ref, wu_ref, wgc_ref, wf_ref, bf_ref,
                 place_ref, ones_ref, k_ref, v_ref, g_ref, cm_ref, gm_ref):
    h = _rmsnorm(meta_ref[...], gain_ref[...]).astype(BF16)
    proj = lambda w_ref: _dot_nt(h, w_ref[...].astype(BF16))
    k_ref[...] = proj(wk_ref).astype(BF16)
    v_ref[...] = proj(wv_ref).astype(BF16)
    g_ref[...] = proj(wgc_ref) * proj(wu_ref)
    cm = _cumsum_rows(_log_sigmoid(proj(wf_ref) + bf_ref[...]))
    cm_ref[...] = cm
    gm_ref[...] = _gate_features(cm, place_ref, ones_ref)


def _meta_call(meta, gain, wt, bf, place, ones):
    tn = TILES.meta_tn
    nt = D_MODEL // tn
    const = lambda shape: pl.BlockSpec(shape, lambda j: (0,) * len(shape))
    group = lambda name: _w_tile_spec(tn, lambda j: COL[name] + j * tn)
    return pl.pallas_call(
        _meta_kernel,
        grid=(nt,),
        in_specs=[
            const((N_META, D_MODEL)), const((1, D_MODEL)),
            group("k"), group("v"), group("u"), group("gate_c"), _forget_w_spec(),
            const((1, LANES)), const((SPLIT_TERMS * LANES, LANES)), const((1, LANES)),
        ],
        out_specs=[
            pl.BlockSpec((N_META, tn), lambda j: (0, j)),
            pl.BlockSpec((N_META, tn), lambda j: (0, j)),
            pl.BlockSpec((N_META, tn), lambda j: (0, j)),
            const((N_META, LANES)),
            const((N_META, LANES)),
        ],
        out_shape=[
            jax.ShapeDtypeStruct((N_META, D_MODEL), BF16),
            jax.ShapeDtypeStruct((N_META, D_MODEL), BF16),
            jax.ShapeDtypeStruct((N_META, D_MODEL), F32),
            jax.ShapeDtypeStruct((N_META, LANES), F32),
            jax.ShapeDtypeStruct((N_META, LANES), BF16),
        ],
        compiler_params=pltpu.CompilerParams(
            dimension_semantics=("arbitrary",), vmem_limit_bytes=VMEM_LIMIT),
        name="meta_proj",
    )(meta, gain, wt, wt, wt, wt, wt, bf, place, ones)


def _norm_kernel(x_ref, gain_ref, wf_ref, bf_ref, cm_ref, place_ref, ones_ref,
                 h_ref, gate_ref, carry_ref, *, chunk):
    @pl.when(pl.program_id(1) == 0)
    def _():
        carry_ref[...] = cm_ref[N_META - 1:N_META, :]

    h = _rmsnorm(x_ref[...], gain_ref[...]).astype(BF16)
    h_ref[...] = h
    logf = _log_sigmoid(_dot_nt(h, wf_ref[...].astype(BF16)) + bf_ref[...])
    tm = logf.shape[0]
    run = carry_ref[...]
    parts = []
    for j in range(tm // chunk):
        c = _cumsum_rows(logf[j * chunk:(j + 1) * chunk]) + run
        run = c[chunk - 1:chunk, :]
        parts.append(c)
    carry_ref[...] = run
    gate_ref[...] = _gate_features(jnp.concatenate(parts, axis=0), place_ref, ones_ref)


def _norm_call(x, gain, wt, bf, cm, place, ones):
    tm, chunk = TILES.norm_tm, TILES.norm_chunk
    b, s, _ = x.shape
    nt = s // tm
    const = lambda shape: pl.BlockSpec(shape, lambda bi, i: (0,) * len(shape))
    return pl.pallas_call(
        functools.partial(_norm_kernel, chunk=chunk),
        grid=(b, nt),
        in_specs=[
            pl.BlockSpec((None, tm, D_MODEL), lambda bi, i: (bi, i, 0)),
            const((1, D_MODEL)),
            _forget_w_spec(),
            const((1, LANES)),
            const((N_META, LANES)),
            const((SPLIT_TERMS * LANES, LANES)),
            const((1, LANES)),
        ],
        out_specs=[
            pl.BlockSpec((tm, D_MODEL), lambda bi, i: (bi * nt + i, 0)),
            pl.BlockSpec((tm, LANES), lambda bi, i: (bi * nt + i, 0)),
        ],
        out_shape=[
            jax.ShapeDtypeStruct((b * s, D_MODEL), BF16),
            jax.ShapeDtypeStruct((b * s, LANES), BF16),
        ],
        scratch_shapes=[pltpu.VMEM((1, LANES), F32)],
        compiler_params=pltpu.CompilerParams(
            dimension_semantics=("arbitrary", "arbitrary"), vmem_limit_bytes=VMEM_LIMIT),
        name="norm_forget",
    )(x, gain, wt, bf, cm, place, ones)


_PROJ_GROUPS = ("q", "k", "v", "z_att", "m_att", "m_conv")


def _proj_group_row(g):
    row = jnp.int32(COL[_PROJ_GROUPS[0]])
    for n in range(1, len(_PROJ_GROUPS)):
        row = jnp.where(g == n, COL[_PROJ_GROUPS[n]], row)
    return row


def _proj_kernel(h_ref, w_ref, o_ref, w_s, *, row_chunk):
    @pl.when(pl.program_id(2) == 0)
    def _():
        _stage_weight(w_ref, w_s)

    scale = jnp.where(pl.program_id(0) == 0, HEAD_DIM ** -0.5 * LOG2E, 1.0).astype(F32)
    for r in range(0, h_ref.shape[0], row_chunk):
        y = _dot_nt(h_ref[r:r + row_chunk, :], w_s[...])
        o_ref[r:r + row_chunk, :] = (y * scale).astype(BF16)


def _proj_call(h, wt):
    tm, tn, row_chunk = TILES.proj_tm, TILES.proj_tn, TILES.proj_rows
    r = h.shape[0]
    ng = len(_PROJ_GROUPS)
    return pl.pallas_call(
        functools.partial(_proj_kernel, row_chunk=row_chunk),
        grid=(ng, D_MODEL // tn, r // tm),
        in_specs=[
            pl.BlockSpec((tm, D_MODEL), lambda g, j, i: (i, 0)),
            _w_tile_spec(tn, lambda g, j, i: _proj_group_row(g) + j * tn),
        ],
        out_specs=pl.BlockSpec((None, tm, tn), lambda g, j, i: (g, i, j)),
        out_shape=jax.ShapeDtypeStruct((ng, r, D_MODEL), BF16),
        scratch_shapes=[pltpu.VMEM((tn, D_MODEL), BF16)],
        compiler_params=pltpu.CompilerParams(
            dimension_semantics=("arbitrary", "arbitrary", "arbitrary"),
            vmem_limit_bytes=VMEM_LIMIT),
        name="plain_proj",
    )(h, wt)


_CONV_GROUPS = ("u", "gate_b", "gate_c", "z_conv")


def _conv_kernel(h_ref, wu_ref, wgb_ref, wgc_ref, wzc_ref, gm_ref, cw_ref,
                 wa_ref, wc_ref, wo_ref, o_ref, wa_out, wc_out, wo_out,
                 w_s, gbuf_ref, *, tiles_per_batch, row_chunk):
    i = pl.program_id(1)
    tm = h_ref.shape[0]
    for src, dst in ((wa_ref, wa_out), (wc_ref, wc_out), (wo_ref, wo_out)):
        dst[...] = src[...].astype(BF16)

    @pl.when(i == 0)
    def _():
        for n, w_ref in enumerate((wu_ref, wgb_ref, wgc_ref, wzc_ref)):
            _stage_weight(w_ref, w_s.at[n])

    @pl.when(i % tiles_per_batch == 0)
    def _():
        gbuf_ref[0:SUBLANES, :] = gm_ref[N_META - SUBLANES:N_META, :]

    for r in range(0, tm, row_chunk):
        h = h_ref[r:r + row_chunk, :]
        g = _dot_nt(h, w_s[2]) * _dot_nt(h, w_s[0])
        first = SUBLANES + r
        gbuf_ref[first:first + row_chunk, :] = g
        conv = (gbuf_ref[first - 2:first - 2 + row_chunk, :] * cw_ref[0:1, :]
                + gbuf_ref[first - 1:first - 1 + row_chunk, :] * cw_ref[1:2, :]
                + g * cw_ref[2:3, :])
        gb = _dot_nt(h, w_s[1])
        zc = _dot_nt(h, w_s[3])
        o_ref[r:r + row_chunk, :] = (gb * conv * _silu(zc)).astype(BF16)
    gbuf_ref[0:SUBLANES, :] = gbuf_ref[tm:tm + SUBLANES, :]


def _conv_call(h, wt, g_meta, conv_w, out_weights, rows_per_batch):
    tm, tc, row_chunk = TILES.conv_tm, TILES.conv_tc, TILES.conv_rows
    r = h.shape[0]
    nrow = r // tm
    steps = (D_MODEL // tc) * nrow
    slab = D_MODEL // steps
    assert slab * steps == D_MODEL and slab % (2 * SUBLANES) == 0
    w_specs = [_w_tile_spec(tc, lambda j, i, name=name: COL[name] + j * tc)
               for name in _CONV_GROUPS]
    slab_in = pl.BlockSpec((None, slab, D_MODEL), lambda j, i: (0, j * nrow + i, 0))
    slab_out = pl.BlockSpec((slab, D_MODEL), lambda j, i: (j * nrow + i, 0))
    cast_shape = jax.ShapeDtypeStruct((D_MODEL, D_MODEL), BF16)
    return pl.pallas_call(
        functools.partial(_conv_kernel, tiles_per_batch=rows_per_batch // tm, row_chunk=row_chunk),
        grid=(D_MODEL // tc, nrow),
        in_specs=[pl.BlockSpec((tm, D_MODEL), lambda j, i: (i, 0))] + w_specs + [
            pl.BlockSpec((N_META, tc), lambda j, i: (0, j)),
            pl.BlockSpec((CONV_K, tc), lambda j, i: (0, j)),
            slab_in, slab_in, slab_in,
        ],
        out_specs=[pl.BlockSpec((tm, tc), lambda j, i: (i, j)), slab_out, slab_out, slab_out],
        out_shape=[jax.ShapeDtypeStruct((r, D_MODEL), BF16), cast_shape, cast_shape, cast_shape],
        scratch_shapes=[pltpu.VMEM((len(_CONV_GROUPS), tc, D_MODEL), BF16),
                        pltpu.VMEM((SUBLANES + tm, tc), F32)],
        compiler_params=pltpu.CompilerParams(
            dimension_semantics=("arbitrary", "arbitrary"), vmem_limit_bytes=VMEM_LIMIT),
        name="conv_proj",
    )(h, *([wt] * len(_CONV_GROUPS)), g_meta, conv_w, *out_weights)


def _attn_kernel(q_ref, k_ref, v_ref, z_ref, km_ref, vm_ref, g_ref, gm_ref, eqk_ref,
                 o_ref, qa_ref, kf_ref, va_ref, tri_ref, s_ref, tmax_ref, m_ref, acc_ref,
                 *, tq, tk, ts, rb):
    seq = q_ref.shape[0]
    nq = seq // tq
    assert tq == 2 * tk and seq % tq == 0
    half = tk // 2

    tri_ref[...] = jnp.where(
        lax.broadcasted_iota(jnp.int32, (tk, tk), 1) <= lax.broadcasted_iota(jnp.int32, (tk, tk), 0),
        0.0, NEG_BIG)
    kma = jnp.concatenate(
        [km_ref[...], _dot(gm_ref[...], eqk_ref[:, LANES:]).astype(BF16)], axis=1)
    pad_lane = lax.broadcasted_iota(jnp.int32, (LANES - N_META, 2 * LANES), 1)
    k_meta_blk = jnp.concatenate(
        [kma, jnp.where(pad_lane == LANES + SPLIT_TERMS, NEG_BIG, 0.0).astype(BF16)],
        axis=0)
    v_meta_blk = jnp.concatenate(
        [vm_ref[...], jnp.ones((N_META, LANES), BF16)], axis=1)
    v_meta_blk = jnp.concatenate(
        [v_meta_blk, jnp.zeros((LANES - N_META, 2 * LANES), BF16)], axis=0)

    def setup(row0):
        rows = slice(row0, row0 + ts)
        feat = _dot(g_ref[rows, :], eqk_ref[...]).astype(BF16)
        kf_ref[rows, :] = feat[:, LANES:]
        qa_ref[rows, :] = jnp.concatenate([q_ref[rows, :], feat[:, :LANES]], axis=1)
        va_ref[rows, :] = jnp.concatenate([v_ref[rows, :], jnp.ones((ts, LANES), BF16)], axis=1)
        m_ref[rows, :] = jnp.full((ts, LANES), NEG_BIG, F32)
        acc_ref[rows, :] = jnp.zeros((ts, 2 * LANES), F32)

    def width(key0, nkeys):
        return nkeys + LANES if key0 == 0 else nkeys

    def stage_a(row0, nrows, key0, nkeys, tri, slot):
        keys = slice(key0, key0 + nkeys)
        ka = jnp.concatenate([k_ref[keys, :], kf_ref[keys, :]], axis=1)
        if key0 == 0:
            ka = jnp.concatenate([ka, k_meta_blk], axis=0)
        s = _dot_nt(qa_ref[row0:row0 + nrows, :], ka)
        w = width(key0, nkeys)
        for r0 in range(0, nrows, rb):
            part = s[r0:r0 + rb]
            if tri:
                masked = part[:, 0:nkeys] + tri_ref[r0:r0 + rb, 0:nkeys]
                part = jnp.concatenate([masked, part[:, nkeys:]], axis=1) if w > nkeys else masked
            s_ref[slot, r0:r0 + rb, 0:w] = part
            t = part[:, 0:LANES]
            for c in range(1, w // LANES):
                t = jnp.maximum(t, part[:, c * LANES:(c + 1) * LANES])
            tmax_ref[slot, r0:r0 + rb, :] = jnp.broadcast_to(
                jnp.max(t, axis=1, keepdims=True), (rb, LANES))

    def stage_b(row0, nrows, key0, nkeys, tri, slot):
        del tri
        w = width(key0, nkeys)
        alphas, ps = [], []
        for r0 in range(0, nrows, rb):
            rows = slice(row0 + r0, row0 + r0 + rb)
            m_old = m_ref[rows, :]
            m_new = jnp.maximum(m_old, tmax_ref[slot, r0:r0 + rb, :])
            m_ref[rows, :] = m_new
            alphas.append(jnp.exp2(m_old - m_new))
            ps.append(jnp.concatenate(
                [jnp.exp2(s_ref[slot, r0:r0 + rb, c * LANES:(c + 1) * LANES] - m_new).astype(BF16)
                 for c in range(w // LANES)], axis=1))
        va = va_ref[key0:key0 + nkeys, :]
        if key0 == 0:
            va = jnp.concatenate([va, v_meta_blk], axis=0)
        pv = _dot(jnp.concatenate(ps, axis=0), va)
        for n, r0 in enumerate(range(0, nrows, rb)):
            rows = slice(row0 + r0, row0 + r0 + rb)
            alpha = jnp.concatenate([alphas[n], alphas[n]], axis=1)
            acc_ref[rows, :] = alpha * acc_ref[rows, :] + pv[r0:r0 + rb]

    def diagonal(row0):
        return [(row0, half, row0, half, True),
                (row0 + half, half, row0, half, False),
                (row0 + half, half, row0 + half, half, True)]

    items = []
    for i in range(nq):
        top = i * tq
        items += [(top, tq, j * tk, tk, False) for j in range(2 * i)]
        items += [(top + tk, tk, top, tk, False)]
        items += diagonal(top) + diagonal(top + tk)

    def finalize(row0, nrows):
        for r0 in range(row0, row0 + nrows, rb):
            o = (acc_ref[r0:r0 + rb, :LANES] / acc_ref[r0:r0 + rb, LANES:]
                 * _silu(z_ref[r0:r0 + rb, :].astype(F32)))
            o_ref[r0:r0 + rb, :] = o.astype(BF16)

    rows_ready = [0]

    def scores(item, slot):
        row0, nrows, key0, nkeys, _ = item
        while rows_ready[0] < max(row0 + nrows, key0 + nkeys):
            setup(rows_ready[0])
            rows_ready[0] += ts
        stage_a(*item, slot)

    scores(items[0], 0)
    for n, item in enumerate(items):
        if n + 1 < len(items):
            scores(items[n + 1], 1 - n % 2)
        stage_b(*item, n % 2)
        if item[4]:
            finalize(item[0], item[1])


def _attn_call(proj, k_meta, v_meta, gate, gate_meta, eqk, batch, seq):
    tq, tk, ts, rb = TILES.attn_tq, TILES.attn_tk, TILES.attn_ts, TILES.attn_rb
    r = batch * seq
    head = lambda g: pl.BlockSpec((None, seq, HEAD_DIM), lambda b, h, g=g: (g, b, h))
    return pl.pallas_call(
        functools.partial(_attn_kernel, tq=tq, tk=tk, ts=ts, rb=rb),
        grid=(batch, HEADS),
        in_specs=[
            head(0), head(1), head(2), head(3),
            pl.BlockSpec((N_META, HEAD_DIM), lambda b, h: (0, h)),
            pl.BlockSpec((N_META, HEAD_DIM), lambda b, h: (0, h)),
            pl.BlockSpec((seq, LANES), lambda b, h: (b, 0)),
            pl.BlockSpec((N_META, LANES), lambda b, h: (0, 0)),
            pl.BlockSpec((None, LANES, 2 * LANES), lambda b, h: (h, 0, 0)),
        ],
        out_specs=pl.BlockSpec((seq, HEAD_DIM), lambda b, h: (b, h)),
        out_shape=jax.ShapeDtypeStruct((r, D_MODEL), BF16),
        scratch_shapes=[
            pltpu.VMEM((seq, 2 * LANES), BF16),
            pltpu.VMEM((seq, LANES), BF16),
            pltpu.VMEM((seq, 2 * LANES), BF16),
            pltpu.VMEM((tk, tk), F32),
            pltpu.VMEM((2, tq, tk + LANES), F32),
            pltpu.VMEM((2, tq, LANES), F32),
            pltpu.VMEM((seq, LANES), F32),
            pltpu.VMEM((seq, 2 * LANES), F32),
        ],
        compiler_params=pltpu.CompilerParams(
            dimension_semantics=("arbitrary", "arbitrary"), vmem_limit_bytes=VMEM_LIMIT),
        name="fox_attention",
    )(proj, proj, proj, proj, k_meta, v_meta, gate, gate_meta, eqk)


def _out_kernel(aa_ref, ac_ref, ma_ref, mc_ref, x_ref, wa_ref, wc_ref, wo_ref, gain_ref, o_ref):
    ya = _dot(aa_ref[...], wa_ref[...])
    yc = _dot(ac_ref[...], wc_ref[...])
    merged = (jax.nn.sigmoid(ma_ref[...].astype(F32)) * ya
              + jax.nn.sigmoid(mc_ref[...].astype(F32)) * yc)
    y = x_ref[...] + _dot(merged.astype(BF16), wo_ref[...])
    o_ref[...] = _rmsnorm(y, gain_ref[...])


def _out_call(a_att, a_conv, proj, x2d, wa, wc, wo, gain):
    tm = TILES.out_tm
    r = x2d.shape[0]
    row = lambda i: (i, 0)
    act = pl.BlockSpec((tm, D_MODEL), row)
    gate = lambda name: pl.BlockSpec((None, tm, D_MODEL),
                                     lambda i, g=_PROJ_GROUPS.index(name): (g, i, 0))
    weight = pl.BlockSpec((D_MODEL, D_MODEL), lambda i: (0, 0), pipeline_mode=pl.Buffered(1))
    return pl.pallas_call(
        _out_kernel,
        grid=(r // tm,),
        in_specs=[act, act, gate("m_att"), gate("m_conv"), act, weight, weight, weight,
                  pl.BlockSpec((1, D_MODEL), lambda i: (0, 0))],
        out_specs=act,
        out_shape=jax.ShapeDtypeStruct((r, D_MODEL), F32),
        compiler_params=pltpu.CompilerParams(
            dimension_semantics=("arbitrary",), vmem_limit_bytes=VMEM_LIMIT),
        name="out_proj",
    )(a_att, a_conv, proj, proj, x2d, wa, wc, wo, gain)


def kernel(x, meta_tokens, norm_gain, w_in, b_f, conv_w, w_att_o, w_conv_o, w_out, final_gain):
    batch, seq, _ = x.shape
    assert x.shape == (batch, seq, D_MODEL) and x.dtype == F32
    assert seq % max(TILES.attn_tq, TILES.conv_tm, TILES.norm_tm) == 0
    assert meta_tokens.shape == (N_META, D_MODEL) and b_f.shape == (1, HEADS)
    assert norm_gain.shape[0] == 1, "single-layer block"
    assert w_in.shape == (1, D_MODEL, N_IN) and w_in.dtype == F32
    wt = jnp.transpose(w_in[0])
    bf = jnp.pad(b_f[0].astype(F32), (0, LANES - HEADS)).reshape(1, LANES)
    gain = norm_gain[0].reshape(1, D_MODEL).astype(F32)
    fgain = final_gain.reshape(1, D_MODEL).astype(F32)
    place, ones, eqk = _feature_constants()

    k_meta, v_meta, g_meta, cm, gate_meta = _meta_call(
        meta_tokens.astype(F32), gain, wt, bf, place, ones)
    h, gate = _norm_call(x, gain, wt, bf, cm, place, ones)
    proj = _proj_call(h, wt)
    a_conv, wa, wc, wo = _conv_call(h, wt, g_meta, conv_w[0].astype(F32),
                                    (w_att_o, w_conv_o, w_out), seq)
    a_att = _attn_call(proj, k_meta, v_meta, gate, gate_meta, eqk, batch, seq)
    out = _out_call(a_att, a_conv, proj, x.reshape(batch * seq, D_MODEL), wa, wc, wo, fgain)
    return out.reshape(batch, seq, D_MODEL)
```

```python
import functools
from typing import NamedTuple

import numpy as np

import jax
import jax.numpy as jnp
from jax import lax
from jax.experimental import pallas as pl
from jax.experimental.pallas import tpu as pltpu

D_MODEL = 2048
N_META = 16
HEADS = 16
HEAD_DIM = 128
CONV_K = 3
EPS = 1e-6
LANES = 128
SUBLANES = 8
NEG_BIG = -1e30
LOG2E = 1.4426950408889634
GATE_LANES = LANES // HEADS
SPLIT_TERMS = 3
VMEM_LIMIT = 56 * 1024 * 1024


class Tiles(NamedTuple):
    meta_tn: int = 512
    norm_tm: int = 1024
    norm_chunk: int = 256
    proj_tm: int = 2048
    proj_tn: int = 1024
    proj_rows: int = 512
    conv_tm: int = 2048
    conv_tc: int = 256
    conv_rows: int = 512
    attn_tq: int = 1024
    attn_tk: int = 512
    attn_ts: int = 512
    attn_rb: int = 128
    out_tm: int = 256


TILES = Tiles()

_GROUPS = ("q", "k", "v", "f", "z_att", "u", "gate_b", "gate_c", "z_conv", "m_att", "m_conv")
_WIDTHS = (D_MODEL, D_MODEL, D_MODEL, HEADS) + (D_MODEL,) * 7
COL = {name: sum(_WIDTHS[:n]) for n, name in enumerate(_GROUPS)}
N_IN = sum(_WIDTHS)

F32 = jnp.float32
BF16 = jnp.bfloat16


def _dot(a, b):
    return jnp.dot(a, b, preferred_element_type=F32)


def _dot_nt(a, b):
    return lax.dot_general(a, b, (((1,), (1,)), ((), ())), preferred_element_type=F32)


def _rmsnorm(x, gain):
    return x * lax.rsqrt(jnp.mean(x * x, axis=-1, keepdims=True) + EPS) * gain


def _log_sigmoid(x):
    return jnp.minimum(x, 0.0) - jnp.log1p(jnp.exp(-jnp.abs(x)))


def _silu(x):
    return x * jax.nn.sigmoid(x)


def _split3(x):
    hi = x.astype(BF16)
    r1 = x - hi.astype(F32)
    mid = r1.astype(BF16)
    lo = (r1 - mid.astype(F32)).astype(BF16)
    return hi, mid, lo


def _cumsum_rows(x):
    n = x.shape[0]
    tri = (lax.broadcasted_iota(jnp.int32, (n, n), 0)
           >= lax.broadcasted_iota(jnp.int32, (n, n), 1)).astype(BF16)
    hi, mid, lo = _split3(x)
    return _dot(tri, hi) + _dot(tri, mid) + _dot(tri, lo)


def _gate_features(c, place_ref, ones_ref):
    hi, mid, lo = _split3(c * LOG2E)
    cat = jnp.concatenate([hi, mid, lo], axis=1)
    return (_dot(cat, place_ref[...]) + ones_ref[...]).astype(BF16)


def _feature_constants():
    n = SPLIT_TERMS
    assert 2 * n <= GATE_LANES
    place = np.zeros((n * LANES, 2 * LANES), np.float32)
    ones = np.zeros((1, 2 * LANES), np.float32)
    for h in range(HEADS):
        base = GATE_LANES * h
        for f in range(n):
            place[f * LANES + h, base + f] = 1.0
            ones[0, base + n + f] = 1.0
            ones[0, LANES + base + f] = 1.0
            place[f * LANES + h, LANES + base + n + f] = -1.0
    return jnp.asarray(place, BF16), jnp.asarray(ones, F32)


def _head_features(x, head):
    shift = lax.rem(LANES - GATE_LANES * head, LANES)
    y = pltpu.roll(x.astype(F32), shift, 1)
    lane = lax.broadcasted_iota(jnp.int32, y.shape, 1)
    return jnp.where(lane < GATE_LANES, y, 0.0).astype(BF16)


def _w_tile_spec(tn, row_of):
    assert all(c % HEADS == 0 for c in COL.values()) and tn % HEADS == 0
    return pl.BlockSpec((pl.Element(tn), pl.Element(D_MODEL)),
                        lambda *ids: (pl.multiple_of(row_of(*ids), HEADS), 0))


def _forget_w_spec():
    assert COL["f"] % LANES == 0
    return pl.BlockSpec((LANES, D_MODEL), lambda *ids: (COL["f"] // LANES, 0))


def _stage_weight(src_ref, dst_ref, chunk=128):
    for r in range(0, src_ref.shape[0], chunk):
        dst_ref[r:r + chunk, :] = src_ref[r:r + chunk, :].astype(BF16)


def _meta_kernel(meta_ref, gain_ref, wk_ref, wv_ref, wu_ref, wgc_ref, wf_ref, bf_ref,
                 place_ref, ones_ref, k_ref, v_ref, g_ref, cm_ref, gm_ref):
    h = _rmsnorm(meta_ref[...], gain_ref[...]).astype(BF16)
    proj = lambda w_ref: _dot_nt(h, w_ref[...].astype(BF16))
    k_ref[...] = proj(wk_ref).astype(BF16)
    v_ref[...] = proj(wv_ref).astype(BF16)
    g_ref[...] = proj(wgc_ref) * proj(wu_ref)
    cm = _cumsum_rows(_log_sigmoid(proj(wf_ref) + bf_ref[...]))
    cm_ref[...] = cm
    gm_ref[...] = _gate_features(cm, place_ref, ones_ref)


def _meta_call(meta, gain, wt, bf, place, ones):
    tn = TILES.meta_tn
    nt = D_MODEL // tn
    const = lambda shape: pl.BlockSpec(shape, lambda j: (0,) * len(shape))
    group = lambda name: _w_tile_spec(tn, lambda j: COL[name] + j * tn)
    return pl.pallas_call(
        _meta_kernel,
        grid=(nt,),
        in_specs=[
            const((N_META, D_MODEL)), const((1, D_MODEL)),
            group("k"), group("v"), group("u"), group("gate_c"), _forget_w_spec(),
            const((1, LANES)), const((SPLIT_TERMS * LANES, 2 * LANES)), const((1, 2 * LANES)),
        ],
        out_specs=[
            pl.BlockSpec((N_META, tn), lambda j: (0, j)),
            pl.BlockSpec((N_META, tn), lambda j: (0, j)),
            pl.BlockSpec((N_META, tn), lambda j: (0, j)),
            const((N_META, LANES)),
            const((N_META, 2 * LANES)),
        ],
        out_shape=[
            jax.ShapeDtypeStruct((N_META, D_MODEL), BF16),
            jax.ShapeDtypeStruct((N_META, D_MODEL), BF16),
            jax.ShapeDtypeStruct((N_META, D_MODEL), F32),
            jax.ShapeDtypeStruct((N_META, LANES), F32),
            jax.ShapeDtypeStruct((N_META, 2 * LANES), BF16),
        ],
        compiler_params=pltpu.CompilerParams(
            dimension_semantics=("arbitrary",), vmem_limit_bytes=VMEM_LIMIT),
        name="meta_proj",
    )(meta, gain, wt, wt, wt, wt, wt, bf, place, ones)


def _norm_kernel(x_ref, gain_ref, wf_ref, bf_ref, cm_ref, place_ref, ones_ref,
                 h_ref, gate_ref, carry_ref, *, chunk):
    @pl.when(pl.program_id(1) == 0)
    def _():
        carry_ref[...] = cm_ref[N_META - 1:N_META, :]

    h = _rmsnorm(x_ref[...], gain_ref[...]).astype(BF16)
    h_ref[...] = h
    logf = _log_sigmoid(_dot_nt(h, wf_ref[...].astype(BF16)) + bf_ref[...])
    tm = logf.shape[0]
    run = carry_ref[...]
    parts = []
    for j in range(tm // chunk):
        c = _cumsum_rows(logf[j * chunk:(j + 1) * chunk]) + run
        run = c[chunk - 1:chunk, :]
        parts.append(c)
    carry_ref[...] = run
    gate_ref[...] = _gate_features(jnp.concatenate(parts, axis=0), place_ref, ones_ref)


def _norm_call(x, gain, wt, bf, cm, place, ones):
    tm, chunk = TILES.norm_tm, TILES.norm_chunk
    b, s, _ = x.shape
    nt = s // tm
    const = lambda shape: pl.BlockSpec(shape, lambda bi, i: (0,) * len(shape))
    return pl.pallas_call(
        functools.partial(_norm_kernel, chunk=chunk),
        grid=(b, nt),
        in_specs=[
            pl.BlockSpec((None, tm, D_MODEL), lambda bi, i: (bi, i, 0)),
            const((1, D_MODEL)),
            _forget_w_spec(),
            const((1, LANES)),
            const((N_META, LANES)),
            const((SPLIT_TERMS * LANES, 2 * LANES)),
            const((1, 2 * LANES)),
        ],
        out_specs=[
            pl.BlockSpec((tm, D_MODEL), lambda bi, i: (bi * nt + i, 0)),
            pl.BlockSpec((tm, 2 * LANES), lambda bi, i: (bi * nt + i, 0)),
        ],
        out_shape=[
            jax.ShapeDtypeStruct((b * s, D_MODEL), BF16),
            jax.ShapeDtypeStruct((b * s, 2 * LANES), BF16),
        ],
        scratch_shapes=[pltpu.VMEM((1, LANES), F32)],
        compiler_params=pltpu.CompilerParams(
            dimension_semantics=("arbitrary", "arbitrary"), vmem_limit_bytes=VMEM_LIMIT),
        name="norm_forget",
    )(x, gain, wt, bf, cm, place, ones)


_PROJ_GROUPS = ("q", "k", "v", "z_att", "m_att", "m_conv")


def _proj_group_row(g):
    row = jnp.int32(COL[_PROJ_GROUPS[0]])
    for n in range(1, len(_PROJ_GROUPS)):
        row = jnp.where(g == n, COL[_PROJ_GROUPS[n]], row)
    return row


def _proj_kernel(h_ref, w_ref, o_ref, w_s, *, row_chunk):
    @pl.when(pl.program_id(2) == 0)
    def _():
        _stage_weight(w_ref, w_s)

    scale = jnp.where(pl.program_id(0) == 0, HEAD_DIM ** -0.5 * LOG2E, 1.0).astype(F32)
    for r in range(0, h_ref.shape[0], row_chunk):
        y = _dot_nt(h_ref[r:r + row_chunk, :], w_s[...])
        o_ref[r:r + row_chunk, :] = (y * scale).astype(BF16)


def _proj_call(h, wt):
    tm, tn, row_chunk = TILES.proj_tm, TILES.proj_tn, TILES.proj_rows
    r = h.shape[0]
    ng = len(_PROJ_GROUPS)
    return pl.pallas_call(
        functools.partial(_proj_kernel, row_chunk=row_chunk),
        grid=(ng, D_MODEL // tn, r // tm),
        in_specs=[
            pl.BlockSpec((tm, D_MODEL), lambda g, j, i: (i, 0)),
            _w_tile_spec(tn, lambda g, j, i: _proj_group_row(g) + j * tn),
        ],
        out_specs=pl.BlockSpec((None, tm, tn), lambda g, j, i: (g, i, j)),
        out_shape=jax.ShapeDtypeStruct((ng, r, D_MODEL), BF16),
        scratch_shapes=[pltpu.VMEM((tn, D_MODEL), BF16)],
        compiler_params=pltpu.CompilerParams(
            dimension_semantics=("arbitrary", "arbitrary", "arbitrary"),
            vmem_limit_bytes=VMEM_LIMIT),
        name="plain_proj",
    )(h, wt)


_CONV_GROUPS = ("u", "gate_b", "gate_c", "z_conv")


def _conv_kernel(h_ref, wu_ref, wgb_ref, wgc_ref, wzc_ref, gm_ref, cw_ref,
                 wa_ref, wc_ref, wo_ref, o_ref, wa_out, wc_out, wo_out,
                 w_s, gbuf_ref, *, tiles_per_batch, row_chunk):
    i = pl.program_id(1)
    tm = h_ref.shape[0]
    for src, dst in ((wa_ref, wa_out), (wc_ref, wc_out), (wo_ref, wo_out)):
        dst[...] = src[...].astype(BF16)

    @pl.when(i == 0)
    def _():
        for n, w_ref in enumerate((wu_ref, wgb_ref, wgc_ref, wzc_ref)):
            _stage_weight(w_ref, w_s.at[n])

    @pl.when(i % tiles_per_batch == 0)
    def _():
        gbuf_ref[0:SUBLANES, :] = gm_ref[N_META - SUBLANES:N_META, :]

    for r in range(0, tm, row_chunk):
        h = h_ref[r:r + row_chunk, :]
        g = _dot_nt(h, w_s[2]) * _dot_nt(h, w_s[0])
        first = SUBLANES + r
        gbuf_ref[first:first + row_chunk, :] = g
        conv = (gbuf_ref[first - 2:first - 2 + row_chunk, :] * cw_ref[0:1, :]
                + gbuf_ref[first - 1:first - 1 + row_chunk, :] * cw_ref[1:2, :]
                + g * cw_ref[2:3, :])
        gb = _dot_nt(h, w_s[1])
        zc = _dot_nt(h, w_s[3])
        o_ref[r:r + row_chunk, :] = (gb * conv * _silu(zc)).astype(BF16)
    gbuf_ref[0:SUBLANES, :] = gbuf_ref[tm:tm + SUBLANES, :]


def _conv_call(h, wt, g_meta, conv_w, out_weights, rows_per_batch):
    tm, tc, row_chunk = TILES.conv_tm, TILES.conv_tc, TILES.conv_rows
    r = h.shape[0]
    nrow = r // tm
    steps = (D_MODEL // tc) * nrow
    slab = D_MODEL // steps
    assert slab * steps == D_MODEL and slab % (2 * SUBLANES) == 0
    w_specs = [_w_tile_spec(tc, lambda j, i, name=name: COL[name] + j * tc)
               for name in _CONV_GROUPS]
    slab_in = pl.BlockSpec((None, slab, D_MODEL), lambda j, i: (0, j * nrow + i, 0))
    slab_out = pl.BlockSpec((slab, D_MODEL), lambda j, i: (j * nrow + i, 0))
    cast_shape = jax.ShapeDtypeStruct((D_MODEL, D_MODEL), BF16)
    return pl.pallas_call(
        functools.partial(_conv_kernel, tiles_per_batch=rows_per_batch // tm, row_chunk=row_chunk),
        grid=(D_MODEL // tc, nrow),
        in_specs=[pl.BlockSpec((tm, D_MODEL), lambda j, i: (i, 0))] + w_specs + [
            pl.BlockSpec((N_META, tc), lambda j, i: (0, j)),
            pl.BlockSpec((CONV_K, tc), lambda j, i: (0, j)),
            slab_in, slab_in, slab_in,
        ],
        out_specs=[pl.BlockSpec((tm, tc), lambda j, i: (i, j)), slab_out, slab_out, slab_out],
        out_shape=[jax.ShapeDtypeStruct((r, D_MODEL), BF16), cast_shape, cast_shape, cast_shape],
        scratch_shapes=[pltpu.VMEM((len(_CONV_GROUPS), tc, D_MODEL), BF16),
                        pltpu.VMEM((SUBLANES + tm, tc), F32)],
        compiler_params=pltpu.CompilerParams(
            dimension_semantics=("arbitrary", "arbitrary"), vmem_limit_bytes=VMEM_LIMIT),
        name="conv_proj",
    )(h, *([wt] * len(_CONV_GROUPS)), g_meta, conv_w, *out_weights)


def _attn_kernel(q_ref, k_ref, v_ref, z_ref, km_ref, vm_ref, g_ref, gm_ref,
                 o_ref, qa_ref, kf_ref, va_ref, tri_ref, s_ref, tmax_ref, m_ref, acc_ref,
                 *, tq, tk, ts, rb):
    seq = q_ref.shape[0]
    nq = seq // tq
    assert tq == 2 * tk and seq % tq == 0
    half = tk // 2
    head = pl.program_id(1)

    tri_ref[...] = jnp.where(
        lax.broadcasted_iota(jnp.int32, (tk, tk), 1) <= lax.broadcasted_iota(jnp.int32, (tk, tk), 0),
        0.0, NEG_BIG)
    kma = jnp.concatenate(
        [km_ref[...], _head_features(gm_ref[:, LANES:], head)], axis=1)
    pad_lane = lax.broadcasted_iota(jnp.int32, (LANES - N_META, 2 * LANES), 1)
    k_meta_blk = jnp.concatenate(
        [kma, jnp.where(pad_lane == LANES + SPLIT_TERMS, NEG_BIG, 0.0).astype(BF16)],
        axis=0)
    v_meta_blk = jnp.concatenate(
        [vm_ref[...], jnp.ones((N_META, LANES), BF16)], axis=1)
    v_meta_blk = jnp.concatenate(
        [v_meta_blk, jnp.zeros((LANES - N_META, 2 * LANES), BF16)], axis=0)

    def setup(row0):
        rows = slice(row0, row0 + ts)
        kf_ref[rows, :] = _head_features(g_ref[rows, LANES:], head)
        qa_ref[rows, :] = jnp.concatenate(
            [q_ref[rows, :], _head_features(g_ref[rows, :LANES], head)], axis=1)
        va_ref[rows, :] = jnp.concatenate([v_ref[rows, :], jnp.ones((ts, LANES), BF16)], axis=1)
        m_ref[rows, :] = jnp.full((ts, LANES), NEG_BIG, F32)
        acc_ref[rows, :] = jnp.zeros((ts, 2 * LANES), F32)

    def width(key0, nkeys):
        return nkeys + LANES if key0 == 0 else nkeys

    def stage_a(row0, nrows, key0, nkeys, tri, slot):
        keys = slice(key0, key0 + nkeys)
        ka = jnp.concatenate([k_ref[keys, :], kf_ref[keys, :]], axis=1)
        if key0 == 0:
            ka = jnp.concatenate([ka, k_meta_blk], axis=0)
        s = _dot_nt(qa_ref[row0:row0 + nrows, :], ka)
        w = width(key0, nkeys)
        for r0 in range(0, nrows, rb):
            part = s[r0:r0 + rb]
            if tri:
                masked = part[:, 0:nkeys] + tri_ref[r0:r0 + rb, 0:nkeys]
                part = jnp.concatenate([masked, part[:, nkeys:]], axis=1) if w > nkeys else masked
            s_ref[slot, r0:r0 + rb, 0:w] = part
            t = part[:, 0:LANES]
            for c in range(1, w // LANES):
                t = jnp.maximum(t, part[:, c * LANES:(c + 1) * LANES])
            tmax_ref[slot, r0:r0 + rb, :] = jnp.broadcast_to(
                jnp.max(t, axis=1, keepdims=True), (rb, LANES))

    def stage_b(row0, nrows, key0, nkeys, tri, slot):
        del tri
        w = width(key0, nkeys)
        alphas, ps = [], []
        for r0 in range(0, nrows, rb):
            rows = slice(row0 + r0, row0 + r0 + rb)
            m_old = m_ref[rows, :]
            m_new = jnp.maximum(m_old, tmax_ref[slot, r0:r0 + rb, :])
            m_ref[rows, :] = m_new
            alphas.append(jnp.exp2(m_old - m_new))
            ps.append(jnp.concatenate(
                [jnp.exp2(s_ref[slot, r0:r0 + rb, c * LANES:(c + 1) * LANES] - m_new).astype(BF16)
                 for c in range(w // LANES)], axis=1))
        va = va_ref[key0:key0 + nkeys, :]
        if key0 == 0:
            va = jnp.concatenate([va, v_meta_blk], axis=0)
        pv = _dot(jnp.concatenate(ps, axis=0), va)
        for n, r0 in enumerate(range(0, nrows, rb)):
            rows = slice(row0 + r0, row0 + r0 + rb)
            alpha = jnp.concatenate([alphas[n], alphas[n]], axis=1)
            acc_ref[rows, :] = alpha * acc_ref[rows, :] + pv[r0:r0 + rb]

    def diagonal(row0):
        return [(row0, half, row0, half, True),
                (row0 + half, half, row0, half, False),
                (row0 + half, half, row0 + half, half, True)]

    items = []
    for i in range(nq):
        top = i * tq
        items += [(top, tq, j * tk, tk, False) for j in range(2 * i)]
        items += [(top + tk, tk, top, tk, False)]
        items += diagonal(top) + diagonal(top + tk)

    def finalize(row0, nrows):
        for r0 in range(row0, row0 + nrows, rb):
            o = (acc_ref[r0:r0 + rb, :LANES] / acc_ref[r0:r0 + rb, LANES:]
                 * _silu(z_ref[r0:r0 + rb, :].astype(F32)))
            o_ref[r0:r0 + rb, :] = o.astype(BF16)

    rows_ready = [0]

    def scores(item, slot):
        row0, nrows, key0, nkeys, _ = item
        while rows_ready[0] < max(row0 + nrows, key0 + nkeys):
            setup(rows_ready[0])
            rows_ready[0] += ts
        stage_a(*item, slot)

    scores(items[0], 0)
    for n, item in enumerate(items):
        if n + 1 < len(items):
            scores(items[n + 1], 1 - n % 2)
        stage_b(*item, n % 2)
        if item[4]:
            finalize(item[0], item[1])


def _attn_call(proj, k_meta, v_meta, gate, gate_meta, batch, seq):
    tq, tk, ts, rb = TILES.attn_tq, TILES.attn_tk, TILES.attn_ts, TILES.attn_rb
    r = batch * seq
    head = lambda g: pl.BlockSpec((None, seq, HEAD_DIM), lambda b, h, g=g: (g, b, h))
    return pl.pallas_call(
        functools.partial(_attn_kernel, tq=tq, tk=tk, ts=ts, rb=rb),
        grid=(batch, HEADS),
        in_specs=[
            head(0), head(1), head(2), head(3),
            pl.BlockSpec((N_META, HEAD_DIM), lambda b, h: (0, h)),
            pl.BlockSpec((N_META, HEAD_DIM), lambda b, h: (0, h)),
            pl.BlockSpec((seq, 2 * LANES), lambda b, h: (b, 0)),
            pl.BlockSpec((N_META, 2 * LANES), lambda b, h: (0, 0)),
        ],
        out_specs=pl.BlockSpec((seq, HEAD_DIM), lambda b, h: (b, h)),
        out_shape=jax.ShapeDtypeStruct((r, D_MODEL), BF16),
        scratch_shapes=[
            pltpu.VMEM((seq, 2 * LANES), BF16),
            pltpu.VMEM((seq, LANES), BF16),
            pltpu.VMEM((seq, 2 * LANES), BF16),
            pltpu.VMEM((tk, tk), F32),
            pltpu.VMEM((2, tq, tk + LANES), F32),
            pltpu.VMEM((2, tq, LANES), F32),
            pltpu.VMEM((seq, LANES), F32),
            pltpu.VMEM((seq, 2 * LANES), F32),
        ],
        compiler_params=pltpu.CompilerParams(
            dimension_semantics=("arbitrary", "arbitrary"), vmem_limit_bytes=VMEM_LIMIT),
        name="fox_attention",
    )(proj, proj, proj, proj, k_meta, v_meta, gate, gate_meta)


def _out_kernel(aa_ref, ac_ref, ma_ref, mc_ref, x_ref, wa_ref, wc_ref, wo_ref, gain_ref, o_ref):
    ya = _dot(aa_ref[...], wa_ref[...])
    yc = _dot(ac_ref[...], wc_ref[...])
    merged = (jax.nn.sigmoid(ma_ref[...].astype(F32)) * ya
              + jax.nn.sigmoid(mc_ref[...].astype(F32)) * yc)
    y = x_ref[...] + _dot(merged.astype(BF16), wo_ref[...])
    o_ref[...] = _rmsnorm(y, gain_ref[...])


def _out_call(a_att, a_conv, proj, x2d, wa, wc, wo, gain):
    tm = TILES.out_tm
    r = x2d.shape[0]
    row = lambda i: (i, 0)
    act = pl.BlockSpec((tm, D_MODEL), row)
    gate = lambda name: pl.BlockSpec((None, tm, D_MODEL),
                                     lambda i, g=_PROJ_GROUPS.index(name): (g, i, 0))
    weight = pl.BlockSpec((D_MODEL, D_MODEL), lambda i: (0, 0), pipeline_mode=pl.Buffered(1))
    return pl.pallas_call(
        _out_kernel,
        grid=(r // tm,),
        in_specs=[act, act, gate("m_att"), gate("m_conv"), act, weight, weight, weight,
                  pl.BlockSpec((1, D_MODEL), lambda i: (0, 0))],
        out_specs=act,
        out_shape=jax.ShapeDtypeStruct((r, D_MODEL), F32),
        compiler_params=pltpu.CompilerParams(
            dimension_semantics=("arbitrary",), vmem_limit_bytes=VMEM_LIMIT),
        name="out_proj",
    )(a_att, a_conv, proj, proj, x2d, wa, wc, wo, gain)


def kernel(x, meta_tokens, norm_gain, w_in, b_f, conv_w, w_att_o, w_conv_o, w_out, final_gain):
    batch, seq, _ = x.shape
    assert x.shape == (batch, seq, D_MODEL) and x.dtype == F32
    assert seq % max(TILES.attn_tq, TILES.conv_tm, TILES.norm_tm) == 0
    assert meta_tokens.shape == (N_META, D_MODEL) and b_f.shape == (1, HEADS)
    assert norm_gain.shape[0] == 1, "single-layer block"
    assert w_in.shape == (1, D_MODEL, N_IN) and w_in.dtype == F32
    wt = jnp.transpose(w_in[0])
    bf = jnp.pad(b_f[0].astype(F32), (0, LANES - HEADS)).reshape(1, LANES)
    gain = norm_gain[0].reshape(1, D_MODEL).astype(F32)
    fgain = final_gain.reshape(1, D_MODEL).astype(F32)
    place, ones = _feature_constants()

    k_meta, v_meta, g_meta, cm, gate_meta = _meta_call(
        meta_tokens.astype(F32), gain, wt, bf, place, ones)
    h, gate = _norm_call(x, gain, wt, bf, cm, place, ones)
    proj = _proj_call(h, wt)
    a_conv, wa, wc, wo = _conv_call(h, wt, g_meta, conv_w[0].astype(F32),
                                    (w_att_o, w_conv_o, w_out), seq)
    a_att = _attn_call(proj, k_meta, v_meta, gate, gate_meta, batch, seq)
    out = _out_call(a_att, a_conv, proj, x.reshape(batch * seq, D_MODEL), wa, wc, wo, fgain)
    return out.reshape(batch, seq, D_MODEL)
```

```python
import functools
from typing import NamedTuple

import numpy as np

import jax
import jax.numpy as jnp
from jax import lax
from jax.experimental import pallas as pl
from jax.experimental.pallas import tpu as pltpu

D_MODEL = 2048
N_META = 16
HEADS = 16
HEAD_DIM = 128
CONV_K = 3
EPS = 1e-6
LANES = 128
SUBLANES = 8
NEG_BIG = -1e30
LOG2E = 1.4426950408889634
GATE_LANES = LANES // HEADS
SPLIT_TERMS = 3
VMEM_LIMIT = 56 * 1024 * 1024


class Tiles(NamedTuple):
    meta_tn: int = 512
    norm_tm: int = 1024
    norm_chunk: int = 256
    proj_tm: int = 2048
    proj_tn: int = 1024
    proj_rows: int = 512
    conv_tm: int = 2048
    conv_tc: int = 256
    conv_rows: int = 512
    attn_tq: int = 1024
    attn_tk: int = 512
    attn_ts: int = 512
    attn_rb: int = 128
    out_tm: int = 256


TILES = Tiles()

_GROUPS = ("q", "k", "v", "f", "z_att", "u", "gate_b", "gate_c", "z_conv", "m_att", "m_conv")
_WIDTHS = (D_MODEL, D_MODEL, D_MODEL, HEADS) + (D_MODEL,) * 7
COL = {name: sum(_WIDTHS[:n]) for n, name in enumerate(_GROUPS)}
N_IN = sum(_WIDTHS)

F32 = jnp.float32
BF16 = jnp.bfloat16


def _dot(a, b):
    return jnp.dot(a, b, preferred_element_type=F32)


def _dot_nt(a, b):
    return lax.dot_general(a, b, (((1,), (1,)), ((), ())), preferred_element_type=F32)


def _rmsnorm(x, gain):
    return x * lax.rsqrt(jnp.mean(x * x, axis=-1, keepdims=True) + EPS) * gain


def _log_sigmoid(x):
    return jnp.minimum(x, 0.0) - jnp.log1p(jnp.exp(-jnp.abs(x)))


def _silu(x):
    return x * jax.nn.sigmoid(x)


def _split3(x):
    hi = x.astype(BF16)
    r1 = x - hi.astype(F32)
    mid = r1.astype(BF16)
    lo = (r1 - mid.astype(F32)).astype(BF16)
    return hi, mid, lo


def _cumsum_rows(x):
    n = x.shape[0]
    tri = (lax.broadcasted_iota(jnp.int32, (n, n), 0)
           >= lax.broadcasted_iota(jnp.int32, (n, n), 1)).astype(BF16)
    hi, mid, lo = _split3(x)
    return _dot(tri, hi) + _dot(tri, mid) + _dot(tri, lo)


def _gate_features(c, place_ref, ones_ref):
    hi, mid, lo = _split3(c * LOG2E)
    cat = jnp.concatenate([hi, mid, lo], axis=1)
    return (_dot(cat, place_ref[...]) + ones_ref[...]).astype(BF16)


def _feature_constants():
    n = SPLIT_TERMS
    assert n + 1 <= GATE_LANES
    place = np.zeros((n * LANES, LANES), np.float32)
    ones = np.zeros((1, LANES), np.float32)
    eq = np.zeros((HEADS, LANES, LANES), np.float32)
    ek = np.zeros((HEADS, LANES, LANES), np.float32)
    for h in range(HEADS):
        base = GATE_LANES * h
        ones[0, base + n] = 1.0
        for f in range(n):
            place[f * LANES + h, base + f] = 1.0
            eq[h, base + f, f] = 1.0
            eq[h, base + n, n + f] = 1.0
            ek[h, base + n, f] = 1.0
            ek[h, base + f, n + f] = -1.0
    return (jnp.asarray(place, BF16), jnp.asarray(ones, F32),
            jnp.asarray(np.concatenate([eq, ek], axis=2), BF16))


def _w_tile_spec(tn, row_of):
    assert all(c % HEADS == 0 for c in COL.values()) and tn % HEADS == 0
    return pl.BlockSpec((pl.Element(tn), pl.Element(D_MODEL)),
                        lambda *ids: (pl.multiple_of(row_of(*ids), HEADS), 0))


def _forget_w_spec():
    assert COL["f"] % LANES == 0
    return pl.BlockSpec((LANES, D_MODEL), lambda *ids: (COL["f"] // LANES, 0))


def _stage_weight(src_ref, dst_ref, chunk=128):
    for r in range(0, src_ref.shape[0], chunk):
        dst_ref[r:r + chunk, :] = src_ref[r:r + chunk, :].astype(BF16)


def _meta_kernel(meta_ref, gain_ref, wk_ref, wv_ref, wu_ref, wgc_ref, wf_ref, bf_ref,
                 place_ref, ones_ref, k_ref, v_ref, g_ref, cm_ref, gm_ref):
    h = _rmsnorm(meta_ref[...], gain_ref[...]).astype(BF16)
    proj = lambda w_ref: _dot_nt(h, w_ref[...].astype(BF16))
    k_ref[...] = proj(wk_ref).astype(BF16)
    v_ref[...] = proj(wv_ref).astype(BF16)
    g_ref[...] = proj(wgc_ref) * proj(wu_ref)
    cm = _cumsum_rows(_log_sigmoid(proj(wf_ref) + bf_ref[...]))
    cm_ref[...] = cm
    gm_ref[...] = _gate_features(cm, place_ref, ones_ref)


def _meta_call(meta, gain, wt, bf, place, ones):
    tn = TILES.meta_tn
    nt = D_MODEL // tn
    const = lambda shape: pl.BlockSpec(shape, lambda j: (0,) * len(shape))
    group = lambda name: _w_tile_spec(tn, lambda j: COL[name] + j * tn)
    return pl.pallas_call(
        _meta_kernel,
        grid=(nt,),
        in_specs=[
            const((N_META, D_MODEL)), const((1, D_MODEL)),
            group("k"), group("v"), group("u"), group("gate_c"), _forget_w_spec(),
            const((1, LANES)), const((SPLIT_TERMS * LANES, LANES)), const((1, LANES)),
        ],
        out_specs=[
            pl.BlockSpec((N_META, tn), lambda j: (0, j)),
            pl.BlockSpec((N_META, tn), lambda j: (0, j)),
            pl.BlockSpec((N_META, tn), lambda j: (0, j)),
            const((N_META, LANES)),
            const((N_META, LANES)),
        ],
        out_shape=[
            jax.ShapeDtypeStruct((N_META, D_MODEL), BF16),
            jax.ShapeDtypeStruct((N_META, D_MODEL), BF16),
            jax.ShapeDtypeStruct((N_META, D_MODEL), F32),
            jax.ShapeDtypeStruct((N_META, LANES), F32),
            jax.ShapeDtypeStruct((N_META, LANES), BF16),
        ],
        compiler_params=pltpu.CompilerParams(
            dimension_semantics=("arbitrary",), vmem_limit_bytes=VMEM_LIMIT),
        name="meta_proj",
    )(meta, gain, wt, wt, wt, wt, wt, bf, place, ones)


def _norm_kernel(x_ref, gain_ref, wf_ref, bf_ref, cm_ref, place_ref, ones_ref,
                 h_ref, gate_ref, carry_ref, *, chunk):
    @pl.when(pl.program_id(1) == 0)
    def _():
        carry_ref[...] = cm_ref[N_META - 1:N_META, :]

    h = _rmsnorm(x_ref[...], gain_ref[...]).astype(BF16)
    h_ref[...] = h
    logf = _log_sigmoid(_dot_nt(h, wf_ref[...].astype(BF16)) + bf_ref[...])
    tm = logf.shape[0]
    run = carry_ref[...]
    parts = []
    for j in range(tm // chunk):
        c = _cumsum_rows(logf[j * chunk:(j + 1) * chunk]) + run
        run = c[chunk - 1:chunk, :]
        parts.append(c)
    carry_ref[...] = run
    gate_ref[...] = _gate_features(jnp.concatenate(parts, axis=0), place_ref, ones_ref)


def _norm_call(x, gain, wt, bf, cm, place, ones):
    tm, chunk = TILES.norm_tm, TILES.norm_chunk
    b, s, _ = x.shape
    nt = s // tm
    const = lambda shape: pl.BlockSpec(shape, lambda bi, i: (0,) * len(shape))
    return pl.pallas_call(
        functools.partial(_norm_kernel, chunk=chunk),
        grid=(b, nt),
        in_specs=[
            pl.BlockSpec((None, tm, D_MODEL), lambda bi, i: (bi, i, 0)),
            const((1, D_MODEL)),
            _forget_w_spec(),
            const((1, LANES)),
            const((N_META, LANES)),
            const((SPLIT_TERMS * LANES, LANES)),
            const((1, LANES)),
        ],
        out_specs=[
            pl.BlockSpec((tm, D_MODEL), lambda bi, i: (bi * nt + i, 0)),
            pl.BlockSpec((tm, LANES), lambda bi, i: (bi * nt + i, 0)),
        ],
        out_shape=[
            jax.ShapeDtypeStruct((b * s, D_MODEL), BF16),
            jax.ShapeDtypeStruct((b * s, LANES), BF16),
        ],
        scratch_shapes=[pltpu.VMEM((1, LANES), F32)],
        compiler_params=pltpu.CompilerParams(
            dimension_semantics=("arbitrary", "arbitrary"), vmem_limit_bytes=VMEM_LIMIT),
        name="norm_forget",
    )(x, gain, wt, bf, cm, place, ones)


_PROJ_GROUPS = ("q", "k", "v", "z_att", "m_att", "m_conv")


def _proj_group_row(g):
    row = jnp.int32(COL[_PROJ_GROUPS[0]])
    for n in range(1, len(_PROJ_GROUPS)):
        row = jnp.where(g == n, COL[_PROJ_GROUPS[n]], row)
    return row


def _proj_kernel(h_ref, w_ref, o_ref, w_s, *, row_chunk):
    @pl.when(pl.program_id(2) == 0)
    def _():
        _stage_weight(w_ref, w_s)

    scale = jnp.where(pl.program_id(0) == 0, HEAD_DIM ** -0.5 * LOG2E, 1.0).astype(F32)
    for r in range(0, h_ref.shape[0], row_chunk):
        y = _dot_nt(h_ref[r:r + row_chunk, :], w_s[...])
        o_ref[r:r + row_chunk, :] = (y * scale).astype(BF16)


def _proj_call(h, wt):
    tm, tn, row_chunk = TILES.proj_tm, TILES.proj_tn, TILES.proj_rows
    r = h.shape[0]
    ng = len(_PROJ_GROUPS)
    return pl.pallas_call(
        functools.partial(_proj_kernel, row_chunk=row_chunk),
        grid=(ng, D_MODEL // tn, r // tm),
        in_specs=[
            pl.BlockSpec((tm, D_MODEL), lambda g, j, i: (i, 0)),
            _w_tile_spec(tn, lambda g, j, i: _proj_group_row(g) + j * tn),
        ],
        out_specs=pl.BlockSpec((None, tm, tn), lambda g, j, i: (g, i, j)),
        out_shape=jax.ShapeDtypeStruct((ng, r, D_MODEL), BF16),
        scratch_shapes=[pltpu.VMEM((tn, D_MODEL), BF16)],
        compiler_params=pltpu.CompilerParams(
            dimension_semantics=("arbitrary", "arbitrary", "arbitrary"),
            vmem_limit_bytes=VMEM_LIMIT),
        name="plain_proj",
    )(h, wt)


_CONV_GROUPS = ("u", "gate_b", "gate_c", "z_conv")


def _conv_kernel(h_ref, wu_ref, wgb_ref, wgc_ref, wzc_ref, gm_ref, cw_ref,
                 wa_ref, wc_ref, wo_ref, o_ref, wa_out, wc_out, wo_out,
                 w_s, gbuf_ref, *, tiles_per_batch, row_chunk):
    i = pl.program_id(1)
    tm = h_ref.shape[0]
    for src, dst in ((wa_ref, wa_out), (wc_ref, wc_out), (wo_ref, wo_out)):
        dst[...] = src[...].astype(BF16)

    @pl.when(i == 0)
    def _():
        for n, w_ref in enumerate((wu_ref, wgb_ref, wgc_ref, wzc_ref)):
            _stage_weight(w_ref, w_s.at[n])

    @pl.when(i % tiles_per_batch == 0)
    def _():
        gbuf_ref[0:SUBLANES, :] = gm_ref[N_META - SUBLANES:N_META, :]

    for r in range(0, tm, row_chunk):
        h = h_ref[r:r + row_chunk, :]
        g = _dot_nt(h, w_s[2]) * _dot_nt(h, w_s[0])
        first = SUBLANES + r
        gbuf_ref[first:first + row_chunk, :] = g
        conv = (gbuf_ref[first - 2:first - 2 + row_chunk, :] * cw_ref[0:1, :]
                + gbuf_ref[first - 1:first - 1 + row_chunk, :] * cw_ref[1:2, :]
                + g * cw_ref[2:3, :])
        gb = _dot_nt(h, w_s[1])
        zc = _dot_nt(h, w_s[3])
        o_ref[r:r + row_chunk, :] = (gb * conv * _silu(zc)).astype(BF16)
    gbuf_ref[0:SUBLANES, :] = gbuf_ref[tm:tm + SUBLANES, :]


def _conv_call(h, wt, g_meta, conv_w, out_weights, rows_per_batch):
    tm, tc, row_chunk = TILES.conv_tm, TILES.conv_tc, TILES.conv_rows
    r = h.shape[0]
    nrow = r // tm
    steps = (D_MODEL // tc) * nrow
    slab = D_MODEL // steps
    assert slab * steps == D_MODEL and slab % (2 * SUBLANES) == 0
    w_specs = [_w_tile_spec(tc, lambda j, i, name=name: COL[name] + j * tc)
               for name in _CONV_GROUPS]
    slab_in = pl.BlockSpec((None, slab, D_MODEL), lambda j, i: (0, j * nrow + i, 0))
    slab_out = pl.BlockSpec((slab, D_MODEL), lambda j, i: (j * nrow + i, 0))
    cast_shape = jax.ShapeDtypeStruct((D_MODEL, D_MODEL), BF16)
    return pl.pallas_call(
        functools.partial(_conv_kernel, tiles_per_batch=rows_per_batch // tm, row_chunk=row_chunk),
        grid=(D_MODEL // tc, nrow),
        in_specs=[pl.BlockSpec((tm, D_MODEL), lambda j, i: (i, 0))] + w_specs + [
            pl.BlockSpec((N_META, tc), lambda j, i: (0, j)),
            pl.BlockSpec((CONV_K, tc), lambda j, i: (0, j)),
            slab_in, slab_in, slab_in,
        ],
        out_specs=[pl.BlockSpec((tm, tc), lambda j, i: (i, j)), slab_out, slab_out, slab_out],
        out_shape=[jax.ShapeDtypeStruct((r, D_MODEL), BF16), cast_shape, cast_shape, cast_shape],
        scratch_shapes=[pltpu.VMEM((len(_CONV_GROUPS), tc, D_MODEL), BF16),
                        pltpu.VMEM((SUBLANES + tm, tc), F32)],
        compiler_params=pltpu.CompilerParams(
            dimension_semantics=("arbitrary", "arbitrary"), vmem_limit_bytes=VMEM_LIMIT),
        name="conv_proj",
    )(h, *([wt] * len(_CONV_GROUPS)), g_meta, conv_w, *out_weights)


def _attn_kernel(q_ref, k_ref, v_ref, z_ref, km_ref, vm_ref, g_ref, gm_ref, eqk_ref,
                 o_ref, qa_ref, kf_ref, va_ref, tri_ref, s_ref, tmax_ref, m_ref, acc_ref,
                 *, tq, tk, ts, rb):
    seq = q_ref.shape[0]
    nq = seq // tq
    assert tq == 2 * tk and seq % tq == 0
    half = tk // 2

    tri_ref[...] = jnp.where(
        lax.broadcasted_iota(jnp.int32, (tk, tk), 1) <= lax.broadcasted_iota(jnp.int32, (tk, tk), 0),
        0.0, NEG_BIG)
    kma = jnp.concatenate(
        [km_ref[...], _dot(gm_ref[...], eqk_ref[:, LANES:]).astype(BF16)], axis=1)
    pad_lane = lax.broadcasted_iota(jnp.int32, (LANES - N_META, 2 * LANES), 1)
    k_meta_blk = jnp.concatenate(
        [kma, jnp.where(pad_lane == LANES + SPLIT_TERMS, NEG_BIG, 0.0).astype(BF16)],
        axis=0)
    v_meta_blk = jnp.concatenate(
        [vm_ref[...], jnp.ones((N_META, LANES), BF16)], axis=1)
    v_meta_blk = jnp.concatenate(
        [v_meta_blk, jnp.zeros((LANES - N_META, 2 * LANES), BF16)], axis=0)

    def setup(row0):
        rows = slice(row0, row0 + ts)
        feat = _dot(g_ref[rows, :], eqk_ref[...]).astype(BF16)
        kf_ref[rows, :] = feat[:, LANES:]
        qa_ref[rows, :] = jnp.concatenate([q_ref[rows, :], feat[:, :LANES]], axis=1)
        va_ref[rows, :] = jnp.concatenate([v_ref[rows, :], jnp.ones((ts, LANES), BF16)], axis=1)
        m_ref[rows, :] = jnp.full((ts, LANES), NEG_BIG, F32)
        acc_ref[rows, :] = jnp.zeros((ts, 2 * LANES), F32)

    def width(key0, nkeys):
        return nkeys + LANES if key0 == 0 else nkeys

    def stage_a(row0, nrows, key0, nkeys, tri, slot):
        keys = slice(key0, key0 + nkeys)
        ka = jnp.concatenate([k_ref[keys, :], kf_ref[keys, :]], axis=1)
        if key0 == 0:
            ka = jnp.concatenate([ka, k_meta_blk], axis=0)
        s = _dot_nt(qa_ref[row0:row0 + nrows, :], ka)
        w = width(key0, nkeys)
        for r0 in range(0, nrows, rb):
            part = s[r0:r0 + rb]
            if tri:
                masked = part[:, 0:nkeys] + tri_ref[r0:r0 + rb, 0:nkeys]
                part = jnp.concatenate([masked, part[:, nkeys:]], axis=1) if w > nkeys else masked
            s_ref[slot, r0:r0 + rb, 0:w] = part
            t = part[:, 0:LANES]
            for c in range(1, w // LANES):
                t = jnp.maximum(t, part[:, c * LANES:(c + 1) * LANES])
            tmax_ref[slot, r0:r0 + rb, :] = jnp.broadcast_to(
                jnp.max(t, axis=1, keepdims=True), (rb, LANES))

    def stage_b(row0, nrows, key0, nkeys, tri, slot):
        del tri
        w = width(key0, nkeys)
        alphas, ps = [], []
        for r0 in range(0, nrows, rb):
            rows = slice(row0 + r0, row0 + r0 + rb)
            m_old = m_ref[rows, :]
            m_new = jnp.maximum(m_old, tmax_ref[slot, r0:r0 + rb, :])
            m_ref[rows, :] = m_new
            alphas.append(jnp.exp2(m_old - m_new))
            ps.append(jnp.concatenate(
                [jnp.exp2((s_ref[slot, r0:r0 + rb, c * LANES:(c + 1) * LANES] - m_new).astype(BF16))
                 for c in range(w // LANES)], axis=1))
        va = va_ref[key0:key0 + nkeys, :]
        if key0 == 0:
            va = jnp.concatenate([va, v_meta_blk], axis=0)
        pv = _dot(jnp.concatenate(ps, axis=0), va)
        for n, r0 in enumerate(range(0, nrows, rb)):
            rows = slice(row0 + r0, row0 + r0 + rb)
            alpha = jnp.concatenate([alphas[n], alphas[n]], axis=1)
            acc_ref[rows, :] = alpha * acc_ref[rows, :] + pv[r0:r0 + rb]

    def diagonal(row0):
        return [(row0, half, row0, half, True),
                (row0 + half, half, row0, half, False),
                (row0 + half, half, row0 + half, half, True)]

    items = []
    for i in range(nq):
        top = i * tq
        items += [(top, tq, j * tk, tk, False) for j in range(2 * i)]
        items += [(top + tk, tk, top, tk, False)]
        items += diagonal(top) + diagonal(top + tk)

    def finalize(row0, nrows):
        for r0 in range(row0, row0 + nrows, rb):
            o = (acc_ref[r0:r0 + rb, :LANES] / acc_ref[r0:r0 + rb, LANES:]
                 * _silu(z_ref[r0:r0 + rb, :].astype(F32)))
            o_ref[r0:r0 + rb, :] = o.astype(BF16)

    rows_ready = [0]

    def scores(item, slot):
        row0, nrows, key0, nkeys, _ = item
        while rows_ready[0] < max(row0 + nrows, key0 + nkeys):
            setup(rows_ready[0])
            rows_ready[0] += ts
        stage_a(*item, slot)

    scores(items[0], 0)
    for n, item in enumerate(items):
        if n + 1 < len(items):
            scores(items[n + 1], 1 - n % 2)
        stage_b(*item, n % 2)
        if item[4]:
            finalize(item[0], item[1])


def _attn_call(proj, k_meta, v_meta, gate, gate_meta, eqk, batch, seq):
    tq, tk, ts, rb = TILES.attn_tq, TILES.attn_tk, TILES.attn_ts, TILES.attn_rb
    r = batch * seq
    head = lambda g: pl.BlockSpec((None, seq, HEAD_DIM), lambda b, h, g=g: (g, b, h))
    return pl.pallas_call(
        functools.partial(_attn_kernel, tq=tq, tk=tk, ts=ts, rb=rb),
        grid=(batch, HEADS),
        in_specs=[
            head(0), head(1), head(2), head(3),
            pl.BlockSpec((N_META, HEAD_DIM), lambda b, h: (0, h)),
            pl.BlockSpec((N_META, HEAD_DIM), lambda b, h: (0, h)),
            pl.BlockSpec((seq, LANES), lambda b, h: (b, 0)),
            pl.BlockSpec((N_META, LANES), lambda b, h: (0, 0)),
            pl.BlockSpec((None, LANES, 2 * LANES), lambda b, h: (h, 0, 0)),
        ],
        out_specs=pl.BlockSpec((seq, HEAD_DIM), lambda b, h: (b, h)),
        out_shape=jax.ShapeDtypeStruct((r, D_MODEL), BF16),
        scratch_shapes=[
            pltpu.VMEM((seq, 2 * LANES), BF16),
            pltpu.VMEM((seq, LANES), BF16),
            pltpu.VMEM((seq, 2 * LANES), BF16),
            pltpu.VMEM((tk, tk), F32),
            pltpu.VMEM((2, tq, tk + LANES), F32),
            pltpu.VMEM((2, tq, LANES), F32),
            pltpu.VMEM((seq, LANES), F32),
            pltpu.VMEM((seq, 2 * LANES), F32),
        ],
        compiler_params=pltpu.CompilerParams(
            dimension_semantics=("arbitrary", "arbitrary"), vmem_limit_bytes=VMEM_LIMIT),
        name="fox_attention",
    )(proj, proj, proj, proj, k_meta, v_meta, gate, gate_meta, eqk)


def _out_kernel(aa_ref, ac_ref, ma_ref, mc_ref, x_ref, wa_ref, wc_ref, wo_ref, gain_ref, o_ref):
    ya = _dot(aa_ref[...], wa_ref[...])
    yc = _dot(ac_ref[...], wc_ref[...])
    merged = (jax.nn.sigmoid(ma_ref[...].astype(F32)) * ya
              + jax.nn.sigmoid(mc_ref[...].astype(F32)) * yc)
    y = x_ref[...] + _dot(merged.astype(BF16), wo_ref[...])
    o_ref[...] = _rmsnorm(y, gain_ref[...])


def _out_call(a_att, a_conv, proj, x2d, wa, wc, wo, gain):
    tm = TILES.out_tm
    r = x2d.shape[0]
    row = lambda i: (i, 0)
    act = pl.BlockSpec((tm, D_MODEL), row)
    gate = lambda name: pl.BlockSpec((None, tm, D_MODEL),
                                     lambda i, g=_PROJ_GROUPS.index(name): (g, i, 0))
    weight = pl.BlockSpec((D_MODEL, D_MODEL), lambda i: (0, 0), pipeline_mode=pl.Buffered(1))
    return pl.pallas_call(
        _out_kernel,
        grid=(r // tm,),
        in_specs=[act, act, gate("m_att"), gate("m_conv"), act, weight, weight, weight,
                  pl.BlockSpec((1, D_MODEL), lambda i: (0, 0))],
        out_specs=act,
        out_shape=jax.ShapeDtypeStruct((r, D_MODEL), F32),
        compiler_params=pltpu.CompilerParams(
            dimension_semantics=("arbitrary",), vmem_limit_bytes=VMEM_LIMIT),
        name="out_proj",
    )(a_att, a_conv, proj, proj, x2d, wa, wc, wo, gain)


def kernel(x, meta_tokens, norm_gain, w_in, b_f, conv_w, w_att_o, w_conv_o, w_out, final_gain):
    batch, seq, _ = x.shape
    assert x.shape == (batch, seq, D_MODEL) and x.dtype == F32
    assert seq % max(TILES.attn_tq, TILES.conv_tm, TILES.norm_tm) == 0
    assert meta_tokens.shape == (N_META, D_MODEL) and b_f.shape == (1, HEADS)
    assert norm_gain.shape[0] == 1, "single-layer block"
    assert w_in.shape == (1, D_MODEL, N_IN) and w_in.dtype == F32
    wt = jnp.transpose(w_in[0])
    bf = jnp.pad(b_f[0].astype(F32), (0, LANES - HEADS)).reshape(1, LANES)
    gain = norm_gain[0].reshape(1, D_MODEL).astype(F32)
    fgain = final_gain.reshape(1, D_MODEL).astype(F32)
    place, ones, eqk = _feature_constants()

    k_meta, v_meta, g_meta, cm, gate_meta = _meta_call(
        meta_tokens.astype(F32), gain, wt, bf, place, ones)
    h, gate = _norm_call(x, gain, wt, bf, cm, place, ones)
    proj = _proj_call(h, wt)
    a_conv, wa, wc, wo = _conv_call(h, wt, g_meta, conv_w[0].astype(F32),
                                    (w_att_o, w_conv_o, w_out), seq)
    a_att = _attn_call(proj, k_meta, v_meta, gate, gate_meta, eqk, batch, seq)
    out = _out_call(a_att, a_conv, proj, x.reshape(batch * seq, D_MODEL), wa, wc, wo, fgain)
    return out.reshape(batch, seq, D_MODEL)
```

```python
import functools
from typing import NamedTuple

import numpy as np

import jax
import jax.numpy as jnp
from jax import lax
from jax.experimental import pallas as pl
from jax.experimental.pallas import tpu as pltpu

D_MODEL = 2048
N_META = 16
HEADS = 16
HEAD_DIM = 128
CONV_K = 3
EPS = 1e-6
LANES = 128
SUBLANES = 8
NEG_BIG = -1e30
LOG2E = 1.4426950408889634
GATE_LANES = LANES // HEADS
SPLIT_TERMS = 3
VMEM_LIMIT = 56 * 1024 * 1024


class Tiles(NamedTuple):
    meta_tn: int = 512
    norm_tm: int = 1024
    norm_chunk: int = 256
    proj_tm: int = 2048
    proj_tn: int = 1024
    proj_rows: int = 512
    conv_tm: int = 2048
    conv_tc: int = 256
    conv_rows: int = 512
    attn_tq: int = 1024
    attn_tk: int = 512
    attn_ts: int = 512
    attn_rb: int = 128
    out_tm: int = 256


TILES = Tiles()

_GROUPS = ("q", "k", "v", "f", "z_att", "u", "gate_b", "gate_c", "z_conv", "m_att", "m_conv")
_WIDTHS = (D_MODEL, D_MODEL, D_MODEL, HEADS) + (D_MODEL,) * 7
COL = {name: sum(_WIDTHS[:n]) for n, name in enumerate(_GROUPS)}
N_IN = sum(_WIDTHS)

F32 = jnp.float32
BF16 = jnp.bfloat16


def _dot(a, b):
    return jnp.dot(a, b, preferred_element_type=F32)


def _dot_nt(a, b):
    return lax.dot_general(a, b, (((1,), (1,)), ((), ())), preferred_element_type=F32)


def _rmsnorm(x, gain):
    return x * lax.rsqrt(jnp.mean(x * x, axis=-1, keepdims=True) + EPS) * gain


def _log_sigmoid(x):
    return jnp.minimum(x, 0.0) - jnp.log1p(jnp.exp(-jnp.abs(x)))


def _silu(x):
    return x * jax.nn.sigmoid(x)


def _split3(x):
    hi = x.astype(BF16)
    r1 = x - hi.astype(F32)
    mid = r1.astype(BF16)
    lo = (r1 - mid.astype(F32)).astype(BF16)
    return hi, mid, lo


def _cumsum_rows(x):
    n = x.shape[0]
    tri = (lax.broadcasted_iota(jnp.int32, (n, n), 0)
           >= lax.broadcasted_iota(jnp.int32, (n, n), 1)).astype(BF16)
    hi, mid, lo = _split3(x)
    return _dot(tri, hi) + _dot(tri, mid) + _dot(tri, lo)


def _gate_features(c, place_ref, ones_ref):
    hi, mid, lo = _split3(c * LOG2E)
    cat = jnp.concatenate([hi, mid, lo], axis=1)
    return (_dot(cat, place_ref[...]) + ones_ref[...]).astype(BF16)


def _feature_constants():
    n = SPLIT_TERMS
    assert n + 1 <= GATE_LANES
    place = np.zeros((n * LANES, LANES), np.float32)
    ones = np.zeros((1, LANES), np.float32)
    eq = np.zeros((HEADS, LANES, LANES), np.float32)
    ek = np.zeros((HEADS, LANES, LANES), np.float32)
    for h in range(HEADS):
        base = GATE_LANES * h
        ones[0, base + n] = 1.0
        for f in range(n):
            place[f * LANES + h, base + f] = 1.0
            eq[h, base + f, f] = 1.0
            eq[h, base + n, n + f] = 1.0
            ek[h, base + n, f] = 1.0
            ek[h, base + f, n + f] = -1.0
    return (jnp.asarray(place, BF16), jnp.asarray(ones, F32),
            jnp.asarray(np.concatenate([eq, ek], axis=2), BF16))


def _w_tile_spec(tn, row_of):
    assert all(c % HEADS == 0 for c in COL.values()) and tn % HEADS == 0
    return pl.BlockSpec((pl.Element(tn), pl.Element(D_MODEL)),
                        lambda *ids: (pl.multiple_of(row_of(*ids), HEADS), 0))


def _forget_w_spec():
    assert COL["f"] % LANES == 0
    return pl.BlockSpec((LANES, D_MODEL), lambda *ids: (COL["f"] // LANES, 0))


def _stage_weight(src_ref, dst_ref, chunk=128):
    for r in range(0, src_ref.shape[0], chunk):
        dst_ref[r:r + chunk, :] = src_ref[r:r + chunk, :].astype(BF16)


def _meta_kernel(meta_ref, gain_ref, wk_ref, wv_ref, wu_ref, wgc_ref, wf_ref, bf_ref,
                 place_ref, ones_ref, k_ref, v_ref, g_ref, cm_ref, gm_ref):
    h = _rmsnorm(meta_ref[...], gain_ref[...]).astype(BF16)
    proj = lambda w_ref: _dot_nt(h, w_ref[...].astype(BF16))
    k_ref[...] = proj(wk_ref).astype(BF16)
    v_ref[...] = proj(wv_ref).astype(BF16)
    g_ref[...] = proj(wgc_ref) * proj(wu_ref)
    cm = _cumsum_rows(_log_sigmoid(proj(wf_ref) + bf_ref[...]))
    cm_ref[...] = cm
    gm_ref[...] = _gate_features(cm, place_ref, ones_ref)


def _meta_call(meta, gain, wt, bf, place, ones):
    tn = TILES.meta_tn
    nt = D_MODEL // tn
    const = lambda shape: pl.BlockSpec(shape, lambda j: (0,) * len(shape))
    group = lambda name: _w_tile_spec(tn, lambda j: COL[name] + j * tn)
    return pl.pallas_call(
        _meta_kernel,
        grid=(nt,),
        in_specs=[
            const((N_META, D_MODEL)), const((1, D_MODEL)),
            group("k"), group("v"), group("u"), group("gate_c"), _forget_w_spec(),
            const((1, LANES)), const((SPLIT_TERMS * LANES, LANES)), const((1, LANES)),
        ],
        out_specs=[
            pl.BlockSpec((N_META, tn), lambda j: (0, j)),
            pl.BlockSpec((N_META, tn), lambda j: (0, j)),
            pl.BlockSpec((N_META, tn), lambda j: (0, j)),
            const((N_META, LANES)),
            const((N_META, LANES)),
        ],
        out_shape=[
            jax.ShapeDtypeStruct((N_META, D_MODEL), BF16),
            jax.ShapeDtypeStruct((N_META, D_MODEL), BF16),
            jax.ShapeDtypeStruct((N_META, D_MODEL), F32),
            jax.ShapeDtypeStruct((N_META, LANES), F32),
            jax.ShapeDtypeStruct((N_META, LANES), BF16),
        ],
        compiler_params=pltpu.CompilerParams(
            dimension_semantics=("arbitrary",), vmem_limit_bytes=VMEM_LIMIT),
        name="meta_proj",
    )(meta, gain, wt, wt, wt, wt, wt, bf, place, ones)


def _norm_kernel(x_ref, gain_ref, wf_ref, bf_ref, cm_ref, place_ref, ones_ref,
                 h_ref, gate_ref, carry_ref, *, chunk):
    @pl.when(pl.program_id(1) == 0)
    def _():
        carry_ref[...] = cm_ref[N_META - 1:N_META, :]

    h = _rmsnorm(x_ref[...], gain_ref[...]).astype(BF16)
    h_ref[...] = h
    logf = _log_sigmoid(_dot_nt(h, wf_ref[...].astype(BF16)) + bf_ref[...])
    tm = logf.shape[0]
    run = carry_ref[...]
    parts = []
    for j in range(tm // chunk):
        c = _cumsum_rows(logf[j * chunk:(j + 1) * chunk]) + run
        run = c[chunk - 1:chunk, :]
        parts.append(c)
    carry_ref[...] = run
    gate_ref[...] = _gate_features(jnp.concatenate(parts, axis=0), place_ref, ones_ref)


def _norm_call(x, gain, wt, bf, cm, place, ones):
    tm, chunk = TILES.norm_tm, TILES.norm_chunk
    b, s, _ = x.shape
    nt = s // tm
    const = lambda shape: pl.BlockSpec(shape, lambda bi, i: (0,) * len(shape))
    return pl.pallas_call(
        functools.partial(_norm_kernel, chunk=chunk),
        grid=(b, nt),
        in_specs=[
            pl.BlockSpec((None, tm, D_MODEL), lambda bi, i: (bi, i, 0)),
            const((1, D_MODEL)),
            _forget_w_spec(),
            const((1, LANES)),
            const((N_META, LANES)),
            const((SPLIT_TERMS * LANES, LANES)),
            const((1, LANES)),
        ],
        out_specs=[
            pl.BlockSpec((tm, D_MODEL), lambda bi, i: (bi * nt + i, 0)),
            pl.BlockSpec((tm, LANES), lambda bi, i: (bi * nt + i, 0)),
        ],
        out_shape=[
            jax.ShapeDtypeStruct((b * s, D_MODEL), BF16),
            jax.ShapeDtypeStruct((b * s, LANES), BF16),
        ],
        scratch_shapes=[pltpu.VMEM((1, LANES), F32)],
        compiler_params=pltpu.CompilerParams(
            dimension_semantics=("arbitrary", "arbitrary"), vmem_limit_bytes=VMEM_LIMIT),
        name="norm_forget",
    )(x, gain, wt, bf, cm, place, ones)


_PROJ_GROUPS = ("q", "k", "v", "z_att", "m_att", "m_conv")


def _proj_group_row(g):
    row = jnp.int32(COL[_PROJ_GROUPS[0]])
    for n in range(1, len(_PROJ_GROUPS)):
        row = jnp.where(g == n, COL[_PROJ_GROUPS[n]], row)
    return row


def _proj_kernel(h_ref, w_ref, o_ref, w_s, *, row_chunk):
    @pl.when(pl.program_id(2) == 0)
    def _():
        for r in range(0, w_ref.shape[0], LANES):
            w_s[:, r:r + LANES] = w_ref[r:r + LANES, :].T.astype(BF16)

    scale = jnp.where(pl.program_id(0) == 0, HEAD_DIM ** -0.5 * LOG2E, 1.0).astype(F32)
    for r in range(0, h_ref.shape[0], row_chunk):
        y = _dot(h_ref[r:r + row_chunk, :], w_s[...])
        o_ref[r:r + row_chunk, :] = (y * scale).astype(BF16)


def _proj_call(h, wt):
    tm, tn, row_chunk = TILES.proj_tm, TILES.proj_tn, TILES.proj_rows
    r = h.shape[0]
    ng = len(_PROJ_GROUPS)
    return pl.pallas_call(
        functools.partial(_proj_kernel, row_chunk=row_chunk),
        grid=(ng, D_MODEL // tn, r // tm),
        in_specs=[
            pl.BlockSpec((tm, D_MODEL), lambda g, j, i: (i, 0)),
            _w_tile_spec(tn, lambda g, j, i: _proj_group_row(g) + j * tn),
        ],
        out_specs=pl.BlockSpec((None, tm, tn), lambda g, j, i: (g, i, j)),
        out_shape=jax.ShapeDtypeStruct((ng, r, D_MODEL), BF16),
        scratch_shapes=[pltpu.VMEM((D_MODEL, tn), BF16)],
        compiler_params=pltpu.CompilerParams(
            dimension_semantics=("arbitrary", "arbitrary", "arbitrary"),
            vmem_limit_bytes=VMEM_LIMIT),
        name="plain_proj",
    )(h, wt)


_CONV_GROUPS = ("u", "gate_b", "gate_c", "z_conv")


def _conv_kernel(h_ref, wu_ref, wgb_ref, wgc_ref, wzc_ref, gm_ref, cw_ref,
                 wa_ref, wc_ref, wo_ref, o_ref, wa_out, wc_out, wo_out,
                 w_s, gbuf_ref, *, tiles_per_batch, row_chunk):
    i = pl.program_id(1)
    tm = h_ref.shape[0]
    for src, dst in ((wa_ref, wa_out), (wc_ref, wc_out), (wo_ref, wo_out)):
        dst[...] = src[...].astype(BF16)

    @pl.when(i == 0)
    def _():
        for n, w_ref in enumerate((wu_ref, wgb_ref, wgc_ref, wzc_ref)):
            _stage_weight(w_ref, w_s.at[n])

    @pl.when(i % tiles_per_batch == 0)
    def _():
        gbuf_ref[0:SUBLANES, :] = gm_ref[N_META - SUBLANES:N_META, :]

    for r in range(0, tm, row_chunk):
        h = h_ref[r:r + row_chunk, :]
        g = _dot_nt(h, w_s[2]) * _dot_nt(h, w_s[0])
        first = SUBLANES + r
        gbuf_ref[first:first + row_chunk, :] = g
        conv = (gbuf_ref[first - 2:first - 2 + row_chunk, :] * cw_ref[0:1, :]
                + gbuf_ref[first - 1:first - 1 + row_chunk, :] * cw_ref[1:2, :]
                + g * cw_ref[2:3, :])
        gb = _dot_nt(h, w_s[1])
        zc = _dot_nt(h, w_s[3])
        o_ref[r:r + row_chunk, :] = (gb * conv * _silu(zc)).astype(BF16)
    gbuf_ref[0:SUBLANES, :] = gbuf_ref[tm:tm + SUBLANES, :]


def _conv_call(h, wt, g_meta, conv_w, out_weights, rows_per_batch):
    tm, tc, row_chunk = TILES.conv_tm, TILES.conv_tc, TILES.conv_rows
    r = h.shape[0]
    nrow = r // tm
    steps = (D_MODEL // tc) * nrow
    slab = D_MODEL // steps
    assert slab * steps == D_MODEL and slab % (2 * SUBLANES) == 0
    w_specs = [_w_tile_spec(tc, lambda j, i, name=name: COL[name] + j * tc)
               for name in _CONV_GROUPS]
    slab_in = pl.BlockSpec((None, slab, D_MODEL), lambda j, i: (0, j * nrow + i, 0))
    slab_out = pl.BlockSpec((slab, D_MODEL), lambda j, i: (j * nrow + i, 0))
    cast_shape = jax.ShapeDtypeStruct((D_MODEL, D_MODEL), BF16)
    return pl.pallas_call(
        functools.partial(_conv_kernel, tiles_per_batch=rows_per_batch // tm, row_chunk=row_chunk),
        grid=(D_MODEL // tc, nrow),
        in_specs=[pl.BlockSpec((tm, D_MODEL), lambda j, i: (i, 0))] + w_specs + [
            pl.BlockSpec((N_META, tc), lambda j, i: (0, j)),
            pl.BlockSpec((CONV_K, tc), lambda j, i: (0, j)),
            slab_in, slab_in, slab_in,
        ],
        out_specs=[pl.BlockSpec((tm, tc), lambda j, i: (i, j)), slab_out, slab_out, slab_out],
        out_shape=[jax.ShapeDtypeStruct((r, D_MODEL), BF16), cast_shape, cast_shape, cast_shape],
        scratch_shapes=[pltpu.VMEM((len(_CONV_GROUPS), tc, D_MODEL), BF16),
                        pltpu.VMEM((SUBLANES + tm, tc), F32)],
        compiler_params=pltpu.CompilerParams(
            dimension_semantics=("arbitrary", "arbitrary"), vmem_limit_bytes=VMEM_LIMIT),
        name="conv_proj",
    )(h, *([wt] * len(_CONV_GROUPS)), g_meta, conv_w, *out_weights)


def _attn_kernel(q_ref, k_ref, v_ref, z_ref, km_ref, vm_ref, g_ref, gm_ref, eqk_ref,
                 o_ref, qa_ref, kf_ref, va_ref, tri_ref, s_ref, tmax_ref, m_ref, acc_ref,
                 *, tq, tk, ts, rb):
    seq = q_ref.shape[0]
    nq = seq // tq
    assert tq == 2 * tk and seq % tq == 0
    half = tk // 2

    tri_ref[...] = jnp.where(
        lax.broadcasted_iota(jnp.int32, (tk, tk), 1) <= lax.broadcasted_iota(jnp.int32, (tk, tk), 0),
        0.0, NEG_BIG)
    kma = jnp.concatenate(
        [km_ref[...], _dot(gm_ref[...], eqk_ref[:, LANES:]).astype(BF16)], axis=1)
    pad_lane = lax.broadcasted_iota(jnp.int32, (LANES - N_META, 2 * LANES), 1)
    k_meta_blk = jnp.concatenate(
        [kma, jnp.where(pad_lane == LANES + SPLIT_TERMS, NEG_BIG, 0.0).astype(BF16)],
        axis=0)
    v_meta_blk = jnp.concatenate(
        [vm_ref[...], jnp.ones((N_META, LANES), BF16)], axis=1)
    v_meta_blk = jnp.concatenate(
        [v_meta_blk, jnp.zeros((LANES - N_META, 2 * LANES), BF16)], axis=0)

    def setup(row0):
        rows = slice(row0, row0 + ts)
        feat = _dot(g_ref[rows, :], eqk_ref[...]).astype(BF16)
        kf_ref[rows, :] = feat[:, LANES:]
        qa_ref[rows, :] = jnp.concatenate([q_ref[rows, :], feat[:, :LANES]], axis=1)
        va_ref[rows, :] = jnp.concatenate([v_ref[rows, :], jnp.ones((ts, LANES), BF16)], axis=1)
        m_ref[rows, :] = jnp.full((ts, LANES), NEG_BIG, F32)
        acc_ref[rows, :] = jnp.zeros((ts, 2 * LANES), F32)

    def width(key0, nkeys):
        return nkeys + LANES if key0 == 0 else nkeys

    def stage_a(row0, nrows, key0, nkeys, tri, slot):
        keys = slice(key0, key0 + nkeys)
        ka = jnp.concatenate([k_ref[keys, :], kf_ref[keys, :]], axis=1)
        if key0 == 0:
            ka = jnp.concatenate([ka, k_meta_blk], axis=0)
        s = _dot_nt(qa_ref[row0:row0 + nrows, :], ka)
        w = width(key0, nkeys)
        for r0 in range(0, nrows, rb):
            part = s[r0:r0 + rb]
            if tri:
                masked = part[:, 0:nkeys] + tri_ref[r0:r0 + rb, 0:nkeys]
                part = jnp.concatenate([masked, part[:, nkeys:]], axis=1) if w > nkeys else masked
            s_ref[slot, r0:r0 + rb, 0:w] = part
            t = part[:, 0:LANES]
            for c in range(1, w // LANES):
                t = jnp.maximum(t, part[:, c * LANES:(c + 1) * LANES])
            tmax_ref[slot, r0:r0 + rb, :] = jnp.broadcast_to(
                jnp.max(t, axis=1, keepdims=True), (rb, LANES))

    def stage_b(row0, nrows, key0, nkeys, tri, slot):
        del tri
        w = width(key0, nkeys)
        alphas, ps = [], []
        for r0 in range(0, nrows, rb):
            rows = slice(row0 + r0, row0 + r0 + rb)
            m_old = m_ref[rows, :]
            m_new = jnp.maximum(m_old, tmax_ref[slot, r0:r0 + rb, :])
            m_ref[rows, :] = m_new
            alphas.append(jnp.exp2(m_old - m_new))
            ps.append(jnp.concatenate(
                [jnp.exp2(s_ref[slot, r0:r0 + rb, c * LANES:(c + 1) * LANES] - m_new).astype(BF16)
                 for c in range(w // LANES)], axis=1))
        va = va_ref[key0:key0 + nkeys, :]
        if key0 == 0:
            va = jnp.concatenate([va, v_meta_blk], axis=0)
        pv = _dot(jnp.concatenate(ps, axis=0), va)
        for n, r0 in enumerate(range(0, nrows, rb)):
            rows = slice(row0 + r0, row0 + r0 + rb)
            alpha = jnp.concatenate([alphas[n], alphas[n]], axis=1)
            acc_ref[rows, :] = alpha * acc_ref[rows, :] + pv[r0:r0 + rb]

    def diagonal(row0):
        return [(row0, half, row0, half, True),
                (row0 + half, half, row0, half, False),
                (row0 + half, half, row0 + half, half, True)]

    items = []
    for i in range(nq):
        top = i * tq
        items += [(top, tq, j * tk, tk, False) for j in range(2 * i)]
        items += [(top + tk, tk, top, tk, False)]
        items += diagonal(top) + diagonal(top + tk)

    def finalize(row0, nrows):
        for r0 in range(row0, row0 + nrows, rb):
            o = (acc_ref[r0:r0 + rb, :LANES] / acc_ref[r0:r0 + rb, LANES:]
                 * _silu(z_ref[r0:r0 + rb, :].astype(F32)))
            o_ref[r0:r0 + rb, :] = o.astype(BF16)

    rows_ready = [0]

    def scores(item, slot):
        row0, nrows, key0, nkeys, _ = item
        while rows_ready[0] < max(row0 + nrows, key0 + nkeys):
            setup(rows_ready[0])
            rows_ready[0] += ts
        stage_a(*item, slot)

    scores(items[0], 0)
    for n, item in enumerate(items):
        if n + 1 < len(items):
            scores(items[n + 1], 1 - n % 2)
        stage_b(*item, n % 2)
        if item[4]:
            finalize(item[0], item[1])


def _attn_call(proj, k_meta, v_meta, gate, gate_meta, eqk, batch, seq):
    tq, tk, ts, rb = TILES.attn_tq, TILES.attn_tk, TILES.attn_ts, TILES.attn_rb
    r = batch * seq
    head = lambda g: pl.BlockSpec((None, seq, HEAD_DIM), lambda b, h, g=g: (g, b, h))
    return pl.pallas_call(
        functools.partial(_attn_kernel, tq=tq, tk=tk, ts=ts, rb=rb),
        grid=(batch, HEADS),
        in_specs=[
            head(0), head(1), head(2), head(3),
            pl.BlockSpec((N_META, HEAD_DIM), lambda b, h: (0, h)),
            pl.BlockSpec((N_META, HEAD_DIM), lambda b, h: (0, h)),
            pl.BlockSpec((seq, LANES), lambda b, h: (b, 0)),
            pl.BlockSpec((N_META, LANES), lambda b, h: (0, 0)),
            pl.BlockSpec((None, LANES, 2 * LANES), lambda b, h: (h, 0, 0)),
        ],
        out_specs=pl.BlockSpec((seq, HEAD_DIM), lambda b, h: (b, h)),
        out_shape=jax.ShapeDtypeStruct((r, D_MODEL), BF16),
        scratch_shapes=[
            pltpu.VMEM((seq, 2 * LANES), BF16),
            pltpu.VMEM((seq, LANES), BF16),
            pltpu.VMEM((seq, 2 * LANES), BF16),
            pltpu.VMEM((tk, tk), F32),
            pltpu.VMEM((2, tq, tk + LANES), F32),
            pltpu.VMEM((2, tq, LANES), F32),
            pltpu.VMEM((seq, LANES), F32),
            pltpu.VMEM((seq, 2 * LANES), F32),
        ],
        compiler_params=pltpu.CompilerParams(
            dimension_semantics=("arbitrary", "arbitrary"), vmem_limit_bytes=VMEM_LIMIT),
        name="fox_attention",
    )(proj, proj, proj, proj, k_meta, v_meta, gate, gate_meta, eqk)


def _out_kernel(aa_ref, ac_ref, ma_ref, mc_ref, x_ref, wa_ref, wc_ref, wo_ref, gain_ref, o_ref):
    ya = _dot(aa_ref[...], wa_ref[...])
    yc = _dot(ac_ref[...], wc_ref[...])
    merged = (jax.nn.sigmoid(ma_ref[...].astype(F32)) * ya
              + jax.nn.sigmoid(mc_ref[...].astype(F32)) * yc)
    y = x_ref[...] + _dot(merged.astype(BF16), wo_ref[...])
    o_ref[...] = _rmsnorm(y, gain_ref[...])


def _out_call(a_att, a_conv, proj, x2d, wa, wc, wo, gain):
    tm = TILES.out_tm
    r = x2d.shape[0]
    row = lambda i: (i, 0)
    act = pl.BlockSpec((tm, D_MODEL), row)
    gate = lambda name: pl.BlockSpec((None, tm, D_MODEL),
                                     lambda i, g=_PROJ_GROUPS.index(name): (g, i, 0))
    weight = pl.BlockSpec((D_MODEL, D_MODEL), lambda i: (0, 0), pipeline_mode=pl.Buffered(1))
    return pl.pallas_call(
        _out_kernel,
        grid=(r // tm,),
        in_specs=[act, act, gate("m_att"), gate("m_conv"), act, weight, weight, weight,
                  pl.BlockSpec((1, D_MODEL), lambda i: (0, 0))],
        out_specs=act,
        out_shape=jax.ShapeDtypeStruct((r, D_MODEL), F32),
        compiler_params=pltpu.CompilerParams(
            dimension_semantics=("arbitrary",), vmem_limit_bytes=VMEM_LIMIT),
        name="out_proj",
    )(a_att, a_conv, proj, proj, x2d, wa, wc, wo, gain)


def kernel(x, meta_tokens, norm_gain, w_in, b_f, conv_w, w_att_o, w_conv_o, w_out, final_gain):
    batch, seq, _ = x.shape
    assert x.shape == (batch, seq, D_MODEL) and x.dtype == F32
    assert seq % max(TILES.attn_tq, TILES.conv_tm, TILES.norm_tm) == 0
    assert meta_tokens.shape == (N_META, D_MODEL) and b_f.shape == (1, HEADS)
    assert norm_gain.shape[0] == 1, "single-layer block"
    assert w_in.shape == (1, D_MODEL, N_IN) and w_in.dtype == F32
    wt = jnp.transpose(w_in[0])
    bf = jnp.pad(b_f[0].astype(F32), (0, LANES - HEADS)).reshape(1, LANES)
    gain = norm_gain[0].reshape(1, D_MODEL).astype(F32)
    fgain = final_gain.reshape(1, D_MODEL).astype(F32)
    place, ones, eqk = _feature_constants()

    k_meta, v_meta, g_meta, cm, gate_meta = _meta_call(
        meta_tokens.astype(F32), gain, wt, bf, place, ones)
    h, gate = _norm_call(x, gain, wt, bf, cm, place, ones)
    proj = _proj_call(h, wt)
    a_conv, wa, wc, wo = _conv_call(h, wt, g_meta, conv_w[0].astype(F32),
                                    (w_att_o, w_conv_o, w_out), seq)
    a_att = _attn_call(proj, k_meta, v_meta, gate, gate_meta, eqk, batch, seq)
    out = _out_call(a_att, a_conv, proj, x.reshape(batch * seq, D_MODEL), wa, wc, wo, fgain)
    return out.reshape(batch, seq, D_MODEL)
```

```python
import functools
from typing import NamedTuple

import numpy as np

import jax
import jax.numpy as jnp
from jax import lax
from jax.experimental import pallas as pl
from jax.experimental.pallas import tpu as pltpu

D_MODEL = 2048
N_META = 16
HEADS = 16
HEAD_DIM = 128
CONV_K = 3
EPS = 1e-6
LANES = 128
SUBLANES = 8
NEG_BIG = -1e30
LOG2E = 1.4426950408889634
GATE_LANES = LANES // HEADS
SPLIT_TERMS = 3
VMEM_LIMIT = 56 * 1024 * 1024


class Tiles(NamedTuple):
    meta_tn: int = 512
    norm_tm: int = 1024
    norm_chunk: int = 256
    proj_tm: int = 1024
    proj_rows: int = 512
    conv_tm: int = 2048
    conv_tc: int = 256
    conv_rows: int = 512
    attn_tq: int = 1024
    attn_tk: int = 512
    attn_ts: int = 512
    attn_rb: int = 128
    out_tm: int = 256


TILES = Tiles()

_GROUPS = ("q", "k", "v", "f", "z_att", "u", "gate_b", "gate_c", "z_conv", "m_att", "m_conv")
_WIDTHS = (D_MODEL, D_MODEL, D_MODEL, HEADS) + (D_MODEL,) * 7
COL = {name: sum(_WIDTHS[:n]) for n, name in enumerate(_GROUPS)}
N_IN = sum(_WIDTHS)

F32 = jnp.float32
BF16 = jnp.bfloat16


def _dot(a, b):
    return jnp.dot(a, b, preferred_element_type=F32)


def _dot_nt(a, b):
    return lax.dot_general(a, b, (((1,), (1,)), ((), ())), preferred_element_type=F32)


def _rmsnorm(x, gain):
    return x * lax.rsqrt(jnp.mean(x * x, axis=-1, keepdims=True) + EPS) * gain


def _log_sigmoid(x):
    return jnp.minimum(x, 0.0) - jnp.log1p(jnp.exp(-jnp.abs(x)))


def _silu(x):
    return x * jax.nn.sigmoid(x)


def _split3(x):
    hi = x.astype(BF16)
    r1 = x - hi.astype(F32)
    mid = r1.astype(BF16)
    lo = (r1 - mid.astype(F32)).astype(BF16)
    return hi, mid, lo


def _cumsum_rows(x):
    n = x.shape[0]
    tri = (lax.broadcasted_iota(jnp.int32, (n, n), 0)
           >= lax.broadcasted_iota(jnp.int32, (n, n), 1)).astype(BF16)
    hi, mid, lo = _split3(x)
    return _dot(tri, hi) + _dot(tri, mid) + _dot(tri, lo)


def _gate_features(c, place_ref, ones_ref):
    hi, mid, lo = _split3(c * LOG2E)
    cat = jnp.concatenate([hi, mid, lo], axis=1)
    return (_dot(cat, place_ref[...]) + ones_ref[...]).astype(BF16)


def _feature_constants():
    n = SPLIT_TERMS
    assert n + 1 <= GATE_LANES
    place = np.zeros((n * LANES, LANES), np.float32)
    ones = np.zeros((1, LANES), np.float32)
    eq = np.zeros((HEADS, LANES, LANES), np.float32)
    ek = np.zeros((HEADS, LANES, LANES), np.float32)
    for h in range(HEADS):
        base = GATE_LANES * h
        ones[0, base + n] = 1.0
        for f in range(n):
            place[f * LANES + h, base + f] = 1.0
            eq[h, base + f, f] = 1.0
            eq[h, base + n, n + f] = 1.0
            ek[h, base + n, f] = 1.0
            ek[h, base + f, n + f] = -1.0
    return (jnp.asarray(place, BF16), jnp.asarray(ones, F32),
            jnp.asarray(np.concatenate([eq, ek], axis=2), BF16))


def _w_tile_spec(tn, row_of):
    assert all(c % HEADS == 0 for c in COL.values()) and tn % HEADS == 0
    return pl.BlockSpec((pl.Element(tn), pl.Element(D_MODEL)),
                        lambda *ids: (pl.multiple_of(row_of(*ids), HEADS), 0))


def _forget_w_spec():
    assert COL["f"] % LANES == 0
    return pl.BlockSpec((LANES, D_MODEL), lambda *ids: (COL["f"] // LANES, 0))


def _stage_weight(src_ref, dst_ref, chunk=128):
    for r in range(0, src_ref.shape[0], chunk):
        dst_ref[r:r + chunk, :] = src_ref[r:r + chunk, :].astype(BF16)


def _meta_kernel(meta_ref, gain_ref, wk_ref, wv_ref, wu_ref, wgc_ref, wf_ref, bf_ref,
                 place_ref, ones_ref, k_ref, v_ref, g_ref, cm_ref, gm_ref):
    h = _rmsnorm(meta_ref[...], gain_ref[...]).astype(BF16)
    proj = lambda w_ref: _dot_nt(h, w_ref[...].astype(BF16))
    k_ref[...] = proj(wk_ref).astype(BF16)
    v_ref[...] = proj(wv_ref).astype(BF16)
    g_ref[...] = proj(wgc_ref) * proj(wu_ref)
    cm = _cumsum_rows(_log_sigmoid(proj(wf_ref) + bf_ref[...]))
    cm_ref[...] = cm
    gm_ref[...] = _gate_features(cm, place_ref, ones_ref)


def _meta_call(meta, gain, wt, bf, place, ones):
    tn = TILES.meta_tn
    nt = D_MODEL // tn
    const = lambda shape: pl.BlockSpec(shape, lambda j: (0,) * len(shape))
    group = lambda name: _w_tile_spec(tn, lambda j: COL[name] + j * tn)
    return pl.pallas_call(
        _meta_kernel,
        grid=(nt,),
        in_specs=[
            const((N_META, D_MODEL)), const((1, D_MODEL)),
            group("k"), group("v"), group("u"), group("gate_c"), _forget_w_spec(),
            const((1, LANES)), const((SPLIT_TERMS * LANES, LANES)), const((1, LANES)),
        ],
        out_specs=[
            pl.BlockSpec((N_META, tn), lambda j: (0, j)),
            pl.BlockSpec((N_META, tn), lambda j: (0, j)),
            pl.BlockSpec((N_META, tn), lambda j: (0, j)),
            const((N_META, LANES)),
            const((N_META, LANES)),
        ],
        out_shape=[
            jax.ShapeDtypeStruct((N_META, D_MODEL), BF16),
            jax.ShapeDtypeStruct((N_META, D_MODEL), BF16),
            jax.ShapeDtypeStruct((N_META, D_MODEL), F32),
            jax.ShapeDtypeStruct((N_META, LANES), F32),
            jax.ShapeDtypeStruct((N_META, LANES), BF16),
        ],
        compiler_params=pltpu.CompilerParams(
            dimension_semantics=("arbitrary",), vmem_limit_bytes=VMEM_LIMIT),
        name="meta_proj",
    )(meta, gain, wt, wt, wt, wt, wt, bf, place, ones)


def _norm_kernel(x_ref, gain_ref, wf_ref, bf_ref, cm_ref, place_ref, ones_ref,
                 h_ref, gate_ref, carry_ref, *, chunk):
    @pl.when(pl.program_id(1) == 0)
    def _():
        carry_ref[...] = cm_ref[N_META - 1:N_META, :]

    h = _rmsnorm(x_ref[...], gain_ref[...]).astype(BF16)
    h_ref[...] = h
    logf = _log_sigmoid(_dot_nt(h, wf_ref[...].astype(BF16)) + bf_ref[...])
    tm = logf.shape[0]
    run = carry_ref[...]
    parts = []
    for j in range(tm // chunk):
        c = _cumsum_rows(logf[j * chunk:(j + 1) * chunk]) + run
        run = c[chunk - 1:chunk, :]
        parts.append(c)
    carry_ref[...] = run
    gate_ref[...] = _gate_features(jnp.concatenate(parts, axis=0), place_ref, ones_ref)


def _norm_call(x, gain, wt, bf, cm, place, ones):
    tm, chunk = TILES.norm_tm, TILES.norm_chunk
    b, s, _ = x.shape
    nt = s // tm
    const = lambda shape: pl.BlockSpec(shape, lambda bi, i: (0,) * len(shape))
    return pl.pallas_call(
        functools.partial(_norm_kernel, chunk=chunk),
        grid=(b, nt),
        in_specs=[
            pl.BlockSpec((None, tm, D_MODEL), lambda bi, i: (bi, i, 0)),
            const((1, D_MODEL)),
            _forget_w_spec(),
            const((1, LANES)),
            const((N_META, LANES)),
            const((SPLIT_TERMS * LANES, LANES)),
            const((1, LANES)),
        ],
        out_specs=[
            pl.BlockSpec((tm, D_MODEL), lambda bi, i: (bi * nt + i, 0)),
            pl.BlockSpec((tm, LANES), lambda bi, i: (bi * nt + i, 0)),
        ],
        out_shape=[
            jax.ShapeDtypeStruct((b * s, D_MODEL), BF16),
            jax.ShapeDtypeStruct((b * s, LANES), BF16),
        ],
        scratch_shapes=[pltpu.VMEM((1, LANES), F32)],
        compiler_params=pltpu.CompilerParams(
            dimension_semantics=("arbitrary", "arbitrary"), vmem_limit_bytes=VMEM_LIMIT),
        name="norm_forget",
    )(x, gain, wt, bf, cm, place, ones)


_PROJ_GROUPS = ("q", "k", "v", "z_att", "m_att", "m_conv")


def _proj_group_row(g):
    row = jnp.int32(COL[_PROJ_GROUPS[0]])
    for n in range(1, len(_PROJ_GROUPS)):
        row = jnp.where(g == n, COL[_PROJ_GROUPS[n]], row)
    return row


def _proj_kernel(h_ref, wt_hbm, o_ref, w_land, w_even, w_odd, sem, *, row_chunk):
    g = pl.program_id(0)
    i = pl.program_id(1)
    n_groups = pl.num_programs(0)
    last_row_step = pl.num_programs(1) - 1

    def fetch(group):
        rows = pl.ds(pl.multiple_of(_proj_group_row(group), HEADS), D_MODEL)
        return pltpu.make_async_copy(wt_hbm.at[rows, :], w_land, sem)

    @pl.when((g == 0) & (i == 0))
    def _():
        fetch(0).start()
        fetch(0).wait()
        _stage_weight(w_land, w_even)

    @pl.when((i == 0) & (g + 1 < n_groups))
    def _():
        fetch(g + 1).start()

    scale = jnp.where(g == 0, HEAD_DIM ** -0.5 * LOG2E, 1.0).astype(F32)

    def multiply(w_cur, w_next=None):
        n_dots = h_ref.shape[0] // row_chunk
        share = D_MODEL // n_dots
        for n in range(n_dots):
            r = n * row_chunk
            y = _dot_nt(h_ref[r:r + row_chunk, :], w_cur[...])
            if w_next is not None:
                rows = slice(n * share, (n + 1) * share)
                _stage_weight(w_land.at[rows, :], w_next.at[rows, :])
            o_ref[r:r + row_chunk, :] = (y * scale).astype(BF16)

    hand_over = (i == last_row_step) & (g + 1 < n_groups)
    for parity, (w_cur, w_next) in enumerate(((w_even, w_odd), (w_odd, w_even))):
        mine = lax.rem(g, 2) == parity

        @pl.when(mine & jnp.logical_not(hand_over))
        def _(w_cur=w_cur):
            multiply(w_cur)

        @pl.when(mine & hand_over)
        def _(w_cur=w_cur, w_next=w_next):
            fetch(g + 1).wait()
            multiply(w_cur, w_next)


def _proj_call(h, wt):
    tm, row_chunk = TILES.proj_tm, TILES.proj_rows
    r = h.shape[0]
    ng = len(_PROJ_GROUPS)
    return pl.pallas_call(
        functools.partial(_proj_kernel, row_chunk=row_chunk),
        grid=(ng, r // tm),
        in_specs=[
            pl.BlockSpec((tm, D_MODEL), lambda g, i: (i, 0)),
            pl.BlockSpec(memory_space=pl.ANY),
        ],
        out_specs=pl.BlockSpec((None, tm, D_MODEL), lambda g, i: (g, i, 0)),
        out_shape=jax.ShapeDtypeStruct((ng, r, D_MODEL), BF16),
        scratch_shapes=[pltpu.VMEM((D_MODEL, D_MODEL), F32),
                        pltpu.VMEM((D_MODEL, D_MODEL), BF16),
                        pltpu.VMEM((D_MODEL, D_MODEL), BF16),
                        pltpu.SemaphoreType.DMA(())],
        compiler_params=pltpu.CompilerParams(
            dimension_semantics=("arbitrary", "arbitrary"),
            vmem_limit_bytes=VMEM_LIMIT),
        name="plain_proj",
    )(h, wt)


_CONV_GROUPS = ("u", "gate_b", "gate_c", "z_conv")


def _conv_kernel(h_ref, wu_ref, wgb_ref, wgc_ref, wzc_ref, gm_ref, cw_ref,
                 wa_ref, wc_ref, wo_ref, o_ref, wa_out, wc_out, wo_out,
                 w_s, gbuf_ref, *, tiles_per_batch, row_chunk):
    i = pl.program_id(1)
    tm = h_ref.shape[0]
    for src, dst in ((wa_ref, wa_out), (wc_ref, wc_out), (wo_ref, wo_out)):
        dst[...] = src[...].astype(BF16)

    @pl.when(i == 0)
    def _():
        for n, w_ref in enumerate((wu_ref, wgb_ref, wgc_ref, wzc_ref)):
            _stage_weight(w_ref, w_s.at[n])

    @pl.when(i % tiles_per_batch == 0)
    def _():
        gbuf_ref[0:SUBLANES, :] = gm_ref[N_META - SUBLANES:N_META, :]

    for r in range(0, tm, row_chunk):
        h = h_ref[r:r + row_chunk, :]
        g = _dot_nt(h, w_s[2]) * _dot_nt(h, w_s[0])
        first = SUBLANES + r
        gbuf_ref[first:first + row_chunk, :] = g
        conv = (gbuf_ref[first - 2:first - 2 + row_chunk, :] * cw_ref[0:1, :]
                + gbuf_ref[first - 1:first - 1 + row_chunk, :] * cw_ref[1:2, :]
                + g * cw_ref[2:3, :])
        gb = _dot_nt(h, w_s[1])
        zc = _dot_nt(h, w_s[3])
        o_ref[r:r + row_chunk, :] = (gb * conv * _silu(zc)).astype(BF16)
    gbuf_ref[0:SUBLANES, :] = gbuf_ref[tm:tm + SUBLANES, :]


def _conv_call(h, wt, g_meta, conv_w, out_weights, rows_per_batch):
    tm, tc, row_chunk = TILES.conv_tm, TILES.conv_tc, TILES.conv_rows
    r = h.shape[0]
    nrow = r // tm
    steps = (D_MODEL // tc) * nrow
    slab = D_MODEL // steps
    assert slab * steps == D_MODEL and slab % (2 * SUBLANES) == 0
    w_specs = [_w_tile_spec(tc, lambda j, i, name=name: COL[name] + j * tc)
               for name in _CONV_GROUPS]
    slab_in = pl.BlockSpec((None, slab, D_MODEL), lambda j, i: (0, j * nrow + i, 0))
    slab_out = pl.BlockSpec((slab, D_MODEL), lambda j, i: (j * nrow + i, 0))
    cast_shape = jax.ShapeDtypeStruct((D_MODEL, D_MODEL), BF16)
    return pl.pallas_call(
        functools.partial(_conv_kernel, tiles_per_batch=rows_per_batch // tm, row_chunk=row_chunk),
        grid=(D_MODEL // tc, nrow),
        in_specs=[pl.BlockSpec((tm, D_MODEL), lambda j, i: (i, 0))] + w_specs + [
            pl.BlockSpec((N_META, tc), lambda j, i: (0, j)),
            pl.BlockSpec((CONV_K, tc), lambda j, i: (0, j)),
            slab_in, slab_in, slab_in,
        ],
        out_specs=[pl.BlockSpec((tm, tc), lambda j, i: (i, j)), slab_out, slab_out, slab_out],
        out_shape=[jax.ShapeDtypeStruct((r, D_MODEL), BF16), cast_shape, cast_shape, cast_shape],
        scratch_shapes=[pltpu.VMEM((len(_CONV_GROUPS), tc, D_MODEL), BF16),
                        pltpu.VMEM((SUBLANES + tm, tc), F32)],
        compiler_params=pltpu.CompilerParams(
            dimension_semantics=("arbitrary", "arbitrary"), vmem_limit_bytes=VMEM_LIMIT),
        name="conv_proj",
    )(h, *([wt] * len(_CONV_GROUPS)), g_meta, conv_w, *out_weights)


def _attn_kernel(q_ref, k_ref, v_ref, z_ref, km_ref, vm_ref, g_ref, gm_ref, eqk_ref,
                 o_ref, qa_ref, kf_ref, va_ref, tri_ref, s_ref, tmax_ref, m_ref, acc_ref,
                 *, tq, tk, ts, rb):
    seq = q_ref.shape[0]
    nq = seq // tq
    assert tq == 2 * tk and seq % tq == 0
    half = tk // 2

    tri_ref[...] = jnp.where(
        lax.broadcasted_iota(jnp.int32, (tk, tk), 1) <= lax.broadcasted_iota(jnp.int32, (tk, tk), 0),
        0.0, NEG_BIG)
    kma = jnp.concatenate(
        [km_ref[...], _dot(gm_ref[...], eqk_ref[:, LANES:]).astype(BF16)], axis=1)
    pad_lane = lax.broadcasted_iota(jnp.int32, (LANES - N_META, 2 * LANES), 1)
    k_meta_blk = jnp.concatenate(
        [kma, jnp.where(pad_lane == LANES + SPLIT_TERMS, NEG_BIG, 0.0).astype(BF16)],
        axis=0)
    v_meta_blk = jnp.concatenate(
        [vm_ref[...], jnp.ones((N_META, LANES), BF16)], axis=1)
    v_meta_blk = jnp.concatenate(
        [v_meta_blk, jnp.zeros((LANES - N_META, 2 * LANES), BF16)], axis=0)

    def setup(row0):
        rows = slice(row0, row0 + ts)
        feat = _dot(g_ref[rows, :], eqk_ref[...]).astype(BF16)
        kf_ref[rows, :] = feat[:, LANES:]
        qa_ref[rows, :] = jnp.concatenate([q_ref[rows, :], feat[:, :LANES]], axis=1)
        va_ref[rows, :] = jnp.concatenate([v_ref[rows, :], jnp.ones((ts, LANES), BF16)], axis=1)
        m_ref[rows, :] = jnp.full((ts, LANES), NEG_BIG, F32)
        acc_ref[rows, :] = jnp.zeros((ts, 2 * LANES), F32)

    def width(key0, nkeys):
        return nkeys + LANES if key0 == 0 else nkeys

    def stage_a(row0, nrows, key0, nkeys, tri, slot):
        keys = slice(key0, key0 + nkeys)
        ka = jnp.concatenate([k_ref[keys, :], kf_ref[keys, :]], axis=1)
        if key0 == 0:
            ka = jnp.concatenate([ka, k_meta_blk], axis=0)
        s = _dot_nt(qa_ref[row0:row0 + nrows, :], ka)
        w = width(key0, nkeys)
        for r0 in range(0, nrows, rb):
            part = s[r0:r0 + rb]
            if tri:
                masked = part[:, 0:nkeys] + tri_ref[r0:r0 + rb, 0:nkeys]
                part = jnp.concatenate([masked, part[:, nkeys:]], axis=1) if w > nkeys else masked
            s_ref[slot, r0:r0 + rb, 0:w] = part
            t = part[:, 0:LANES]
            for c in range(1, w // LANES):
                t = jnp.maximum(t, part[:, c * LANES:(c + 1) * LANES])
            tmax_ref[slot, r0:r0 + rb, :] = jnp.broadcast_to(
                jnp.max(t, axis=1, keepdims=True), (rb, LANES))

    def stage_b(row0, nrows, key0, nkeys, tri, slot):
        del tri
        w = width(key0, nkeys)
        alphas, ps = [], []
        for r0 in range(0, nrows, rb):
            rows = slice(row0 + r0, row0 + r0 + rb)
            m_old = m_ref[rows, :]
            m_new = jnp.maximum(m_old, tmax_ref[slot, r0:r0 + rb, :])
            m_ref[rows, :] = m_new
            alphas.append(jnp.exp2(m_old - m_new))
            ps.append(jnp.concatenate(
                [jnp.exp2(s_ref[slot, r0:r0 + rb, c * LANES:(c + 1) * LANES] - m_new).astype(BF16)
                 for c in range(w // LANES)], axis=1))
        va = va_ref[key0:key0 + nkeys, :]
        if key0 == 0:
            va = jnp.concatenate([va, v_meta_blk], axis=0)
        pv = _dot(jnp.concatenate(ps, axis=0), va)
        for n, r0 in enumerate(range(0, nrows, rb)):
            rows = slice(row0 + r0, row0 + r0 + rb)
            alpha = jnp.concatenate([alphas[n], alphas[n]], axis=1)
            acc_ref[rows, :] = alpha * acc_ref[rows, :] + pv[r0:r0 + rb]

    def diagonal(row0):
        return [(row0, half, row0, half, True),
                (row0 + half, half, row0, half, False),
                (row0 + half, half, row0 + half, half, True)]

    items = []
    for i in range(nq):
        top = i * tq
        items += [(top, tq, j * tk, tk, False) for j in range(2 * i)]
        items += [(top + tk, tk, top, tk, False)]
        items += diagonal(top) + diagonal(top + tk)

    def finalize(row0, nrows):
        for r0 in range(row0, row0 + nrows, rb):
            o = (acc_ref[r0:r0 + rb, :LANES] / acc_ref[r0:r0 + rb, LANES:]
                 * _silu(z_ref[r0:r0 + rb, :].astype(F32)))
            o_ref[r0:r0 + rb, :] = o.astype(BF16)

    rows_ready = [0]

    def scores(item, slot):
        row0, nrows, key0, nkeys, _ = item
        while rows_ready[0] < max(row0 + nrows, key0 + nkeys):
            setup(rows_ready[0])
            rows_ready[0] += ts
        stage_a(*item, slot)

    scores(items[0], 0)
    for n, item in enumerate(items):
        if n + 1 < len(items):
            scores(items[n + 1], 1 - n % 2)
        stage_b(*item, n % 2)
        if item[4]:
            finalize(item[0], item[1])


def _attn_call(proj, k_meta, v_meta, gate, gate_meta, eqk, batch, seq):
    tq, tk, ts, rb = TILES.attn_tq, TILES.attn_tk, TILES.attn_ts, TILES.attn_rb
    r = batch * seq
    head = lambda g: pl.BlockSpec((None, seq, HEAD_DIM), lambda b, h, g=g: (g, b, h))
    return pl.pallas_call(
        functools.partial(_attn_kernel, tq=tq, tk=tk, ts=ts, rb=rb),
        grid=(batch, HEADS),
        in_specs=[
            head(0), head(1), head(2), head(3),
            pl.BlockSpec((N_META, HEAD_DIM), lambda b, h: (0, h)),
            pl.BlockSpec((N_META, HEAD_DIM), lambda b, h: (0, h)),
            pl.BlockSpec((seq, LANES), lambda b, h: (b, 0)),
            pl.BlockSpec((N_META, LANES), lambda b, h: (0, 0)),
            pl.BlockSpec((None, LANES, 2 * LANES), lambda b, h: (h, 0, 0)),
        ],
        out_specs=pl.BlockSpec((seq, HEAD_DIM), lambda b, h: (b, h)),
        out_shape=jax.ShapeDtypeStruct((r, D_MODEL), BF16),
        scratch_shapes=[
            pltpu.VMEM((seq, 2 * LANES), BF16),
            pltpu.VMEM((seq, LANES), BF16),
            pltpu.VMEM((seq, 2 * LANES), BF16),
            pltpu.VMEM((tk, tk), F32),
            pltpu.VMEM((2, tq, tk + LANES), F32),
            pltpu.VMEM((2, tq, LANES), F32),
            pltpu.VMEM((seq, LANES), F32),
            pltpu.VMEM((seq, 2 * LANES), F32),
        ],
        compiler_params=pltpu.CompilerParams(
            dimension_semantics=("arbitrary", "arbitrary"), vmem_limit_bytes=VMEM_LIMIT),
        name="fox_attention",
    )(proj, proj, proj, proj, k_meta, v_meta, gate, gate_meta, eqk)


def _out_kernel(aa_ref, ac_ref, ma_ref, mc_ref, x_ref, wa_ref, wc_ref, wo_ref, gain_ref, o_ref):
    ya = _dot(aa_ref[...], wa_ref[...])
    yc = _dot(ac_ref[...], wc_ref[...])
    merged = (jax.nn.sigmoid(ma_ref[...].astype(F32)) * ya
              + jax.nn.sigmoid(mc_ref[...].astype(F32)) * yc)
    y = x_ref[...] + _dot(merged.astype(BF16), wo_ref[...])
    o_ref[...] = _rmsnorm(y, gain_ref[...])


def _out_call(a_att, a_conv, proj, x2d, wa, wc, wo, gain):
    tm = TILES.out_tm
    r = x2d.shape[0]
    row = lambda i: (i, 0)
    act = pl.BlockSpec((tm, D_MODEL), row)
    gate = lambda name: pl.BlockSpec((None, tm, D_MODEL),
                                     lambda i, g=_PROJ_GROUPS.index(name): (g, i, 0))
    weight = pl.BlockSpec((D_MODEL, D_MODEL), lambda i: (0, 0), pipeline_mode=pl.Buffered(1))
    return pl.pallas_call(
        _out_kernel,
        grid=(r // tm,),
        in_specs=[act, act, gate("m_att"), gate("m_conv"), act, weight, weight, weight,
                  pl.BlockSpec((1, D_MODEL), lambda i: (0, 0))],
        out_specs=act,
        out_shape=jax.ShapeDtypeStruct((r, D_MODEL), F32),
        compiler_params=pltpu.CompilerParams(
            dimension_semantics=("arbitrary",), vmem_limit_bytes=VMEM_LIMIT),
        name="out_proj",
    )(a_att, a_conv, proj, proj, x2d, wa, wc, wo, gain)


def kernel(x, meta_tokens, norm_gain, w_in, b_f, conv_w, w_att_o, w_conv_o, w_out, final_gain):
    batch, seq, _ = x.shape
    assert x.shape == (batch, seq, D_MODEL) and x.dtype == F32
    assert seq % max(TILES.attn_tq, TILES.conv_tm, TILES.norm_tm) == 0
    assert meta_tokens.shape == (N_META, D_MODEL) and b_f.shape == (1, HEADS)
    assert norm_gain.shape[0] == 1, "single-layer block"
    assert w_in.shape == (1, D_MODEL, N_IN) and w_in.dtype == F32
    wt = jnp.transpose(w_in[0])
    bf = jnp.pad(b_f[0].astype(F32), (0, LANES - HEADS)).reshape(1, LANES)
    gain = norm_gain[0].reshape(1, D_MODEL).astype(F32)
    fgain = final_gain.reshape(1, D_MODEL).astype(F32)
    place, ones, eqk = _feature_constants()

    k_meta, v_meta, g_meta, cm, gate_meta = _meta_call(
        meta_tokens.astype(F32), gain, wt, bf, place, ones)
    h, gate = _norm_call(x, gain, wt, bf, cm, place, ones)
    proj = _proj_call(h, wt)
    a_conv, wa, wc, wo = _conv_call(h, wt, g_meta, conv_w[0].astype(F32),
                                    (w_att_o, w_conv_o, w_out), seq)
    a_att = _attn_call(proj, k_meta, v_meta, gate, gate_meta, eqk, batch, seq)
    out = _out_call(a_att, a_conv, proj, x.reshape(batch * seq, D_MODEL), wa, wc, wo, fgain)
    return out.reshape(batch, seq, D_MODEL)
```

```python
import functools
from typing import NamedTuple

import numpy as np

import jax
import jax.numpy as jnp
from jax import lax
from jax.experimental import pallas as pl
from jax.experimental.pallas import tpu as pltpu

D_MODEL = 2048
N_META = 16
HEADS = 16
HEAD_DIM = 128
CONV_K = 3
EPS = 1e-6
LANES = 128
SUBLANES = 8
NEG_BIG = -1e30
LOG2E = 1.4426950408889634
GATE_LANES = LANES // HEADS
SPLIT_TERMS = 3
VMEM_LIMIT = 56 * 1024 * 1024


class Tiles(NamedTuple):
    meta_tn: int = 512
    norm_tm: int = 1024
    norm_chunk: int = 256
    proj_tm: int = 1024
    proj_rows: int = 512
    proj_cols: int = 1024
    conv_tm: int = 2048
    conv_tc: int = 256
    conv_rows: int = 512
    attn_tq: int = 1024
    attn_tk: int = 512
    attn_ts: int = 512
    attn_rb: int = 128
    out_tm: int = 256


TILES = Tiles()

_GROUPS = ("q", "k", "v", "f", "z_att", "u", "gate_b", "gate_c", "z_conv", "m_att", "m_conv")
_WIDTHS = (D_MODEL, D_MODEL, D_MODEL, HEADS) + (D_MODEL,) * 7
COL = {name: sum(_WIDTHS[:n]) for n, name in enumerate(_GROUPS)}
N_IN = sum(_WIDTHS)

F32 = jnp.float32
BF16 = jnp.bfloat16


def _dot(a, b):
    return jnp.dot(a, b, preferred_element_type=F32)


def _dot_nt(a, b):
    return lax.dot_general(a, b, (((1,), (1,)), ((), ())), preferred_element_type=F32)


def _rmsnorm(x, gain):
    return x * lax.rsqrt(jnp.mean(x * x, axis=-1, keepdims=True) + EPS) * gain


def _log_sigmoid(x):
    return jnp.minimum(x, 0.0) - jnp.log1p(jnp.exp(-jnp.abs(x)))


def _silu(x):
    return x * jax.nn.sigmoid(x)


def _split3(x):
    hi = x.astype(BF16)
    r1 = x - hi.astype(F32)
    mid = r1.astype(BF16)
    lo = (r1 - mid.astype(F32)).astype(BF16)
    return hi, mid, lo


def _cumsum_rows(x):
    n = x.shape[0]
    tri = (lax.broadcasted_iota(jnp.int32, (n, n), 0)
           >= lax.broadcasted_iota(jnp.int32, (n, n), 1)).astype(BF16)
    hi, mid, lo = _split3(x)
    return _dot(tri, hi) + _dot(tri, mid) + _dot(tri, lo)


def _gate_features(c, place_ref, ones_ref):
    hi, mid, lo = _split3(c * LOG2E)
    cat = jnp.concatenate([hi, mid, lo], axis=1)
    return (_dot(cat, place_ref[...]) + ones_ref[...]).astype(BF16)


def _feature_constants():
    n = SPLIT_TERMS
    assert n + 1 <= GATE_LANES
    place = np.zeros((n * LANES, LANES), np.float32)
    ones = np.zeros((1, LANES), np.float32)
    eq = np.zeros((HEADS, LANES, LANES), np.float32)
    ek = np.zeros((HEADS, LANES, LANES), np.float32)
    for h in range(HEADS):
        base = GATE_LANES * h
        ones[0, base + n] = 1.0
        for f in range(n):
            place[f * LANES + h, base + f] = 1.0
            eq[h, base + f, f] = 1.0
            eq[h, base + n, n + f] = 1.0
            ek[h, base + n, f] = 1.0
            ek[h, base + f, n + f] = -1.0
    return (jnp.asarray(place, BF16), jnp.asarray(ones, F32),
            jnp.asarray(np.concatenate([eq, ek], axis=2), BF16))


def _w_tile_spec(tn, row_of):
    assert all(c % HEADS == 0 for c in COL.values()) and tn % HEADS == 0
    return pl.BlockSpec((pl.Element(tn), pl.Element(D_MODEL)),
                        lambda *ids: (pl.multiple_of(row_of(*ids), HEADS), 0))


def _forget_w_spec():
    assert COL["f"] % LANES == 0
    return pl.BlockSpec((LANES, D_MODEL), lambda *ids: (COL["f"] // LANES, 0))


def _stage_weight(src_ref, dst_ref, chunk=128):
    for r in range(0, src_ref.shape[0], chunk):
        dst_ref[r:r + chunk, :] = src_ref[r:r + chunk, :].astype(BF16)


def _meta_kernel(meta_ref, gain_ref, wk_ref, wv_ref, wu_ref, wgc_ref, wf_ref, bf_ref,
                 place_ref, ones_ref, k_ref, v_ref, g_ref, cm_ref, gm_ref):
    h = _rmsnorm(meta_ref[...], gain_ref[...]).astype(BF16)
    proj = lambda w_ref: _dot_nt(h, w_ref[...].astype(BF16))
    k_ref[...] = proj(wk_ref).astype(BF16)
    v_ref[...] = proj(wv_ref).astype(BF16)
    g_ref[...] = proj(wgc_ref) * proj(wu_ref)
    cm = _cumsum_rows(_log_sigmoid(proj(wf_ref) + bf_ref[...]))
    cm_ref[...] = cm
    gm_ref[...] = _gate_features(cm, place_ref, ones_ref)


def _meta_call(meta, gain, wt, bf, place, ones):
    tn = TILES.meta_tn
    nt = D_MODEL // tn
    const = lambda shape: pl.BlockSpec(shape, lambda j: (0,) * len(shape))
    group = lambda name: _w_tile_spec(tn, lambda j: COL[name] + j * tn)
    return pl.pallas_call(
        _meta_kernel,
        grid=(nt,),
        in_specs=[
            const((N_META, D_MODEL)), const((1, D_MODEL)),
            group("k"), group("v"), group("u"), group("gate_c"), _forget_w_spec(),
            const((1, LANES)), const((SPLIT_TERMS * LANES, LANES)), const((1, LANES)),
        ],
        out_specs=[
            pl.BlockSpec((N_META, tn), lambda j: (0, j)),
            pl.BlockSpec((N_META, tn), lambda j: (0, j)),
            pl.BlockSpec((N_META, tn), lambda j: (0, j)),
            const((N_META, LANES)),
            const((N_META, LANES)),
        ],
        out_shape=[
            jax.ShapeDtypeStruct((N_META, D_MODEL), BF16),
            jax.ShapeDtypeStruct((N_META, D_MODEL), BF16),
            jax.ShapeDtypeStruct((N_META, D_MODEL), F32),
            jax.ShapeDtypeStruct((N_META, LANES), F32),
            jax.ShapeDtypeStruct((N_META, LANES), BF16),
        ],
        compiler_params=pltpu.CompilerParams(
            dimension_semantics=("arbitrary",), vmem_limit_bytes=VMEM_LIMIT),
        name="meta_proj",
    )(meta, gain, wt, wt, wt, wt, wt, bf, place, ones)


def _norm_kernel(x_ref, gain_ref, wf_ref, bf_ref, cm_ref, place_ref, ones_ref,
                 h_ref, gate_ref, carry_ref, *, chunk):
    @pl.when(pl.program_id(1) == 0)
    def _():
        carry_ref[...] = cm_ref[N_META - 1:N_META, :]

    h = _rmsnorm(x_ref[...], gain_ref[...]).astype(BF16)
    h_ref[...] = h
    logf = _log_sigmoid(_dot_nt(h, wf_ref[...].astype(BF16)) + bf_ref[...])
    tm = logf.shape[0]
    run = carry_ref[...]
    parts = []
    for j in range(tm // chunk):
        c = _cumsum_rows(logf[j * chunk:(j + 1) * chunk]) + run
        run = c[chunk - 1:chunk, :]
        parts.append(c)
    carry_ref[...] = run
    gate_ref[...] = _gate_features(jnp.concatenate(parts, axis=0), place_ref, ones_ref)


def _norm_call(x, gain, wt, bf, cm, place, ones):
    tm, chunk = TILES.norm_tm, TILES.norm_chunk
    b, s, _ = x.shape
    nt = s // tm
    const = lambda shape: pl.BlockSpec(shape, lambda bi, i: (0,) * len(shape))
    return pl.pallas_call(
        functools.partial(_norm_kernel, chunk=chunk),
        grid=(b, nt),
        in_specs=[
            pl.BlockSpec((None, tm, D_MODEL), lambda bi, i: (bi, i, 0)),
            const((1, D_MODEL)),
            _forget_w_spec(),
            const((1, LANES)),
            const((N_META, LANES)),
            const((SPLIT_TERMS * LANES, LANES)),
            const((1, LANES)),
        ],
        out_specs=[
            pl.BlockSpec((tm, D_MODEL), lambda bi, i: (bi * nt + i, 0)),
            pl.BlockSpec((tm, LANES), lambda bi, i: (bi * nt + i, 0)),
        ],
        out_shape=[
            jax.ShapeDtypeStruct((b * s, D_MODEL), BF16),
            jax.ShapeDtypeStruct((b * s, LANES), BF16),
        ],
        scratch_shapes=[pltpu.VMEM((1, LANES), F32)],
        compiler_params=pltpu.CompilerParams(
            dimension_semantics=("arbitrary", "arbitrary"), vmem_limit_bytes=VMEM_LIMIT),
        name="norm_forget",
    )(x, gain, wt, bf, cm, place, ones)


_PROJ_GROUPS = ("q", "k", "v", "z_att", "m_att", "m_conv")


def _proj_group_row(g):
    row = jnp.int32(COL[_PROJ_GROUPS[0]])
    for n in range(1, len(_PROJ_GROUPS)):
        row = jnp.where(g == n, COL[_PROJ_GROUPS[n]], row)
    return row


def _proj_kernel(h_ref, wt_hbm, o_ref, w_land, w_even, w_odd, sem, *, row_chunk, col_chunk):
    g = pl.program_id(0)
    i = pl.program_id(1)
    n_groups = pl.num_programs(0)
    last_row_step = pl.num_programs(1) - 1

    def fetch(group):
        rows = pl.ds(pl.multiple_of(_proj_group_row(group), HEADS), D_MODEL)
        return pltpu.make_async_copy(wt_hbm.at[rows, :], w_land, sem)

    @pl.when((g == 0) & (i == 0))
    def _():
        fetch(0).start()
        fetch(0).wait()
        _stage_weight(w_land, w_even)

    @pl.when((i == 0) & (g + 1 < n_groups))
    def _():
        fetch(g + 1).start()

    scale = jnp.where(g == 0, HEAD_DIM ** -0.5 * LOG2E, 1.0).astype(F32)

    def multiply(w_cur, w_next=None):
        n_dots = (h_ref.shape[0] // row_chunk) * (D_MODEL // col_chunk)
        share = D_MODEL // n_dots
        n = 0
        for r in range(0, h_ref.shape[0], row_chunk):
            for c in range(0, D_MODEL, col_chunk):
                y = _dot_nt(h_ref[r:r + row_chunk, :], w_cur[c:c + col_chunk, :])
                if w_next is not None:
                    rows = slice(n * share, (n + 1) * share)
                    _stage_weight(w_land.at[rows, :], w_next.at[rows, :])
                o_ref[r:r + row_chunk, c:c + col_chunk] = (y * scale).astype(BF16)
                n += 1

    hand_over = (i == last_row_step) & (g + 1 < n_groups)
    for parity, (w_cur, w_next) in enumerate(((w_even, w_odd), (w_odd, w_even))):
        mine = lax.rem(g, 2) == parity

        @pl.when(mine & jnp.logical_not(hand_over))
        def _(w_cur=w_cur):
            multiply(w_cur)

        @pl.when(mine & hand_over)
        def _(w_cur=w_cur, w_next=w_next):
            fetch(g + 1).wait()
            multiply(w_cur, w_next)


def _proj_call(h, wt):
    tm, row_chunk = TILES.proj_tm, TILES.proj_rows
    r = h.shape[0]
    ng = len(_PROJ_GROUPS)
    return pl.pallas_call(
        functools.partial(_proj_kernel, row_chunk=row_chunk, col_chunk=TILES.proj_cols),
        grid=(ng, r // tm),
        in_specs=[
            pl.BlockSpec((tm, D_MODEL), lambda g, i: (i, 0)),
            pl.BlockSpec(memory_space=pl.ANY),
        ],
        out_specs=pl.BlockSpec((None, tm, D_MODEL), lambda g, i: (g, i, 0)),
        out_shape=jax.ShapeDtypeStruct((ng, r, D_MODEL), BF16),
        scratch_shapes=[pltpu.VMEM((D_MODEL, D_MODEL), F32),
                        pltpu.VMEM((D_MODEL, D_MODEL), BF16),
                        pltpu.VMEM((D_MODEL, D_MODEL), BF16),
                        pltpu.SemaphoreType.DMA(())],
        compiler_params=pltpu.CompilerParams(
            dimension_semantics=("arbitrary", "arbitrary"),
            vmem_limit_bytes=VMEM_LIMIT),
        name="plain_proj",
    )(h, wt)


_CONV_GROUPS = ("u", "gate_b", "gate_c", "z_conv")


def _conv_kernel(h_ref, wu_ref, wgb_ref, wgc_ref, wzc_ref, gm_ref, cw_ref,
                 wa_ref, wc_ref, wo_ref, o_ref, wa_out, wc_out, wo_out,
                 w_s, gbuf_ref, *, tiles_per_batch, row_chunk):
    i = pl.program_id(1)
    tm = h_ref.shape[0]
    for src, dst in ((wa_ref, wa_out), (wc_ref, wc_out), (wo_ref, wo_out)):
        dst[...] = src[...].astype(BF16)

    @pl.when(i == 0)
    def _():
        for n, w_ref in enumerate((wu_ref, wgb_ref, wgc_ref, wzc_ref)):
            _stage_weight(w_ref, w_s.at[n])

    @pl.when(i % tiles_per_batch == 0)
    def _():
        gbuf_ref[0:SUBLANES, :] = gm_ref[N_META - SUBLANES:N_META, :]

    for r in range(0, tm, row_chunk):
        h = h_ref[r:r + row_chunk, :]
        g = _dot_nt(h, w_s[2]) * _dot_nt(h, w_s[0])
        first = SUBLANES + r
        gbuf_ref[first:first + row_chunk, :] = g
        conv = (gbuf_ref[first - 2:first - 2 + row_chunk, :] * cw_ref[0:1, :]
                + gbuf_ref[first - 1:first - 1 + row_chunk, :] * cw_ref[1:2, :]
                + g * cw_ref[2:3, :])
        gb = _dot_nt(h, w_s[1])
        zc = _dot_nt(h, w_s[3])
        o_ref[r:r + row_chunk, :] = (gb * conv * _silu(zc)).astype(BF16)
    gbuf_ref[0:SUBLANES, :] = gbuf_ref[tm:tm + SUBLANES, :]


def _conv_call(h, wt, g_meta, conv_w, out_weights, rows_per_batch):
    tm, tc, row_chunk = TILES.conv_tm, TILES.conv_tc, TILES.conv_rows
    r = h.shape[0]
    nrow = r // tm
    steps = (D_MODEL // tc) * nrow
    slab = D_MODEL // steps
    assert slab * steps == D_MODEL and slab % (2 * SUBLANES) == 0
    w_specs = [_w_tile_spec(tc, lambda j, i, name=name: COL[name] + j * tc)
               for name in _CONV_GROUPS]
    slab_in = pl.BlockSpec((None, slab, D_MODEL), lambda j, i: (0, j * nrow + i, 0))
    slab_out = pl.BlockSpec((slab, D_MODEL), lambda j, i: (j * nrow + i, 0))
    cast_shape = jax.ShapeDtypeStruct((D_MODEL, D_MODEL), BF16)
    return pl.pallas_call(
        functools.partial(_conv_kernel, tiles_per_batch=rows_per_batch // tm, row_chunk=row_chunk),
        grid=(D_MODEL // tc, nrow),
        in_specs=[pl.BlockSpec((tm, D_MODEL), lambda j, i: (i, 0))] + w_specs + [
            pl.BlockSpec((N_META, tc), lambda j, i: (0, j)),
            pl.BlockSpec((CONV_K, tc), lambda j, i: (0, j)),
            slab_in, slab_in, slab_in,
        ],
        out_specs=[pl.BlockSpec((tm, tc), lambda j, i: (i, j)), slab_out, slab_out, slab_out],
        out_shape=[jax.ShapeDtypeStruct((r, D_MODEL), BF16), cast_shape, cast_shape, cast_shape],
        scratch_shapes=[pltpu.VMEM((len(_CONV_GROUPS), tc, D_MODEL), BF16),
                        pltpu.VMEM((SUBLANES + tm, tc), F32)],
        compiler_params=pltpu.CompilerParams(
            dimension_semantics=("arbitrary", "arbitrary"), vmem_limit_bytes=VMEM_LIMIT),
        name="conv_proj",
    )(h, *([wt] * len(_CONV_GROUPS)), g_meta, conv_w, *out_weights)


def _attn_kernel(q_ref, k_ref, v_ref, z_ref, km_ref, vm_ref, g_ref, gm_ref, eqk_ref,
                 o_ref, qa_ref, kf_ref, va_ref, tri_ref, s_ref, tmax_ref, m_ref, acc_ref,
                 *, tq, tk, ts, rb):
    seq = q_ref.shape[0]
    nq = seq // tq
    assert tq == 2 * tk and seq % tq == 0
    half = tk // 2

    tri_ref[...] = jnp.where(
        lax.broadcasted_iota(jnp.int32, (tk, tk), 1) <= lax.broadcasted_iota(jnp.int32, (tk, tk), 0),
        0.0, NEG_BIG)
    kma = jnp.concatenate(
        [km_ref[...], _dot(gm_ref[...], eqk_ref[:, LANES:]).astype(BF16)], axis=1)
    pad_lane = lax.broadcasted_iota(jnp.int32, (LANES - N_META, 2 * LANES), 1)
    k_meta_blk = jnp.concatenate(
        [kma, jnp.where(pad_lane == LANES + SPLIT_TERMS, NEG_BIG, 0.0).astype(BF16)],
        axis=0)
    v_meta_blk = jnp.concatenate(
        [vm_ref[...], jnp.ones((N_META, LANES), BF16)], axis=1)
    v_meta_blk = jnp.concatenate(
        [v_meta_blk, jnp.zeros((LANES - N_META, 2 * LANES), BF16)], axis=0)

    def setup(row0):
        rows = slice(row0, row0 + ts)
        feat = _dot(g_ref[rows, :], eqk_ref[...]).astype(BF16)
        kf_ref[rows, :] = feat[:, LANES:]
        qa_ref[rows, :] = jnp.concatenate([q_ref[rows, :], feat[:, :LANES]], axis=1)
        va_ref[rows, :] = jnp.concatenate([v_ref[rows, :], jnp.ones((ts, LANES), BF16)], axis=1)
        m_ref[rows, :] = jnp.full((ts, LANES), NEG_BIG, F32)
        acc_ref[rows, :] = jnp.zeros((ts, 2 * LANES), F32)

    def width(key0, nkeys):
        return nkeys + LANES if key0 == 0 else nkeys

    def stage_a(row0, nrows, key0, nkeys, tri, slot):
        keys = slice(key0, key0 + nkeys)
        ka = jnp.concatenate([k_ref[keys, :], kf_ref[keys, :]], axis=1)
        if key0 == 0:
            ka = jnp.concatenate([ka, k_meta_blk], axis=0)
        s = _dot_nt(qa_ref[row0:row0 + nrows, :], ka)
        w = width(key0, nkeys)
        for r0 in range(0, nrows, rb):
            part = s[r0:r0 + rb]
            if tri:
                masked = part[:, 0:nkeys] + tri_ref[r0:r0 + rb, 0:nkeys]
                part = jnp.concatenate([masked, part[:, nkeys:]], axis=1) if w > nkeys else masked
            s_ref[slot, r0:r0 + rb, 0:w] = part
            t = part[:, 0:LANES]
            for c in range(1, w // LANES):
                t = jnp.maximum(t, part[:, c * LANES:(c + 1) * LANES])
            tmax_ref[slot, r0:r0 + rb, :] = jnp.broadcast_to(
                jnp.max(t, axis=1, keepdims=True), (rb, LANES))

    def stage_b(row0, nrows, key0, nkeys, tri, slot):
        del tri
        w = width(key0, nkeys)
        alphas, ps = [], []
        for r0 in range(0, nrows, rb):
            rows = slice(row0 + r0, row0 + r0 + rb)
            m_old = m_ref[rows, :]
            m_new = jnp.maximum(m_old, tmax_ref[slot, r0:r0 + rb, :])
            m_ref[rows, :] = m_new
            alphas.append(jnp.exp2(m_old - m_new))
            ps.append(jnp.concatenate(
                [jnp.exp2(s_ref[slot, r0:r0 + rb, c * LANES:(c + 1) * LANES] - m_new).astype(BF16)
                 for c in range(w // LANES)], axis=1))
        va = va_ref[key0:key0 + nkeys, :]
        if key0 == 0:
            va = jnp.concatenate([va, v_meta_blk], axis=0)
        pv = _dot(jnp.concatenate(ps, axis=0), va)
        for n, r0 in enumerate(range(0, nrows, rb)):
            rows = slice(row0 + r0, row0 + r0 + rb)
            alpha = jnp.concatenate([alphas[n], alphas[n]], axis=1)
            acc_ref[rows, :] = alpha * acc_ref[rows, :] + pv[r0:r0 + rb]

    def diagonal(row0):
        return [(row0, half, row0, half, True),
                (row0 + half, half, row0, half, False),
                (row0 + half, half, row0 + half, half, True)]

    items = []
    for i in range(nq):
        top = i * tq
        items += [(top, tq, j * tk, tk, False) for j in range(2 * i)]
        items += [(top + tk, tk, top, tk, False)]
        items += diagonal(top) + diagonal(top + tk)

    def finalize(row0, nrows):
        for r0 in range(row0, row0 + nrows, rb):
            o = (acc_ref[r0:r0 + rb, :LANES] / acc_ref[r0:r0 + rb, LANES:]
                 * _silu(z_ref[r0:r0 + rb, :].astype(F32)))
            o_ref[r0:r0 + rb, :] = o.astype(BF16)

    rows_ready = [0]

    def scores(item, slot):
        row0, nrows, key0, nkeys, _ = item
        while rows_ready[0] < max(row0 + nrows, key0 + nkeys):
            setup(rows_ready[0])
            rows_ready[0] += ts
        stage_a(*item, slot)

    scores(items[0], 0)
    for n, item in enumerate(items):
        if n + 1 < len(items):
            scores(items[n + 1], 1 - n % 2)
        stage_b(*item, n % 2)
        if item[4]:
            finalize(item[0], item[1])


def _attn_call(proj, k_meta, v_meta, gate, gate_meta, eqk, batch, seq):
    tq, tk, ts, rb = TILES.attn_tq, TILES.attn_tk, TILES.attn_ts, TILES.attn_rb
    r = batch * seq
    head = lambda g: pl.BlockSpec((None, seq, HEAD_DIM), lambda b, h, g=g: (g, b, h))
    return pl.pallas_call(
        functools.partial(_attn_kernel, tq=tq, tk=tk, ts=ts, rb=rb),
        grid=(batch, HEADS),
        in_specs=[
            head(0), head(1), head(2), head(3),
            pl.BlockSpec((N_META, HEAD_DIM), lambda b, h: (0, h)),
            pl.BlockSpec((N_META, HEAD_DIM), lambda b, h: (0, h)),
            pl.BlockSpec((seq, LANES), lambda b, h: (b, 0)),
            pl.BlockSpec((N_META, LANES), lambda b, h: (0, 0)),
            pl.BlockSpec((None, LANES, 2 * LANES), lambda b, h: (h, 0, 0)),
        ],
        out_specs=pl.BlockSpec((seq, HEAD_DIM), lambda b, h: (b, h)),
        out_shape=jax.ShapeDtypeStruct((r, D_MODEL), BF16),
        scratch_shapes=[
            pltpu.VMEM((seq, 2 * LANES), BF16),
            pltpu.VMEM((seq, LANES), BF16),
            pltpu.VMEM((seq, 2 * LANES), BF16),
            pltpu.VMEM((tk, tk), F32),
            pltpu.VMEM((2, tq, tk + LANES), F32),
            pltpu.VMEM((2, tq, LANES), F32),
            pltpu.VMEM((seq, LANES), F32),
            pltpu.VMEM((seq, 2 * LANES), F32),
        ],
        compiler_params=pltpu.CompilerParams(
            dimension_semantics=("arbitrary", "arbitrary"), vmem_limit_bytes=VMEM_LIMIT),
        name="fox_attention",
    )(proj, proj, proj, proj, k_meta, v_meta, gate, gate_meta, eqk)


def _out_kernel(aa_ref, ac_ref, ma_ref, mc_ref, x_ref, wa_ref, wc_ref, wo_ref, gain_ref, o_ref):
    ya = _dot(aa_ref[...], wa_ref[...])
    yc = _dot(ac_ref[...], wc_ref[...])
    merged = (jax.nn.sigmoid(ma_ref[...].astype(F32)) * ya
              + jax.nn.sigmoid(mc_ref[...].astype(F32)) * yc)
    y = x_ref[...] + _dot(merged.astype(BF16), wo_ref[...])
    o_ref[...] = _rmsnorm(y, gain_ref[...])


def _out_call(a_att, a_conv, proj, x2d, wa, wc, wo, gain):
    tm = TILES.out_tm
    r = x2d.shape[0]
    row = lambda i: (i, 0)
    act = pl.BlockSpec((tm, D_MODEL), row)
    gate = lambda name: pl.BlockSpec((None, tm, D_MODEL),
                                     lambda i, g=_PROJ_GROUPS.index(name): (g, i, 0))
    weight = pl.BlockSpec((D_MODEL, D_MODEL), lambda i: (0, 0), pipeline_mode=pl.Buffered(1))
    return pl.pallas_call(
        _out_kernel,
        grid=(r // tm,),
        in_specs=[act, act, gate("m_att"), gate("m_conv"), act, weight, weight, weight,
                  pl.BlockSpec((1, D_MODEL), lambda i: (0, 0))],
        out_specs=act,
        out_shape=jax.ShapeDtypeStruct((r, D_MODEL), F32),
        compiler_params=pltpu.CompilerParams(
            dimension_semantics=("arbitrary",), vmem_limit_bytes=VMEM_LIMIT),
        name="out_proj",
    )(a_att, a_conv, proj, proj, x2d, wa, wc, wo, gain)


def kernel(x, meta_tokens, norm_gain, w_in, b_f, conv_w, w_att_o, w_conv_o, w_out, final_gain):
    batch, seq, _ = x.shape
    assert x.shape == (batch, seq, D_MODEL) and x.dtype == F32
    assert seq % max(TILES.attn_tq, TILES.conv_tm, TILES.norm_tm) == 0
    assert meta_tokens.shape == (N_META, D_MODEL) and b_f.shape == (1, HEADS)
    assert norm_gain.shape[0] == 1, "single-layer block"
    assert w_in.shape == (1, D_MODEL, N_IN) and w_in.dtype == F32
    wt = jnp.transpose(w_in[0])
    bf = jnp.pad(b_f[0].astype(F32), (0, LANES - HEADS)).reshape(1, LANES)
    gain = norm_gain[0].reshape(1, D_MODEL).astype(F32)
    fgain = final_gain.reshape(1, D_MODEL).astype(F32)
    place, ones, eqk = _feature_constants()

    k_meta, v_meta, g_meta, cm, gate_meta = _meta_call(
        meta_tokens.astype(F32), gain, wt, bf, place, ones)
    h, gate = _norm_call(x, gain, wt, bf, cm, place, ones)
    proj = _proj_call(h, wt)
    a_conv, wa, wc, wo = _conv_call(h, wt, g_meta, conv_w[0].astype(F32),
                                    (w_att_o, w_conv_o, w_out), seq)
    a_att = _attn_call(proj, k_meta, v_meta, gate, gate_meta, eqk, batch, seq)
    out = _out_call(a_att, a_conv, proj, x.reshape(batch * seq, D_MODEL), wa, wc, wo, fgain)
    return out.reshape(batch, seq, D_MODEL)
```

```python
import functools
from typing import NamedTuple

import numpy as np

import jax
import jax.numpy as jnp
from jax import lax
from jax.experimental import pallas as pl
from jax.experimental.pallas import tpu as pltpu

D_MODEL = 2048
N_META = 16
HEADS = 16
HEAD_DIM = 128
CONV_K = 3
EPS = 1e-6
LANES = 128
SUBLANES = 8
NEG_BIG = -1e30
LOG2E = 1.4426950408889634
GATE_LANES = LANES // HEADS
SPLIT_TERMS = 3
VMEM_LIMIT = 56 * 1024 * 1024


class Tiles(NamedTuple):
    meta_tn: int = 512
    norm_tm: int = 1024
    norm_chunk: int = 256
    proj_tm: int = 2048
    proj_tn: int = 1024
    proj_rows: int = 512
    conv_tm: int = 2048
    conv_tc: int = 256
    conv_rows: int = 512
    attn_tq: int = 1024
    attn_tk: int = 512
    attn_ts: int = 512
    attn_rb: int = 128
    out_tm: int = 256


TILES = Tiles()

_GROUPS = ("q", "k", "v", "f", "z_att", "u", "gate_b", "gate_c", "z_conv", "m_att", "m_conv")
_WIDTHS = (D_MODEL, D_MODEL, D_MODEL, HEADS) + (D_MODEL,) * 7
COL = {name: sum(_WIDTHS[:n]) for n, name in enumerate(_GROUPS)}
N_IN = sum(_WIDTHS)

F32 = jnp.float32
BF16 = jnp.bfloat16


def _dot(a, b):
    return jnp.dot(a, b, preferred_element_type=F32)


def _dot_nt(a, b):
    return lax.dot_general(a, b, (((1,), (1,)), ((), ())), preferred_element_type=F32)


def _rmsnorm(x, gain):
    return x * lax.rsqrt(jnp.mean(x * x, axis=-1, keepdims=True) + EPS) * gain


def _log_sigmoid(x):
    return jnp.minimum(x, 0.0) - jnp.log1p(jnp.exp(-jnp.abs(x)))


def _silu(x):
    return x * jax.nn.sigmoid(x)


def _split3(x):
    hi = x.astype(BF16)
    r1 = x - hi.astype(F32)
    mid = r1.astype(BF16)
    lo = (r1 - mid.astype(F32)).astype(BF16)
    return hi, mid, lo


def _cumsum_rows(x):
    n = x.shape[0]
    tri = (lax.broadcasted_iota(jnp.int32, (n, n), 0)
           >= lax.broadcasted_iota(jnp.int32, (n, n), 1)).astype(BF16)
    hi, mid, lo = _split3(x)
    return _dot(tri, hi) + _dot(tri, mid) + _dot(tri, lo)


def _gate_features(c, place_ref, ones_ref):
    hi, mid, lo = _split3(c * LOG2E)
    cat = jnp.concatenate([hi, mid, lo], axis=1)
    return (_dot(cat, place_ref[...]) + ones_ref[...]).astype(BF16)


def _feature_constants():
    n = SPLIT_TERMS
    assert n + 1 <= GATE_LANES
    place = np.zeros((n * LANES, LANES), np.float32)
    ones = np.zeros((1, LANES), np.float32)
    eq = np.zeros((HEADS, LANES, LANES), np.float32)
    ek = np.zeros((HEADS, LANES, LANES), np.float32)
    for h in range(HEADS):
        base = GATE_LANES * h
        ones[0, base + n] = 1.0
        for f in range(n):
            place[f * LANES + h, base + f] = 1.0
            eq[h, base + f, f] = 1.0
            eq[h, base + n, n + f] = 1.0
            ek[h, base + n, f] = 1.0
            ek[h, base + f, n + f] = -1.0
    return (jnp.asarray(place, BF16), jnp.asarray(ones, F32),
            jnp.asarray(np.concatenate([eq, ek], axis=2), BF16))


def _w_tile_spec(tn, row_of):
    assert all(c % HEADS == 0 for c in COL.values()) and tn % HEADS == 0
    return pl.BlockSpec((pl.Element(tn), pl.Element(D_MODEL)),
                        lambda *ids: (pl.multiple_of(row_of(*ids), HEADS), 0))


def _forget_w_spec():
    assert COL["f"] % LANES == 0
    return pl.BlockSpec((LANES, D_MODEL), lambda *ids: (COL["f"] // LANES, 0))


def _stage_weight(src_ref, dst_ref, chunk=128):
    for r in range(0, src_ref.shape[0], chunk):
        dst_ref[r:r + chunk, :] = src_ref[r:r + chunk, :].astype(BF16)


def _meta_kernel(meta_ref, gain_ref, wk_ref, wv_ref, wu_ref, wgc_ref, wf_ref, bf_ref,
                 place_ref, ones_ref, k_ref, v_ref, g_ref, cm_ref, gm_ref):
    h = _rmsnorm(meta_ref[...], gain_ref[...]).astype(BF16)
    proj = lambda w_ref: _dot_nt(h, w_ref[...].astype(BF16))
    k_ref[...] = proj(wk_ref).astype(BF16)
    v_ref[...] = proj(wv_ref).astype(BF16)
    g_ref[...] = proj(wgc_ref) * proj(wu_ref)
    cm = _cumsum_rows(_log_sigmoid(proj(wf_ref) + bf_ref[...]))
    cm_ref[...] = cm
    gm_ref[...] = _gate_features(cm, place_ref, ones_ref)


def _meta_call(meta, gain, wt, bf, place, ones):
    tn = TILES.meta_tn
    nt = D_MODEL // tn
    const = lambda shape: pl.BlockSpec(shape, lambda j: (0,) * len(shape))
    group = lambda name: _w_tile_spec(tn, lambda j: COL[name] + j * tn)
    return pl.pallas_call(
        _meta_kernel,
        grid=(nt,),
        in_specs=[
            const((N_META, D_MODEL)), const((1, D_MODEL)),
            group("k"), group("v"), group("u"), group("gate_c"), _forget_w_spec(),
            const((1, LANES)), const((SPLIT_TERMS * LANES, LANES)), const((1, LANES)),
        ],
        out_specs=[
            pl.BlockSpec((N_META, tn), lambda j: (0, j)),
            pl.BlockSpec((N_META, tn), lambda j: (0, j)),
            pl.BlockSpec((N_META, tn), lambda j: (0, j)),
            const((N_META, LANES)),
            const((N_META, LANES)),
        ],
        out_shape=[
            jax.ShapeDtypeStruct((N_META, D_MODEL), BF16),
            jax.ShapeDtypeStruct((N_META, D_MODEL), BF16),
            jax.ShapeDtypeStruct((N_META, D_MODEL), F32),
            jax.ShapeDtypeStruct((N_META, LANES), F32),
            jax.ShapeDtypeStruct((N_META, LANES), BF16),
        ],
        compiler_params=pltpu.CompilerParams(
            dimension_semantics=("arbitrary",), vmem_limit_bytes=VMEM_LIMIT),
        name="meta_proj",
    )(meta, gain, wt, wt, wt, wt, wt, bf, place, ones)


def _norm_kernel(x_ref, gain_ref, wf_ref, bf_ref, cm_ref, place_ref, ones_ref,
                 h_ref, gate_ref, carry_ref, *, chunk):
    @pl.when(pl.program_id(1) == 0)
    def _():
        carry_ref[...] = cm_ref[N_META - 1:N_META, :]

    h = _rmsnorm(x_ref[...], gain_ref[...]).astype(BF16)
    h_ref[...] = h
    logf = _log_sigmoid(_dot_nt(h, wf_ref[...].astype(BF16)) + bf_ref[...])
    tm = logf.shape[0]
    run = carry_ref[...]
    parts = []
    for j in range(tm // chunk):
        c = _cumsum_rows(logf[j * chunk:(j + 1) * chunk]) + run
        run = c[chunk - 1:chunk, :]
        parts.append(c)
    carry_ref[...] = run
    gate_ref[...] = _gate_features(jnp.concatenate(parts, axis=0), place_ref, ones_ref)


def _norm_call(x, gain, wt, bf, cm, place, ones):
    tm, chunk = TILES.norm_tm, TILES.norm_chunk
    b, s, _ = x.shape
    nt = s // tm
    const = lambda shape: pl.BlockSpec(shape, lambda bi, i: (0,) * len(shape))
    return pl.pallas_call(
        functools.partial(_norm_kernel, chunk=chunk),
        grid=(b, nt),
        in_specs=[
            pl.BlockSpec((None, tm, D_MODEL), lambda bi, i: (bi, i, 0)),
            const((1, D_MODEL)),
            _forget_w_spec(),
            const((1, LANES)),
            const((N_META, LANES)),
            const((SPLIT_TERMS * LANES, LANES)),
            const((1, LANES)),
        ],
        out_specs=[
            pl.BlockSpec((tm, D_MODEL), lambda bi, i: (bi * nt + i, 0)),
            pl.BlockSpec((tm, LANES), lambda bi, i: (bi * nt + i, 0)),
        ],
        out_shape=[
            jax.ShapeDtypeStruct((b * s, D_MODEL), BF16),
            jax.ShapeDtypeStruct((b * s, LANES), BF16),
        ],
        scratch_shapes=[pltpu.VMEM((1, LANES), F32)],
        compiler_params=pltpu.CompilerParams(
            dimension_semantics=("arbitrary", "arbitrary"), vmem_limit_bytes=VMEM_LIMIT),
        name="norm_forget",
    )(x, gain, wt, bf, cm, place, ones)


_PROJ_GROUPS = ("q", "k", "v", "z_att", "m_att", "m_conv")


def _proj_group_row(g):
    row = jnp.int32(COL[_PROJ_GROUPS[0]])
    for n in range(1, len(_PROJ_GROUPS)):
        row = jnp.where(g == n, COL[_PROJ_GROUPS[n]], row)
    return row


def _proj_kernel(h_ref, w_ref, o_ref, w_s, *, row_chunk):
    @pl.when(pl.program_id(2) == 0)
    def _():
        _stage_weight(w_ref, w_s)

    scale = jnp.where(pl.program_id(0) == 0, HEAD_DIM ** -0.5 * LOG2E, 1.0).astype(F32)
    for r in range(0, h_ref.shape[0], row_chunk):
        y = _dot_nt(h_ref[r:r + row_chunk, :], w_s[...])
        o_ref[r:r + row_chunk, :] = (y * scale).astype(BF16)


def _proj_call(h, wt):
    tm, tn, row_chunk = TILES.proj_tm, TILES.proj_tn, TILES.proj_rows
    r = h.shape[0]
    ng = len(_PROJ_GROUPS)
    return pl.pallas_call(
        functools.partial(_proj_kernel, row_chunk=row_chunk),
        grid=(ng, D_MODEL // tn, r // tm),
        in_specs=[
            pl.BlockSpec((tm, D_MODEL), lambda g, j, i: (i, 0)),
            _w_tile_spec(tn, lambda g, j, i: _proj_group_row(g) + j * tn),
        ],
        out_specs=pl.BlockSpec((None, tm, tn), lambda g, j, i: (g, i, j)),
        out_shape=jax.ShapeDtypeStruct((ng, r, D_MODEL), BF16),
        scratch_shapes=[pltpu.VMEM((tn, D_MODEL), BF16)],
        compiler_params=pltpu.CompilerParams(
            dimension_semantics=("arbitrary", "arbitrary", "arbitrary"),
            vmem_limit_bytes=VMEM_LIMIT),
        name="plain_proj",
    )(h, wt)


_CONV_GROUPS = ("u", "gate_b", "gate_c", "z_conv")


def _conv_kernel(h_ref, wu_ref, wgb_ref, wgc_ref, wzc_ref, gm_ref, cw_ref,
                 wa_ref, wc_ref, wo_ref, o_ref, wa_out, wc_out, wo_out,
                 w_s, gbuf_ref, *, tiles_per_batch, row_chunk):
    i = pl.program_id(1)
    tm = h_ref.shape[0]
    for src, dst in ((wa_ref, wa_out), (wc_ref, wc_out), (wo_ref, wo_out)):
        dst[...] = src[...].astype(BF16)

    @pl.when(i == 0)
    def _():
        for n, w_ref in enumerate((wu_ref, wgb_ref, wgc_ref, wzc_ref)):
            _stage_weight(w_ref, w_s.at[n])

    @pl.when(i % tiles_per_batch == 0)
    def _():
        gbuf_ref[0:SUBLANES, :] = gm_ref[N_META - SUBLANES:N_META, :]

    for r in range(0, tm, row_chunk):
        h = h_ref[r:r + row_chunk, :]
        g = _dot_nt(h, w_s[2]) * _dot_nt(h, w_s[0])
        first = SUBLANES + r
        gbuf_ref[first:first + row_chunk, :] = g
        conv = (gbuf_ref[first - 2:first - 2 + row_chunk, :] * cw_ref[0:1, :]
                + gbuf_ref[first - 1:first - 1 + row_chunk, :] * cw_ref[1:2, :]
                + g * cw_ref[2:3, :])
        gb = _dot_nt(h, w_s[1])
        zc = _dot_nt(h, w_s[3])
        o_ref[r:r + row_chunk, :] = (gb * conv * _silu(zc)).astype(BF16)
    gbuf_ref[0:SUBLANES, :] = gbuf_ref[tm:tm + SUBLANES, :]


def _conv_call(h, wt, g_meta, conv_w, out_weights, rows_per_batch):
    tm, tc, row_chunk = TILES.conv_tm, TILES.conv_tc, TILES.conv_rows
    r = h.shape[0]
    nrow = r // tm
    steps = (D_MODEL // tc) * nrow
    slab = D_MODEL // steps
    assert slab * steps == D_MODEL and slab % (2 * SUBLANES) == 0
    w_specs = [_w_tile_spec(tc, lambda j, i, name=name: COL[name] + j * tc)
               for name in _CONV_GROUPS]
    slab_in = pl.BlockSpec((None, slab, D_MODEL), lambda j, i: (0, j * nrow + i, 0))
    slab_out = pl.BlockSpec((slab, D_MODEL), lambda j, i: (j * nrow + i, 0))
    cast_shape = jax.ShapeDtypeStruct((D_MODEL, D_MODEL), BF16)
    return pl.pallas_call(
        functools.partial(_conv_kernel, tiles_per_batch=rows_per_batch // tm, row_chunk=row_chunk),
        grid=(D_MODEL // tc, nrow),
        in_specs=[pl.BlockSpec((tm, D_MODEL), lambda j, i: (i, 0))] + w_specs + [
            pl.BlockSpec((N_META, tc), lambda j, i: (0, j)),
            pl.BlockSpec((CONV_K, tc), lambda j, i: (0, j)),
            slab_in, slab_in, slab_in,
        ],
        out_specs=[pl.BlockSpec((tm, tc), lambda j, i: (i, j)), slab_out, slab_out, slab_out],
        out_shape=[jax.ShapeDtypeStruct((r, D_MODEL), BF16), cast_shape, cast_shape, cast_shape],
        scratch_shapes=[pltpu.VMEM((len(_CONV_GROUPS), tc, D_MODEL), BF16),
                        pltpu.VMEM((SUBLANES + tm, tc), F32)],
        compiler_params=pltpu.CompilerParams(
            dimension_semantics=("arbitrary", "arbitrary"), vmem_limit_bytes=VMEM_LIMIT),
        name="conv_proj",
    )(h, *([wt] * len(_CONV_GROUPS)), g_meta, conv_w, *out_weights)


def _attn_kernel(q_ref, k_ref, v_ref, z_ref, km_ref, vm_ref, g_ref, gm_ref, eqk_ref,
                 o_ref, qa_ref, kf_ref, va_ref, tri_ref, s_ref, tmax_ref, m_ref, acc_ref,
                 *, tq, tk, ts, rb):
    seq = q_ref.shape[0]
    nq = seq // tq
    assert tq == 2 * tk and seq % tq == 0
    half = tk // 2

    tri_ref[...] = jnp.where(
        lax.broadcasted_iota(jnp.int32, (tk, tk), 1) <= lax.broadcasted_iota(jnp.int32, (tk, tk), 0),
        0.0, NEG_BIG)
    kma = jnp.concatenate(
        [km_ref[...], _dot(gm_ref[...], eqk_ref[:, LANES:]).astype(BF16)], axis=1)
    pad_lane = lax.broadcasted_iota(jnp.int32, (LANES - N_META, 2 * LANES), 1)
    k_meta_blk = jnp.concatenate(
        [kma, jnp.where(pad_lane == LANES + SPLIT_TERMS, NEG_BIG, 0.0).astype(BF16)],
        axis=0)
    v_meta_blk = jnp.concatenate(
        [vm_ref[...], jnp.ones((N_META, LANES), BF16)], axis=1)
    v_meta_blk = jnp.concatenate(
        [v_meta_blk, jnp.zeros((LANES - N_META, 2 * LANES), BF16)], axis=0)

    def setup(row0):
        rows = slice(row0, row0 + ts)
        feat = _dot(g_ref[rows, :], eqk_ref[...]).astype(BF16)
        kf_ref[rows, :] = feat[:, LANES:]
        qa_ref[rows, :] = jnp.concatenate([q_ref[rows, :], feat[:, :LANES]], axis=1)
        va_ref[rows, :] = jnp.concatenate([v_ref[rows, :], jnp.ones((ts, LANES), BF16)], axis=1)
        m_ref[rows, :] = jnp.full((ts, LANES), NEG_BIG, F32)
        acc_ref[rows, :] = jnp.zeros((ts, 2 * LANES), F32)

    def width(key0, nkeys):
        return nkeys + LANES if key0 == 0 else nkeys

    def stage_a(row0, nrows, key0, nkeys, tri, slot):
        keys = slice(key0, key0 + nkeys)
        ka = jnp.concatenate([k_ref[keys, :], kf_ref[keys, :]], axis=1)
        if key0 == 0:
            ka = jnp.concatenate([ka, k_meta_blk], axis=0)
        s = _dot_nt(qa_ref[row0:row0 + nrows, :], ka)
        w = width(key0, nkeys)
        for r0 in range(0, nrows, rb):
            part = s[r0:r0 + rb]
            if tri:
                masked = part[:, 0:nkeys] + tri_ref[r0:r0 + rb, 0:nkeys]
                part = jnp.concatenate([masked, part[:, nkeys:]], axis=1) if w > nkeys else masked
            s_ref[slot, r0:r0 + rb, 0:w] = part
            t = part[:, 0:LANES]
            for c in range(1, w // LANES):
                t = jnp.maximum(t, part[:, c * LANES:(c + 1) * LANES])
            tmax_ref[slot, r0:r0 + rb, :] = jnp.broadcast_to(
                jnp.max(t, axis=1, keepdims=True), (rb, LANES))

    def stage_b(row0, nrows, key0, nkeys, tri, slot):
        del tri
        w = width(key0, nkeys)
        alphas, ps = [], []
        for r0 in range(0, nrows, rb):
            rows = slice(row0 + r0, row0 + r0 + rb)
            m_old = m_ref[rows, :]
            m_new = jnp.maximum(m_old, tmax_ref[slot, r0:r0 + rb, :])
            m_ref[rows, :] = m_new
            alphas.append(jnp.exp2(m_old - m_new))
            ps.append(jnp.concatenate(
                [jnp.exp2(s_ref[slot, r0:r0 + rb, c * LANES:(c + 1) * LANES] - m_new).astype(BF16)
                 for c in range(w // LANES)], axis=1))
        va = va_ref[key0:key0 + nkeys, :]
        if key0 == 0:
            va = jnp.concatenate([va, v_meta_blk], axis=0)
        pv = _dot(jnp.concatenate(ps, axis=0), va)
        for n, r0 in enumerate(range(0, nrows, rb)):
            rows = slice(row0 + r0, row0 + r0 + rb)
            alpha = jnp.concatenate([alphas[n], alphas[n]], axis=1)
            acc_ref[rows, :] = alpha * acc_ref[rows, :] + pv[r0:r0 + rb]

    def diagonal(row0):
        return [(row0, half, row0, half, True),
                (row0 + half, half, row0, half, False),
                (row0 + half, half, row0 + half, half, True)]

    items = []
    for i in range(nq):
        top = i * tq
        items += [(top, tq, j * tk, tk, False) for j in range(2 * i)]
        items += [(top + tk, tk, top, tk, False)]
        items += diagonal(top) + diagonal(top + tk)

    def finalize(row0, nrows):
        for r0 in range(row0, row0 + nrows, rb):
            o = (acc_ref[r0:r0 + rb, :LANES] / acc_ref[r0:r0 + rb, LANES:]
                 * _silu(z_ref[r0:r0 + rb, :].astype(F32)))
            o_ref[r0:r0 + rb, :] = o.astype(BF16)

    rows_ready = [0]

    def scores(item, slot):
        row0, nrows, key0, nkeys, _ = item
        while rows_ready[0] < max(row0 + nrows, key0 + nkeys):
            setup(rows_ready[0])
            rows_ready[0] += ts
        stage_a(*item, slot)

    scores(items[0], 0)
    for n, item in enumerate(items):
        if n + 1 < len(items):
            scores(items[n + 1], 1 - n % 2)
        stage_b(*item, n % 2)
        if item[4]:
            finalize(item[0], item[1])


def _attn_call(proj, k_meta, v_meta, gate, gate_meta, eqk, batch, seq):
    tq, tk, ts, rb = TILES.attn_tq, TILES.attn_tk, TILES.attn_ts, TILES.attn_rb
    r = batch * seq
    head = lambda g: pl.BlockSpec((None, seq, HEAD_DIM), lambda b, h, g=g: (g, b, h))
    return pl.pallas_call(
        functools.partial(_attn_kernel, tq=tq, tk=tk, ts=ts, rb=rb),
        grid=(batch, HEADS),
        in_specs=[
            head(0), head(1), head(2), head(3),
            pl.BlockSpec((N_META, HEAD_DIM), lambda b, h: (0, h)),
            pl.BlockSpec((N_META, HEAD_DIM), lambda b, h: (0, h)),
            pl.BlockSpec((seq, LANES), lambda b, h: (b, 0)),
            pl.BlockSpec((N_META, LANES), lambda b, h: (0, 0)),
            pl.BlockSpec((None, LANES, 2 * LANES), lambda b, h: (h, 0, 0)),
        ],
        out_specs=pl.BlockSpec((seq, HEAD_DIM), lambda b, h: (b, h)),
        out_shape=jax.ShapeDtypeStruct((r, D_MODEL), BF16),
        scratch_shapes=[
            pltpu.VMEM((seq, 2 * LANES), BF16),
            pltpu.VMEM((seq, LANES), BF16),
            pltpu.VMEM((seq, 2 * LANES), BF16),
            pltpu.VMEM((tk, tk), F32),
            pltpu.VMEM((2, tq, tk + LANES), F32),
            pltpu.VMEM((2, tq, LANES), F32),
            pltpu.VMEM((seq, LANES), F32),
            pltpu.VMEM((seq, 2 * LANES), F32),
        ],
        compiler_params=pltpu.CompilerParams(
            dimension_semantics=("arbitrary", "arbitrary"), vmem_limit_bytes=VMEM_LIMIT),
        name="fox_attention",
    )(proj, proj, proj, proj, k_meta, v_meta, gate, gate_meta, eqk)


def _out_kernel(aa_ref, ac_ref, ma_ref, mc_ref, x_ref, wa_hbm, wc_hbm, wo_hbm, gain_ref, o_ref,
                w_s, sems):
    first = pl.program_id(0) == 0
    copies = [pltpu.make_async_copy(src, w_s.at[n], sems.at[n])
              for n, src in enumerate((wa_hbm, wc_hbm, wo_hbm))]

    @pl.when(first)
    def _():
        for copy in copies:
            copy.start()

    def step(wait):
        if wait:
            copies[0].wait()
        ya = _dot(aa_ref[...], w_s[0])
        if wait:
            copies[1].wait()
        yc = _dot(ac_ref[...], w_s[1])
        merged = (jax.nn.sigmoid(ma_ref[...].astype(F32)) * ya
                  + jax.nn.sigmoid(mc_ref[...].astype(F32)) * yc)
        if wait:
            copies[2].wait()
        y = x_ref[...] + _dot(merged.astype(BF16), w_s[2])
        o_ref[...] = _rmsnorm(y, gain_ref[...])

    pl.when(first)(lambda: step(True))
    pl.when(jnp.logical_not(first))(lambda: step(False))


def _out_call(a_att, a_conv, proj, x2d, wa, wc, wo, gain):
    tm = TILES.out_tm
    r = x2d.shape[0]
    row = lambda i: (i, 0)
    act = pl.BlockSpec((tm, D_MODEL), row)
    gate = lambda name: pl.BlockSpec((None, tm, D_MODEL),
                                     lambda i, g=_PROJ_GROUPS.index(name): (g, i, 0))
    weight = pl.BlockSpec(memory_space=pl.ANY)
    return pl.pallas_call(
        _out_kernel,
        grid=(r // tm,),
        in_specs=[act, act, gate("m_att"), gate("m_conv"), act, weight, weight, weight,
                  pl.BlockSpec((1, D_MODEL), lambda i: (0, 0))],
        out_specs=act,
        out_shape=jax.ShapeDtypeStruct((r, D_MODEL), F32),
        scratch_shapes=[pltpu.VMEM((3, D_MODEL, D_MODEL), BF16), pltpu.SemaphoreType.DMA((3,))],
        compiler_params=pltpu.CompilerParams(
            dimension_semantics=("arbitrary",), vmem_limit_bytes=VMEM_LIMIT),
        name="out_proj",
    )(a_att, a_conv, proj, proj, x2d, wa, wc, wo, gain)


def kernel(x, meta_tokens, norm_gain, w_in, b_f, conv_w, w_att_o, w_conv_o, w_out, final_gain):
    batch, seq, _ = x.shape
    assert x.shape == (batch, seq, D_MODEL) and x.dtype == F32
    assert seq % max(TILES.attn_tq, TILES.conv_tm, TILES.norm_tm) == 0
    assert meta_tokens.shape == (N_META, D_MODEL) and b_f.shape == (1, HEADS)
    assert norm_gain.shape[0] == 1, "single-layer block"
    assert w_in.shape == (1, D_MODEL, N_IN) and w_in.dtype == F32
    wt = jnp.transpose(w_in[0])
    bf = jnp.pad(b_f[0].astype(F32), (0, LANES - HEADS)).reshape(1, LANES)
    gain = norm_gain[0].reshape(1, D_MODEL).astype(F32)
    fgain = final_gain.reshape(1, D_MODEL).astype(F32)
    place, ones, eqk = _feature_constants()

    k_meta, v_meta, g_meta, cm, gate_meta = _meta_call(
        meta_tokens.astype(F32), gain, wt, bf, place, ones)
    h, gate = _norm_call(x, gain, wt, bf, cm, place, ones)
    proj = _proj_call(h, wt)
    a_conv, wa, wc, wo = _conv_call(h, wt, g_meta, conv_w[0].astype(F32),
                                    (w_att_o, w_conv_o, w_out), seq)
    a_att = _attn_call(proj, k_meta, v_meta, gate, gate_meta, eqk, batch, seq)
    out = _out_call(a_att, a_conv, proj, x.reshape(batch * seq, D_MODEL), wa, wc, wo, fgain)
    return out.reshape(batch, seq, D_MODEL)
```

```python
import functools
from typing import NamedTuple

import numpy as np

import jax
import jax.numpy as jnp
from jax import lax
from jax.experimental import pallas as pl
from jax.experimental.pallas import tpu as pltpu

D_MODEL = 2048
N_META = 16
HEADS = 16
HEAD_DIM = 128
CONV_K = 3
EPS = 1e-6
LANES = 128
SUBLANES = 8
NEG_BIG = -1e30
LOG2E = 1.4426950408889634
GATE_LANES = LANES // HEADS
SPLIT_TERMS = 3
VMEM_LIMIT = 56 * 1024 * 1024


class Tiles(NamedTuple):
    meta_tn: int = 256
    norm_tm: int = 1024
    norm_chunk: int = 256
    proj_tm: int = 2048
    proj_tn: int = 1024
    proj_rows: int = 512
    conv_tm: int = 2048
    conv_tc: int = 256
    conv_rows: int = 512
    attn_tq: int = 1024
    attn_tk: int = 512
    attn_ts: int = 512
    attn_rb: int = 128
    out_tm: int = 256


TILES = Tiles()

_GROUPS = ("q", "k", "v", "f", "z_att", "u", "gate_b", "gate_c", "z_conv", "m_att", "m_conv")
_WIDTHS = (D_MODEL, D_MODEL, D_MODEL, HEADS) + (D_MODEL,) * 7
COL = {name: sum(_WIDTHS[:n]) for n, name in enumerate(_GROUPS)}
N_IN = sum(_WIDTHS)

F32 = jnp.float32
BF16 = jnp.bfloat16


def _dot(a, b):
    return jnp.dot(a, b, preferred_element_type=F32)


def _dot_nt(a, b):
    return lax.dot_general(a, b, (((1,), (1,)), ((), ())), preferred_element_type=F32)


def _rmsnorm(x, gain):
    return x * lax.rsqrt(jnp.mean(x * x, axis=-1, keepdims=True) + EPS) * gain


def _log_sigmoid(x):
    return jnp.minimum(x, 0.0) - jnp.log1p(jnp.exp(-jnp.abs(x)))


def _silu(x):
    return x * jax.nn.sigmoid(x)


def _split3(x):
    hi = x.astype(BF16)
    r1 = x - hi.astype(F32)
    mid = r1.astype(BF16)
    lo = (r1 - mid.astype(F32)).astype(BF16)
    return hi, mid, lo


def _cumsum_rows(x):
    n = x.shape[0]
    tri = (lax.broadcasted_iota(jnp.int32, (n, n), 0)
           >= lax.broadcasted_iota(jnp.int32, (n, n), 1)).astype(BF16)
    hi, mid, lo = _split3(x)
    return _dot(tri, hi) + _dot(tri, mid) + _dot(tri, lo)


def _gate_features(c, place_ref, ones_ref):
    hi, mid, lo = _split3(c * LOG2E)
    cat = jnp.concatenate([hi, mid, lo], axis=1)
    return (_dot(cat, place_ref[...]) + ones_ref[...]).astype(BF16)


def _feature_constants():
    n = SPLIT_TERMS
    assert n + 1 <= GATE_LANES
    place = np.zeros((n * LANES, LANES), np.float32)
    ones = np.zeros((1, LANES), np.float32)
    eq = np.zeros((HEADS, LANES, LANES), np.float32)
    ek = np.zeros((HEADS, LANES, LANES), np.float32)
    for h in range(HEADS):
        base = GATE_LANES * h
        ones[0, base + n] = 1.0
        for f in range(n):
            place[f * LANES + h, base + f] = 1.0
            eq[h, base + f, f] = 1.0
            eq[h, base + n, n + f] = 1.0
            ek[h, base + n, f] = 1.0
            ek[h, base + f, n + f] = -1.0
    return (jnp.asarray(place, BF16), jnp.asarray(ones, F32),
            jnp.asarray(np.concatenate([eq, ek], axis=2), BF16))


def _w_tile_spec(tn, row_of):
    assert all(c % HEADS == 0 for c in COL.values()) and tn % HEADS == 0
    return pl.BlockSpec((pl.Element(tn), pl.Element(D_MODEL)),
                        lambda *ids: (pl.multiple_of(row_of(*ids), HEADS), 0))


def _forget_w_spec():
    assert COL["f"] % LANES == 0
    return pl.BlockSpec((LANES, D_MODEL), lambda *ids: (COL["f"] // LANES, 0))


def _stage_weight(src_ref, dst_ref, chunk=128):
    for r in range(0, src_ref.shape[0], chunk):
        dst_ref[r:r + chunk, :] = src_ref[r:r + chunk, :].astype(BF16)


def _meta_kernel(meta_ref, gain_ref, wk_ref, wv_ref, wu_ref, wgc_ref, wf_ref, bf_ref,
                 place_ref, ones_ref, k_ref, v_ref, g_ref, cm_ref, gm_ref):
    h = _rmsnorm(meta_ref[...], gain_ref[...]).astype(BF16)
    proj = lambda w_ref: _dot_nt(h, w_ref[...].astype(BF16))
    k_ref[...] = proj(wk_ref).astype(BF16)
    v_ref[...] = proj(wv_ref).astype(BF16)
    g_ref[...] = proj(wgc_ref) * proj(wu_ref)
    cm = _cumsum_rows(_log_sigmoid(proj(wf_ref) + bf_ref[...]))
    cm_ref[...] = cm
    gm_ref[...] = _gate_features(cm, place_ref, ones_ref)


def _meta_call(meta, gain, wt, bf, place, ones):
    tn = TILES.meta_tn
    nt = D_MODEL // tn
    const = lambda shape: pl.BlockSpec(shape, lambda j: (0,) * len(shape))
    group = lambda name: _w_tile_spec(tn, lambda j: COL[name] + j * tn)
    return pl.pallas_call(
        _meta_kernel,
        grid=(nt,),
        in_specs=[
            const((N_META, D_MODEL)), const((1, D_MODEL)),
            group("k"), group("v"), group("u"), group("gate_c"), _forget_w_spec(),
            const((1, LANES)), const((SPLIT_TERMS * LANES, LANES)), const((1, LANES)),
        ],
        out_specs=[
            pl.BlockSpec((N_META, tn), lambda j: (0, j)),
            pl.BlockSpec((N_META, tn), lambda j: (0, j)),
            pl.BlockSpec((N_META, tn), lambda j: (0, j)),
            const((N_META, LANES)),
            const((N_META, LANES)),
        ],
        out_shape=[
            jax.ShapeDtypeStruct((N_META, D_MODEL), BF16),
            jax.ShapeDtypeStruct((N_META, D_MODEL), BF16),
            jax.ShapeDtypeStruct((N_META, D_MODEL), F32),
            jax.ShapeDtypeStruct((N_META, LANES), F32),
            jax.ShapeDtypeStruct((N_META, LANES), BF16),
        ],
        compiler_params=pltpu.CompilerParams(
            dimension_semantics=("arbitrary",), vmem_limit_bytes=VMEM_LIMIT),
        name="meta_proj",
    )(meta, gain, wt, wt, wt, wt, wt, bf, place, ones)


def _norm_kernel(x_ref, gain_ref, wf_ref, bf_ref, cm_ref, place_ref, ones_ref,
                 h_ref, gate_ref, carry_ref, *, chunk):
    @pl.when(pl.program_id(1) == 0)
    def _():
        carry_ref[...] = cm_ref[N_META - 1:N_META, :]

    h = _rmsnorm(x_ref[...], gain_ref[...]).astype(BF16)
    h_ref[...] = h
    logf = _log_sigmoid(_dot_nt(h, wf_ref[...].astype(BF16)) + bf_ref[...])
    tm = logf.shape[0]
    run = carry_ref[...]
    parts = []
    for j in range(tm // chunk):
        c = _cumsum_rows(logf[j * chunk:(j + 1) * chunk]) + run
        run = c[chunk - 1:chunk, :]
        parts.append(c)
    carry_ref[...] = run
    gate_ref[...] = _gate_features(jnp.concatenate(parts, axis=0), place_ref, ones_ref)


def _norm_call(x, gain, wt, bf, cm, place, ones):
    tm, chunk = TILES.norm_tm, TILES.norm_chunk
    b, s, _ = x.shape
    nt = s // tm
    const = lambda shape: pl.BlockSpec(shape, lambda bi, i: (0,) * len(shape))
    return pl.pallas_call(
        functools.partial(_norm_kernel, chunk=chunk),
        grid=(b, nt),
        in_specs=[
            pl.BlockSpec((None, tm, D_MODEL), lambda bi, i: (bi, i, 0)),
            const((1, D_MODEL)),
            _forget_w_spec(),
            const((1, LANES)),
            const((N_META, LANES)),
            const((SPLIT_TERMS * LANES, LANES)),
            const((1, LANES)),
        ],
        out_specs=[
            pl.BlockSpec((tm, D_MODEL), lambda bi, i: (bi * nt + i, 0)),
            pl.BlockSpec((tm, LANES), lambda bi, i: (bi * nt + i, 0)),
        ],
        out_shape=[
            jax.ShapeDtypeStruct((b * s, D_MODEL), BF16),
            jax.ShapeDtypeStruct((b * s, LANES), BF16),
        ],
        scratch_shapes=[pltpu.VMEM((1, LANES), F32)],
        compiler_params=pltpu.CompilerParams(
            dimension_semantics=("arbitrary", "arbitrary"), vmem_limit_bytes=VMEM_LIMIT),
        name="norm_forget",
    )(x, gain, wt, bf, cm, place, ones)


_PROJ_GROUPS = ("q", "k", "v", "z_att", "m_att", "m_conv")


def _proj_group_row(g):
    row = jnp.int32(COL[_PROJ_GROUPS[0]])
    for n in range(1, len(_PROJ_GROUPS)):
        row = jnp.where(g == n, COL[_PROJ_GROUPS[n]], row)
    return row


def _proj_kernel(h_ref, w_ref, o_ref, w_s, *, row_chunk):
    @pl.when(pl.program_id(2) == 0)
    def _():
        _stage_weight(w_ref, w_s)

    scale = jnp.where(pl.program_id(0) == 0, HEAD_DIM ** -0.5 * LOG2E, 1.0).astype(F32)
    for r in range(0, h_ref.shape[0], row_chunk):
        y = _dot_nt(h_ref[r:r + row_chunk, :], w_s[...])
        o_ref[r:r + row_chunk, :] = (y * scale).astype(BF16)


def _proj_call(h, wt):
    tm, tn, row_chunk = TILES.proj_tm, TILES.proj_tn, TILES.proj_rows
    r = h.shape[0]
    ng = len(_PROJ_GROUPS)
    return pl.pallas_call(
        functools.partial(_proj_kernel, row_chunk=row_chunk),
        grid=(ng, D_MODEL // tn, r // tm),
        in_specs=[
            pl.BlockSpec((tm, D_MODEL), lambda g, j, i: (i, 0)),
            _w_tile_spec(tn, lambda g, j, i: _proj_group_row(g) + j * tn),
        ],
        out_specs=pl.BlockSpec((None, tm, tn), lambda g, j, i: (g, i, j)),
        out_shape=jax.ShapeDtypeStruct((ng, r, D_MODEL), BF16),
        scratch_shapes=[pltpu.VMEM((tn, D_MODEL), BF16)],
        compiler_params=pltpu.CompilerParams(
            dimension_semantics=("arbitrary", "arbitrary", "arbitrary"),
            vmem_limit_bytes=VMEM_LIMIT),
        name="plain_proj",
    )(h, wt)


_CONV_GROUPS = ("u", "gate_b", "gate_c", "z_conv")


def _conv_kernel(h_ref, wu_ref, wgb_ref, wgc_ref, wzc_ref, gm_ref, cw_ref,
                 wa_ref, wc_ref, wo_ref, o_ref, wa_out, wc_out, wo_out,
                 w_s, gbuf_ref, *, tiles_per_batch, row_chunk):
    i = pl.program_id(1)
    tm = h_ref.shape[0]
    for src, dst in ((wa_ref, wa_out), (wc_ref, wc_out), (wo_ref, wo_out)):
        dst[...] = src[...].astype(BF16)

    @pl.when(i == 0)
    def _():
        for n, w_ref in enumerate((wu_ref, wgb_ref, wgc_ref, wzc_ref)):
            _stage_weight(w_ref, w_s.at[n])

    @pl.when(i % tiles_per_batch == 0)
    def _():
        gbuf_ref[0:SUBLANES, :] = gm_ref[N_META - SUBLANES:N_META, :]

    for r in range(0, tm, row_chunk):
        h = h_ref[r:r + row_chunk, :]
        g = _dot_nt(h, w_s[2]) * _dot_nt(h, w_s[0])
        first = SUBLANES + r
        gbuf_ref[first:first + row_chunk, :] = g
        conv = (gbuf_ref[first - 2:first - 2 + row_chunk, :] * cw_ref[0:1, :]
                + gbuf_ref[first - 1:first - 1 + row_chunk, :] * cw_ref[1:2, :]
                + g * cw_ref[2:3, :])
        gb = _dot_nt(h, w_s[1])
        zc = _dot_nt(h, w_s[3])
        o_ref[r:r + row_chunk, :] = (gb * conv * _silu(zc)).astype(BF16)
    gbuf_ref[0:SUBLANES, :] = gbuf_ref[tm:tm + SUBLANES, :]


def _conv_call(h, wt, g_meta, conv_w, out_weights, rows_per_batch):
    tm, tc, row_chunk = TILES.conv_tm, TILES.conv_tc, TILES.conv_rows
    r = h.shape[0]
    nrow = r // tm
    steps = (D_MODEL // tc) * nrow
    slab = D_MODEL // steps
    assert slab * steps == D_MODEL and slab % (2 * SUBLANES) == 0
    w_specs = [_w_tile_spec(tc, lambda j, i, name=name: COL[name] + j * tc)
               for name in _CONV_GROUPS]
    slab_in = pl.BlockSpec((None, slab, D_MODEL), lambda j, i: (0, j * nrow + i, 0))
    slab_out = pl.BlockSpec((slab, D_MODEL), lambda j, i: (j * nrow + i, 0))
    cast_shape = jax.ShapeDtypeStruct((D_MODEL, D_MODEL), BF16)
    return pl.pallas_call(
        functools.partial(_conv_kernel, tiles_per_batch=rows_per_batch // tm, row_chunk=row_chunk),
        grid=(D_MODEL // tc, nrow),
        in_specs=[pl.BlockSpec((tm, D_MODEL), lambda j, i: (i, 0))] + w_specs + [
            pl.BlockSpec((N_META, tc), lambda j, i: (0, j)),
            pl.BlockSpec((None, CONV_K, tc), lambda j, i: (0, 0, j)),
            slab_in, slab_in, slab_in,
        ],
        out_specs=[pl.BlockSpec((tm, tc), lambda j, i: (i, j)), slab_out, slab_out, slab_out],
        out_shape=[jax.ShapeDtypeStruct((r, D_MODEL), BF16), cast_shape, cast_shape, cast_shape],
        scratch_shapes=[pltpu.VMEM((len(_CONV_GROUPS), tc, D_MODEL), BF16),
                        pltpu.VMEM((SUBLANES + tm, tc), F32)],
        compiler_params=pltpu.CompilerParams(
            dimension_semantics=("arbitrary", "arbitrary"), vmem_limit_bytes=VMEM_LIMIT),
        name="conv_proj",
    )(h, *([wt] * len(_CONV_GROUPS)), g_meta, conv_w, *out_weights)


def _attn_kernel(q_ref, k_ref, v_ref, z_ref, km_ref, vm_ref, g_ref, gm_ref, eqk_ref,
                 o_ref, qa_ref, kf_ref, va_ref, tri_ref, s_ref, tmax_ref, m_ref, acc_ref,
                 *, tq, tk, ts, rb):
    seq = q_ref.shape[0]
    nq = seq // tq
    assert tq == 2 * tk and seq % tq == 0
    half = tk // 2

    tri_ref[...] = jnp.where(
        lax.broadcasted_iota(jnp.int32, (tk, tk), 1) <= lax.broadcasted_iota(jnp.int32, (tk, tk), 0),
        0.0, NEG_BIG)
    kma = jnp.concatenate(
        [km_ref[...], _dot(gm_ref[...], eqk_ref[:, LANES:]).astype(BF16)], axis=1)
    pad_lane = lax.broadcasted_iota(jnp.int32, (LANES - N_META, 2 * LANES), 1)
    k_meta_blk = jnp.concatenate(
        [kma, jnp.where(pad_lane == LANES + SPLIT_TERMS, NEG_BIG, 0.0).astype(BF16)],
        axis=0)
    v_meta_blk = jnp.concatenate(
        [vm_ref[...], jnp.ones((N_META, LANES), BF16)], axis=1)
    v_meta_blk = jnp.concatenate(
        [v_meta_blk, jnp.zeros((LANES - N_META, 2 * LANES), BF16)], axis=0)

    def setup(row0):
        rows = slice(row0, row0 + ts)
        feat = _dot(g_ref[rows, :], eqk_ref[...]).astype(BF16)
        kf_ref[rows, :] = feat[:, LANES:]
        qa_ref[rows, :] = jnp.concatenate([q_ref[rows, :], feat[:, :LANES]], axis=1)
        va_ref[rows, :] = jnp.concatenate([v_ref[rows, :], jnp.ones((ts, LANES), BF16)], axis=1)
        m_ref[rows, :] = jnp.full((ts, LANES), NEG_BIG, F32)
        acc_ref[rows, :] = jnp.zeros((ts, 2 * LANES), F32)

    def width(key0, nkeys):
        return nkeys + LANES if key0 == 0 else nkeys

    def stage_a(row0, nrows, key0, nkeys, tri, slot):
        keys = slice(key0, key0 + nkeys)
        ka = jnp.concatenate([k_ref[keys, :], kf_ref[keys, :]], axis=1)
        if key0 == 0:
            ka = jnp.concatenate([ka, k_meta_blk], axis=0)
        s = _dot_nt(qa_ref[row0:row0 + nrows, :], ka)
        w = width(key0, nkeys)
        for r0 in range(0, nrows, rb):
            part = s[r0:r0 + rb]
            if tri:
                masked = part[:, 0:nkeys] + tri_ref[r0:r0 + rb, 0:nkeys]
                part = jnp.concatenate([masked, part[:, nkeys:]], axis=1) if w > nkeys else masked
            s_ref[slot, r0:r0 + rb, 0:w] = part
            t = part[:, 0:LANES]
            for c in range(1, w // LANES):
                t = jnp.maximum(t, part[:, c * LANES:(c + 1) * LANES])
            tmax_ref[slot, r0:r0 + rb, :] = jnp.broadcast_to(
                jnp.max(t, axis=1, keepdims=True), (rb, LANES))

    def stage_b(row0, nrows, key0, nkeys, tri, slot):
        del tri
        w = width(key0, nkeys)
        alphas, ps = [], []
        for r0 in range(0, nrows, rb):
            rows = slice(row0 + r0, row0 + r0 + rb)
            m_old = m_ref[rows, :]
            m_new = jnp.maximum(m_old, tmax_ref[slot, r0:r0 + rb, :])
            m_ref[rows, :] = m_new
            alphas.append(jnp.exp2(m_old - m_new))
            ps.append(jnp.concatenate(
                [jnp.exp2(s_ref[slot, r0:r0 + rb, c * LANES:(c + 1) * LANES] - m_new).astype(BF16)
                 for c in range(w // LANES)], axis=1))
        va = va_ref[key0:key0 + nkeys, :]
        if key0 == 0:
            va = jnp.concatenate([va, v_meta_blk], axis=0)
        pv = _dot(jnp.concatenate(ps, axis=0), va)
        for n, r0 in enumerate(range(0, nrows, rb)):
            rows = slice(row0 + r0, row0 + r0 + rb)
            alpha = jnp.concatenate([alphas[n], alphas[n]], axis=1)
            acc_ref[rows, :] = alpha * acc_ref[rows, :] + pv[r0:r0 + rb]

    def diagonal(row0):
        return [(row0, half, row0, half, True),
                (row0 + half, half, row0, half, False),
                (row0 + half, half, row0 + half, half, True)]

    items = []
    for i in range(nq):
        top = i * tq
        items += [(top, tq, j * tk, tk, False) for j in range(2 * i)]
        items += [(top + tk, tk, top, tk, False)]
        items += diagonal(top) + diagonal(top + tk)

    def finalize(row0, nrows):
        for r0 in range(row0, row0 + nrows, rb):
            o = (acc_ref[r0:r0 + rb, :LANES] / acc_ref[r0:r0 + rb, LANES:]
                 * _silu(z_ref[r0:r0 + rb, :].astype(F32)))
            o_ref[r0:r0 + rb, :] = o.astype(BF16)

    rows_ready = [0]

    def scores(item, slot):
        row0, nrows, key0, nkeys, _ = item
        while rows_ready[0] < max(row0 + nrows, key0 + nkeys):
            setup(rows_ready[0])
            rows_ready[0] += ts
        stage_a(*item, slot)

    scores(items[0], 0)
    for n, item in enumerate(items):
        if n + 1 < len(items):
            scores(items[n + 1], 1 - n % 2)
        stage_b(*item, n % 2)
        if item[4]:
            finalize(item[0], item[1])


def _attn_call(proj, k_meta, v_meta, gate, gate_meta, eqk, batch, seq):
    tq, tk, ts, rb = TILES.attn_tq, TILES.attn_tk, TILES.attn_ts, TILES.attn_rb
    r = batch * seq
    head = lambda g: pl.BlockSpec((None, seq, HEAD_DIM), lambda b, h, g=g: (g, b, h))
    return pl.pallas_call(
        functools.partial(_attn_kernel, tq=tq, tk=tk, ts=ts, rb=rb),
        grid=(batch, HEADS),
        in_specs=[
            head(0), head(1), head(2), head(3),
            pl.BlockSpec((N_META, HEAD_DIM), lambda b, h: (0, h)),
            pl.BlockSpec((N_META, HEAD_DIM), lambda b, h: (0, h)),
            pl.BlockSpec((seq, LANES), lambda b, h: (b, 0)),
            pl.BlockSpec((N_META, LANES), lambda b, h: (0, 0)),
            pl.BlockSpec((None, LANES, 2 * LANES), lambda b, h: (h, 0, 0)),
        ],
        out_specs=pl.BlockSpec((seq, HEAD_DIM), lambda b, h: (b, h)),
        out_shape=jax.ShapeDtypeStruct((r, D_MODEL), BF16),
        scratch_shapes=[
            pltpu.VMEM((seq, 2 * LANES), BF16),
            pltpu.VMEM((seq, LANES), BF16),
            pltpu.VMEM((seq, 2 * LANES), BF16),
            pltpu.VMEM((tk, tk), F32),
            pltpu.VMEM((2, tq, tk + LANES), F32),
            pltpu.VMEM((2, tq, LANES), F32),
            pltpu.VMEM((seq, LANES), F32),
            pltpu.VMEM((seq, 2 * LANES), F32),
        ],
        compiler_params=pltpu.CompilerParams(
            dimension_semantics=("arbitrary", "arbitrary"), vmem_limit_bytes=VMEM_LIMIT),
        name="fox_attention",
    )(proj, proj, proj, proj, k_meta, v_meta, gate, gate_meta, eqk)


def _out_kernel(aa_ref, ac_ref, ma_ref, mc_ref, x_ref, wa_hbm, wc_hbm, wo_hbm, gain_ref, o_ref,
                w_s, sems):
    first = pl.program_id(0) == 0
    copies = [pltpu.make_async_copy(src, w_s.at[n], sems.at[n])
              for n, src in enumerate((wa_hbm, wc_hbm, wo_hbm))]

    @pl.when(first)
    def _():
        for copy in copies:
            copy.start()

    def step(wait):
        if wait:
            copies[0].wait()
        ya = _dot(aa_ref[...], w_s[0])
        if wait:
            copies[1].wait()
        yc = _dot(ac_ref[...], w_s[1])
        merged = (jax.nn.sigmoid(ma_ref[...].astype(F32)) * ya
                  + jax.nn.sigmoid(mc_ref[...].astype(F32)) * yc)
        if wait:
            copies[2].wait()
        y = x_ref[...] + _dot(merged.astype(BF16), w_s[2])
        o_ref[...] = _rmsnorm(y, gain_ref[...])

    pl.when(first)(lambda: step(True))
    pl.when(jnp.logical_not(first))(lambda: step(False))


def _out_call(a_att, a_conv, proj, x2d, wa, wc, wo, gain):
    tm = TILES.out_tm
    r = x2d.shape[0]
    row = lambda i: (i, 0)
    act = pl.BlockSpec((tm, D_MODEL), row)
    gate = lambda name: pl.BlockSpec((None, tm, D_MODEL),
                                     lambda i, g=_PROJ_GROUPS.index(name): (g, i, 0))
    weight = pl.BlockSpec(memory_space=pl.ANY)
    return pl.pallas_call(
        _out_kernel,
        grid=(r // tm,),
        in_specs=[act, act, gate("m_att"), gate("m_conv"), act, weight, weight, weight,
                  pl.BlockSpec((1, D_MODEL), lambda i: (0, 0))],
        out_specs=act,
        out_shape=jax.ShapeDtypeStruct((r, D_MODEL), F32),
        scratch_shapes=[pltpu.VMEM((3, D_MODEL, D_MODEL), BF16), pltpu.SemaphoreType.DMA((3,))],
        compiler_params=pltpu.CompilerParams(
            dimension_semantics=("arbitrary",), vmem_limit_bytes=VMEM_LIMIT),
        name="out_proj",
    )(a_att, a_conv, proj, proj, x2d, wa, wc, wo, gain)


def kernel(x, meta_tokens, norm_gain, w_in, b_f, conv_w, w_att_o, w_conv_o, w_out, final_gain):
    batch, seq, _ = x.shape
    assert x.shape == (batch, seq, D_MODEL) and x.dtype == F32
    assert seq % max(TILES.attn_tq, TILES.conv_tm, TILES.norm_tm) == 0
    assert meta_tokens.shape == (N_META, D_MODEL) and b_f.shape == (1, HEADS)
    assert norm_gain.shape[0] == 1, "single-layer block"
    assert w_in.shape == (1, D_MODEL, N_IN) and w_in.dtype == F32
    wt = jnp.transpose(w_in[0])
    bf = jnp.pad(b_f[0].astype(F32), (0, LANES - HEADS)).reshape(1, LANES)
    gain = norm_gain[0].reshape(1, D_MODEL).astype(F32)
    fgain = final_gain.reshape(1, D_MODEL).astype(F32)
    place, ones, eqk = _feature_constants()

    k_meta, v_meta, g_meta, cm, gate_meta = _meta_call(
        meta_tokens.astype(F32), gain, wt, bf, place, ones)
    h, gate = _norm_call(x, gain, wt, bf, cm, place, ones)
    proj = _proj_call(h, wt)
    assert conv_w.shape == (1, CONV_K, D_MODEL) and conv_w.dtype == F32
    a_conv, wa, wc, wo = _conv_call(h, wt, g_meta, conv_w,
                                    (w_att_o, w_conv_o, w_out), seq)
    a_att = _attn_call(proj, k_meta, v_meta, gate, gate_meta, eqk, batch, seq)
    out = _out_call(a_att, a_conv, proj, x.reshape(batch * seq, D_MODEL), wa, wc, wo, fgain)
    return out.reshape(batch, seq, D_MODEL)
```

```python
import functools
from typing import NamedTuple

import numpy as np

import jax
import jax.numpy as jnp
from jax import lax
from jax.experimental import pallas as pl
from jax.experimental.pallas import tpu as pltpu

D_MODEL = 2048
N_META = 16
HEADS = 16
HEAD_DIM = 128
CONV_K = 3
EPS = 1e-6
LANES = 128
SUBLANES = 8
NEG_BIG = -1e30
LOG2E = 1.4426950408889634
GATE_LANES = LANES // HEADS
SPLIT_TERMS = 3
VMEM_LIMIT = 56 * 1024 * 1024


class Tiles(NamedTuple):
    meta_tn: int = 512
    norm_tm: int = 1024
    norm_chunk: int = 256
    proj_tm: int = 2048
    proj_tn: int = 1024
    proj_rows: int = 512
    conv_tm: int = 2048
    conv_tc: int = 256
    conv_rows: int = 512
    attn_tq: int = 1024
    attn_tk: int = 512
    attn_ts: int = 512
    attn_rb: int = 128
    out_tm: int = 256


TILES = Tiles()

_GROUPS = ("q", "k", "v", "f", "z_att", "u", "gate_b", "gate_c", "z_conv", "m_att", "m_conv")
_WIDTHS = (D_MODEL, D_MODEL, D_MODEL, HEADS) + (D_MODEL,) * 7
COL = {name: sum(_WIDTHS[:n]) for n, name in enumerate(_GROUPS)}
N_IN = sum(_WIDTHS)

F32 = jnp.float32
BF16 = jnp.bfloat16


def _dot(a, b):
    return jnp.dot(a, b, preferred_element_type=F32)


def _dot_nt(a, b):
    return lax.dot_general(a, b, (((1,), (1,)), ((), ())), preferred_element_type=F32)


def _rmsnorm(x, gain):
    return x * lax.rsqrt(jnp.mean(x * x, axis=-1, keepdims=True) + EPS) * gain


def _log_sigmoid(x):
    return jnp.minimum(x, 0.0) - jnp.log1p(jnp.exp(-jnp.abs(x)))


def _silu(x):
    return x * jax.nn.sigmoid(x)


def _split3(x):
    hi = x.astype(BF16)
    r1 = x - hi.astype(F32)
    mid = r1.astype(BF16)
    lo = (r1 - mid.astype(F32)).astype(BF16)
    return hi, mid, lo


def _cumsum_rows(x):
    n = x.shape[0]
    tri = (lax.broadcasted_iota(jnp.int32, (n, n), 0)
           >= lax.broadcasted_iota(jnp.int32, (n, n), 1)).astype(BF16)
    hi, mid, lo = _split3(x)
    return _dot(tri, hi) + _dot(tri, mid) + _dot(tri, lo)


def _gate_features(c, place_ref, ones_ref):
    hi, mid, lo = _split3(c * LOG2E)
    cat = jnp.concatenate([hi, mid, lo], axis=1)
    return (_dot(cat, place_ref[...]) + ones_ref[...]).astype(BF16)


def _feature_constants():
    n = SPLIT_TERMS
    assert n + 1 <= GATE_LANES
    place = np.zeros((n * LANES, LANES), np.float32)
    ones = np.zeros((1, LANES), np.float32)
    eq = np.zeros((HEADS, LANES, LANES), np.float32)
    ek = np.zeros((HEADS, LANES, LANES), np.float32)
    for h in range(HEADS):
        base = GATE_LANES * h
        ones[0, base + n] = 1.0
        for f in range(n):
            place[f * LANES + h, base + f] = 1.0
            eq[h, base + f, f] = 1.0
            eq[h, base + n, n + f] = 1.0
            ek[h, base + n, f] = 1.0
            ek[h, base + f, n + f] = -1.0
    return (jnp.asarray(place, BF16), jnp.asarray(ones, F32),
            jnp.asarray(np.concatenate([eq, ek], axis=2), BF16))


def _w_tile_spec(tn, row_of):
    assert all(c % HEADS == 0 for c in COL.values()) and tn % HEADS == 0
    return pl.BlockSpec((pl.Element(tn), pl.Element(D_MODEL)),
                        lambda *ids: (pl.multiple_of(row_of(*ids), HEADS), 0))


def _forget_w_spec():
    assert COL["f"] % LANES == 0
    return pl.BlockSpec((LANES, D_MODEL), lambda *ids: (COL["f"] // LANES, 0))


def _stage_weight(src_ref, dst_ref, chunk=128):
    for r in range(0, src_ref.shape[0], chunk):
        dst_ref[r:r + chunk, :] = src_ref[r:r + chunk, :].astype(BF16)


def _meta_kernel(meta_ref, gain_ref, wk_ref, wv_ref, wu_ref, wgc_ref, wf_ref, bf_ref,
                 place_ref, ones_ref, k_ref, v_ref, g_ref, cm_ref, gm_ref):
    h = _rmsnorm(meta_ref[...], gain_ref[...]).astype(BF16)
    proj = lambda w_ref: _dot_nt(h, w_ref[...].astype(BF16))
    k_ref[...] = proj(wk_ref).astype(BF16)
    v_ref[...] = proj(wv_ref).astype(BF16)
    g_ref[...] = proj(wgc_ref) * proj(wu_ref)
    cm = _cumsum_rows(_log_sigmoid(proj(wf_ref) + bf_ref[...]))
    cm_ref[...] = cm
    gm_ref[...] = _gate_features(cm, place_ref, ones_ref)


def _meta_call(meta, gain, wt, bf, place, ones):
    tn = TILES.meta_tn
    nt = D_MODEL // tn
    const = lambda shape: pl.BlockSpec(shape, lambda j: (0,) * len(shape))
    group = lambda name: _w_tile_spec(tn, lambda j: COL[name] + j * tn)
    return pl.pallas_call(
        _meta_kernel,
        grid=(nt,),
        in_specs=[
            const((N_META, D_MODEL)), const((1, D_MODEL)),
            group("k"), group("v"), group("u"), group("gate_c"), _forget_w_spec(),
            const((1, LANES)), const((SPLIT_TERMS * LANES, LANES)), const((1, LANES)),
        ],
        out_specs=[
            pl.BlockSpec((N_META, tn), lambda j: (0, j)),
            pl.BlockSpec((N_META, tn), lambda j: (0, j)),
            pl.BlockSpec((N_META, tn), lambda j: (0, j)),
            const((N_META, LANES)),
            const((N_META, LANES)),
        ],
        out_shape=[
            jax.ShapeDtypeStruct((N_META, D_MODEL), BF16),
            jax.ShapeDtypeStruct((N_META, D_MODEL), BF16),
            jax.ShapeDtypeStruct((N_META, D_MODEL), F32),
            jax.ShapeDtypeStruct((N_META, LANES), F32),
            jax.ShapeDtypeStruct((N_META, LANES), BF16),
        ],
        compiler_params=pltpu.CompilerParams(
            dimension_semantics=("arbitrary",), vmem_limit_bytes=VMEM_LIMIT),
        name="meta_proj",
    )(meta, gain, wt, wt, wt, wt, wt, bf, place, ones)


def _norm_kernel(x_ref, gain_ref, wf_ref, bf_ref, cm_ref, place_ref, ones_ref,
                 h_ref, gate_ref, carry_ref, *, chunk):
    @pl.when(pl.program_id(1) == 0)
    def _():
        carry_ref[...] = cm_ref[N_META - 1:N_META, :]

    h = _rmsnorm(x_ref[...], gain_ref[...]).astype(BF16)
    h_ref[...] = h
    logf = _log_sigmoid(_dot_nt(h, wf_ref[...].astype(BF16)) + bf_ref[...])
    tm = logf.shape[0]
    run = carry_ref[...]
    parts = []
    for j in range(tm // chunk):
        c = _cumsum_rows(logf[j * chunk:(j + 1) * chunk]) + run
        run = c[chunk - 1:chunk, :]
        parts.append(c)
    carry_ref[...] = run
    gate_ref[...] = _gate_features(jnp.concatenate(parts, axis=0), place_ref, ones_ref)


def _norm_call(x, gain, wt, bf, cm, place, ones):
    tm, chunk = TILES.norm_tm, TILES.norm_chunk
    b, s, _ = x.shape
    nt = s // tm
    const = lambda shape: pl.BlockSpec(shape, lambda bi, i: (0,) * len(shape))
    return pl.pallas_call(
        functools.partial(_norm_kernel, chunk=chunk),
        grid=(b, nt),
        in_specs=[
            pl.BlockSpec((None, tm, D_MODEL), lambda bi, i: (bi, i, 0)),
            const((1, D_MODEL)),
            _forget_w_spec(),
            const((1, LANES)),
            const((N_META, LANES)),
            const((SPLIT_TERMS * LANES, LANES)),
            const((1, LANES)),
        ],
        out_specs=[
            pl.BlockSpec((tm, D_MODEL), lambda bi, i: (bi * nt + i, 0)),
            pl.BlockSpec((tm, LANES), lambda bi, i: (bi * nt + i, 0)),
        ],
        out_shape=[
            jax.ShapeDtypeStruct((b * s, D_MODEL), BF16),
            jax.ShapeDtypeStruct((b * s, LANES), BF16),
        ],
        scratch_shapes=[pltpu.VMEM((1, LANES), F32)],
        compiler_params=pltpu.CompilerParams(
            dimension_semantics=("arbitrary", "arbitrary"), vmem_limit_bytes=VMEM_LIMIT),
        name="norm_forget",
    )(x, gain, wt, bf, cm, place, ones)


_HEAD_GROUPS = ("q", "k", "v", "z_att")
_GATE_GROUPS = ("m_att", "m_conv")


def _group_row(groups, g):
    row = jnp.int32(COL[groups[0]])
    for n in range(1, len(groups)):
        row = jnp.where(g == n, COL[groups[n]], row)
    return row


def _proj_kernel(h_ref, w_ref, o_ref, w_s, *, row_chunk, scale_first, head_major):
    @pl.when(pl.program_id(2) == 0)
    def _():
        _stage_weight(w_ref, w_s)

    scale = HEAD_DIM ** -0.5 * LOG2E if scale_first else 1.0
    scale = jnp.where(pl.program_id(0) == 0, scale, 1.0).astype(F32)
    for r in range(0, h_ref.shape[0], row_chunk):
        y = (_dot_nt(h_ref[r:r + row_chunk, :], w_s[...]) * scale).astype(BF16)
        if head_major:
            for hh in range(y.shape[1] // HEAD_DIM):
                o_ref[hh, r:r + row_chunk, :] = y[:, hh * HEAD_DIM:(hh + 1) * HEAD_DIM]
        else:
            o_ref[r:r + row_chunk, :] = y


def _proj_call(h, wt, groups, head_major):
    tm, tn, row_chunk = TILES.proj_tm, TILES.proj_tn, TILES.proj_rows
    r = h.shape[0]
    ng = len(groups)
    if head_major:
        out_spec = pl.BlockSpec((None, tn // HEAD_DIM, tm, HEAD_DIM), lambda g, j, i: (g, j, i, 0))
        out_shape = jax.ShapeDtypeStruct((ng, HEADS, r, HEAD_DIM), BF16)
    else:
        out_spec = pl.BlockSpec((None, tm, tn), lambda g, j, i: (g, i, j))
        out_shape = jax.ShapeDtypeStruct((ng, r, D_MODEL), BF16)
    return pl.pallas_call(
        functools.partial(_proj_kernel, row_chunk=row_chunk, scale_first=groups[0] == "q",
                          head_major=head_major),
        grid=(ng, D_MODEL // tn, r // tm),
        in_specs=[
            pl.BlockSpec((tm, D_MODEL), lambda g, j, i: (i, 0)),
            _w_tile_spec(tn, lambda g, j, i: _group_row(groups, g) + j * tn),
        ],
        out_specs=out_spec,
        out_shape=out_shape,
        scratch_shapes=[pltpu.VMEM((tn, D_MODEL), BF16)],
        compiler_params=pltpu.CompilerParams(
            dimension_semantics=("arbitrary", "arbitrary", "arbitrary"),
            vmem_limit_bytes=VMEM_LIMIT),
        name="head_proj" if head_major else "gate_proj",
    )(h, wt)


_CONV_GROUPS = ("u", "gate_b", "gate_c", "z_conv")


def _conv_kernel(h_ref, wu_ref, wgb_ref, wgc_ref, wzc_ref, gm_ref, cw_ref,
                 wa_ref, wc_ref, wo_ref, o_ref, wa_out, wc_out, wo_out,
                 w_s, gbuf_ref, *, tiles_per_batch, row_chunk):
    i = pl.program_id(1)
    tm = h_ref.shape[0]
    for src, dst in ((wa_ref, wa_out), (wc_ref, wc_out), (wo_ref, wo_out)):
        dst[...] = src[...].astype(BF16)

    @pl.when(i == 0)
    def _():
        for n, w_ref in enumerate((wu_ref, wgb_ref, wgc_ref, wzc_ref)):
            _stage_weight(w_ref, w_s.at[n])

    @pl.when(i % tiles_per_batch == 0)
    def _():
        gbuf_ref[0:SUBLANES, :] = gm_ref[N_META - SUBLANES:N_META, :]

    for r in range(0, tm, row_chunk):
        h = h_ref[r:r + row_chunk, :]
        g = _dot_nt(h, w_s[2]) * _dot_nt(h, w_s[0])
        first = SUBLANES + r
        gbuf_ref[first:first + row_chunk, :] = g
        conv = (gbuf_ref[first - 2:first - 2 + row_chunk, :] * cw_ref[0:1, :]
                + gbuf_ref[first - 1:first - 1 + row_chunk, :] * cw_ref[1:2, :]
                + g * cw_ref[2:3, :])
        gb = _dot_nt(h, w_s[1])
        zc = _dot_nt(h, w_s[3])
        o_ref[r:r + row_chunk, :] = (gb * conv * _silu(zc)).astype(BF16)
    gbuf_ref[0:SUBLANES, :] = gbuf_ref[tm:tm + SUBLANES, :]


def _conv_call(h, wt, g_meta, conv_w, out_weights, rows_per_batch):
    tm, tc, row_chunk = TILES.conv_tm, TILES.conv_tc, TILES.conv_rows
    r = h.shape[0]
    nrow = r // tm
    steps = (D_MODEL // tc) * nrow
    slab = D_MODEL // steps
    assert slab * steps == D_MODEL and slab % (2 * SUBLANES) == 0
    w_specs = [_w_tile_spec(tc, lambda j, i, name=name: COL[name] + j * tc)
               for name in _CONV_GROUPS]
    slab_in = pl.BlockSpec((None, slab, D_MODEL), lambda j, i: (0, j * nrow + i, 0))
    slab_out = pl.BlockSpec((slab, D_MODEL), lambda j, i: (j * nrow + i, 0))
    cast_shape = jax.ShapeDtypeStruct((D_MODEL, D_MODEL), BF16)
    return pl.pallas_call(
        functools.partial(_conv_kernel, tiles_per_batch=rows_per_batch // tm, row_chunk=row_chunk),
        grid=(D_MODEL // tc, nrow),
        in_specs=[pl.BlockSpec((tm, D_MODEL), lambda j, i: (i, 0))] + w_specs + [
            pl.BlockSpec((N_META, tc), lambda j, i: (0, j)),
            pl.BlockSpec((CONV_K, tc), lambda j, i: (0, j)),
            slab_in, slab_in, slab_in,
        ],
        out_specs=[pl.BlockSpec((tm, tc), lambda j, i: (i, j)), slab_out, slab_out, slab_out],
        out_shape=[jax.ShapeDtypeStruct((r, D_MODEL), BF16), cast_shape, cast_shape, cast_shape],
        scratch_shapes=[pltpu.VMEM((len(_CONV_GROUPS), tc, D_MODEL), BF16),
                        pltpu.VMEM((SUBLANES + tm, tc), F32)],
        compiler_params=pltpu.CompilerParams(
            dimension_semantics=("arbitrary", "arbitrary"), vmem_limit_bytes=VMEM_LIMIT),
        name="conv_proj",
    )(h, *([wt] * len(_CONV_GROUPS)), g_meta, conv_w, *out_weights)


def _attn_kernel(q_ref, k_ref, v_ref, z_ref, km_ref, vm_ref, g_ref, gm_ref, eqk_ref,
                 o_ref, qa_ref, kf_ref, va_ref, tri_ref, s_ref, tmax_ref, m_ref, acc_ref,
                 *, tq, tk, ts, rb):
    seq = q_ref.shape[0]
    nq = seq // tq
    assert tq == 2 * tk and seq % tq == 0
    half = tk // 2

    tri_ref[...] = jnp.where(
        lax.broadcasted_iota(jnp.int32, (tk, tk), 1) <= lax.broadcasted_iota(jnp.int32, (tk, tk), 0),
        0.0, NEG_BIG)
    kma = jnp.concatenate(
        [km_ref[...], _dot(gm_ref[...], eqk_ref[:, LANES:]).astype(BF16)], axis=1)
    pad_lane = lax.broadcasted_iota(jnp.int32, (LANES - N_META, 2 * LANES), 1)
    k_meta_blk = jnp.concatenate(
        [kma, jnp.where(pad_lane == LANES + SPLIT_TERMS, NEG_BIG, 0.0).astype(BF16)],
        axis=0)
    v_meta_blk = jnp.concatenate(
        [vm_ref[...], jnp.ones((N_META, LANES), BF16)], axis=1)
    v_meta_blk = jnp.concatenate(
        [v_meta_blk, jnp.zeros((LANES - N_META, 2 * LANES), BF16)], axis=0)

    def setup(row0):
        rows = slice(row0, row0 + ts)
        feat = _dot(g_ref[rows, :], eqk_ref[...]).astype(BF16)
        kf_ref[rows, :] = feat[:, LANES:]
        qa_ref[rows, :] = jnp.concatenate([q_ref[rows, :], feat[:, :LANES]], axis=1)
        va_ref[rows, :] = jnp.concatenate([v_ref[rows, :], jnp.ones((ts, LANES), BF16)], axis=1)
        m_ref[rows, :] = jnp.full((ts, LANES), NEG_BIG, F32)
        acc_ref[rows, :] = jnp.zeros((ts, 2 * LANES), F32)

    def width(key0, nkeys):
        return nkeys + LANES if key0 == 0 else nkeys

    def stage_a(row0, nrows, key0, nkeys, tri, slot):
        keys = slice(key0, key0 + nkeys)
        ka = jnp.concatenate([k_ref[keys, :], kf_ref[keys, :]], axis=1)
        if key0 == 0:
            ka = jnp.concatenate([ka, k_meta_blk], axis=0)
        s = _dot_nt(qa_ref[row0:row0 + nrows, :], ka)
        w = width(key0, nkeys)
        for r0 in range(0, nrows, rb):
            part = s[r0:r0 + rb]
            if tri:
                masked = part[:, 0:nkeys] + tri_ref[r0:r0 + rb, 0:nkeys]
                part = jnp.concatenate([masked, part[:, nkeys:]], axis=1) if w > nkeys else masked
            s_ref[slot, r0:r0 + rb, 0:w] = part
            t = part[:, 0:LANES]
            for c in range(1, w // LANES):
                t = jnp.maximum(t, part[:, c * LANES:(c + 1) * LANES])
            tmax_ref[slot, r0:r0 + rb, :] = jnp.broadcast_to(
                jnp.max(t, axis=1, keepdims=True), (rb, LANES))

    def stage_b(row0, nrows, key0, nkeys, tri, slot):
        del tri
        w = width(key0, nkeys)
        alphas, ps = [], []
        for r0 in range(0, nrows, rb):
            rows = slice(row0 + r0, row0 + r0 + rb)
            m_old = m_ref[rows, :]
            m_new = jnp.maximum(m_old, tmax_ref[slot, r0:r0 + rb, :])
            m_ref[rows, :] = m_new
            alphas.append(jnp.exp2(m_old - m_new))
            ps.append(jnp.concatenate(
                [jnp.exp2(s_ref[slot, r0:r0 + rb, c * LANES:(c + 1) * LANES] - m_new).astype(BF16)
                 for c in range(w // LANES)], axis=1))
        va = va_ref[key0:key0 + nkeys, :]
        if key0 == 0:
            va = jnp.concatenate([va, v_meta_blk], axis=0)
        pv = _dot(jnp.concatenate(ps, axis=0), va)
        for n, r0 in enumerate(range(0, nrows, rb)):
            rows = slice(row0 + r0, row0 + r0 + rb)
            alpha = jnp.concatenate([alphas[n], alphas[n]], axis=1)
            acc_ref[rows, :] = alpha * acc_ref[rows, :] + pv[r0:r0 + rb]

    def diagonal(row0):
        return [(row0, half, row0, half, True),
                (row0 + half, half, row0, half, False),
                (row0 + half, half, row0 + half, half, True)]

    items = []
    for i in range(nq):
        top = i * tq
        items += [(top, tq, j * tk, tk, False) for j in range(2 * i)]
        items += [(top + tk, tk, top, tk, False)]
        items += diagonal(top) + diagonal(top + tk)

    def finalize(row0, nrows):
        for r0 in range(row0, row0 + nrows, rb):
            o = (acc_ref[r0:r0 + rb, :LANES] / acc_ref[r0:r0 + rb, LANES:]
                 * _silu(z_ref[r0:r0 + rb, :].astype(F32)))
            o_ref[r0:r0 + rb, :] = o.astype(BF16)

    rows_ready = [0]

    def scores(item, slot):
        row0, nrows, key0, nkeys, _ = item
        while rows_ready[0] < max(row0 + nrows, key0 + nkeys):
            setup(rows_ready[0])
            rows_ready[0] += ts
        stage_a(*item, slot)

    scores(items[0], 0)
    for n, item in enumerate(items):
        if n + 1 < len(items):
            scores(items[n + 1], 1 - n % 2)
        stage_b(*item, n % 2)
        if item[4]:
            finalize(item[0], item[1])


def _attn_call(proj, k_meta, v_meta, gate, gate_meta, eqk, batch, seq):
    tq, tk, ts, rb = TILES.attn_tq, TILES.attn_tk, TILES.attn_ts, TILES.attn_rb
    r = batch * seq
    head = lambda g: pl.BlockSpec((None, None, seq, HEAD_DIM), lambda b, h, g=g: (g, h, b, 0))
    return pl.pallas_call(
        functools.partial(_attn_kernel, tq=tq, tk=tk, ts=ts, rb=rb),
        grid=(batch, HEADS),
        in_specs=[
            head(0), head(1), head(2), head(3),
            pl.BlockSpec((N_META, HEAD_DIM), lambda b, h: (0, h)),
            pl.BlockSpec((N_META, HEAD_DIM), lambda b, h: (0, h)),
            pl.BlockSpec((seq, LANES), lambda b, h: (b, 0)),
            pl.BlockSpec((N_META, LANES), lambda b, h: (0, 0)),
            pl.BlockSpec((None, LANES, 2 * LANES), lambda b, h: (h, 0, 0)),
        ],
        out_specs=pl.BlockSpec((seq, HEAD_DIM), lambda b, h: (b, h)),
        out_shape=jax.ShapeDtypeStruct((r, D_MODEL), BF16),
        scratch_shapes=[
            pltpu.VMEM((seq, 2 * LANES), BF16),
            pltpu.VMEM((seq, LANES), BF16),
            pltpu.VMEM((seq, 2 * LANES), BF16),
            pltpu.VMEM((tk, tk), F32),
            pltpu.VMEM((2, tq, tk + LANES), F32),
            pltpu.VMEM((2, tq, LANES), F32),
            pltpu.VMEM((seq, LANES), F32),
            pltpu.VMEM((seq, 2 * LANES), F32),
        ],
        compiler_params=pltpu.CompilerParams(
            dimension_semantics=("arbitrary", "arbitrary"), vmem_limit_bytes=VMEM_LIMIT),
        name="fox_attention",
    )(proj, proj, proj, proj, k_meta, v_meta, gate, gate_meta, eqk)


def _out_kernel(aa_ref, ac_ref, ma_ref, mc_ref, x_ref, wa_hbm, wc_hbm, wo_hbm, gain_ref, o_ref,
                w_s, sems):
    first = pl.program_id(0) == 0
    copies = [pltpu.make_async_copy(src, w_s.at[n], sems.at[n])
              for n, src in enumerate((wa_hbm, wc_hbm, wo_hbm))]

    @pl.when(first)
    def _():
        for copy in copies:
            copy.start()

    def step(wait):
        if wait:
            copies[0].wait()
        ya = _dot(aa_ref[...], w_s[0])
        if wait:
            copies[1].wait()
        yc = _dot(ac_ref[...], w_s[1])
        merged = (jax.nn.sigmoid(ma_ref[...].astype(F32)) * ya
                  + jax.nn.sigmoid(mc_ref[...].astype(F32)) * yc)
        if wait:
            copies[2].wait()
        y = x_ref[...] + _dot(merged.astype(BF16), w_s[2])
        o_ref[...] = _rmsnorm(y, gain_ref[...])

    pl.when(first)(lambda: step(True))
    pl.when(jnp.logical_not(first))(lambda: step(False))


def _out_call(a_att, a_conv, proj, x2d, wa, wc, wo, gain):
    tm = TILES.out_tm
    r = x2d.shape[0]
    row = lambda i: (i, 0)
    act = pl.BlockSpec((tm, D_MODEL), row)
    gate = lambda name: pl.BlockSpec((None, tm, D_MODEL),
                                     lambda i, g=_GATE_GROUPS.index(name): (g, i, 0))
    weight = pl.BlockSpec(memory_space=pl.ANY)
    return pl.pallas_call(
        _out_kernel,
        grid=(r // tm,),
        in_specs=[act, act, gate("m_att"), gate("m_conv"), act, weight, weight, weight,
                  pl.BlockSpec((1, D_MODEL), lambda i: (0, 0))],
        out_specs=act,
        out_shape=jax.ShapeDtypeStruct((r, D_MODEL), F32),
        scratch_shapes=[pltpu.VMEM((3, D_MODEL, D_MODEL), BF16), pltpu.SemaphoreType.DMA((3,))],
        compiler_params=pltpu.CompilerParams(
            dimension_semantics=("arbitrary",), vmem_limit_bytes=VMEM_LIMIT),
        name="out_proj",
    )(a_att, a_conv, proj, proj, x2d, wa, wc, wo, gain)


def kernel(x, meta_tokens, norm_gain, w_in, b_f, conv_w, w_att_o, w_conv_o, w_out, final_gain):
    batch, seq, _ = x.shape
    assert x.shape == (batch, seq, D_MODEL) and x.dtype == F32
    assert seq % max(TILES.attn_tq, TILES.conv_tm, TILES.norm_tm) == 0
    assert meta_tokens.shape == (N_META, D_MODEL) and b_f.shape == (1, HEADS)
    assert norm_gain.shape[0] == 1, "single-layer block"
    assert w_in.shape == (1, D_MODEL, N_IN) and w_in.dtype == F32
    wt = jnp.transpose(w_in[0])
    bf = jnp.pad(b_f[0].astype(F32), (0, LANES - HEADS)).reshape(1, LANES)
    gain = norm_gain[0].reshape(1, D_MODEL).astype(F32)
    fgain = final_gain.reshape(1, D_MODEL).astype(F32)
    place, ones, eqk = _feature_constants()

    k_meta, v_meta, g_meta, cm, gate_meta = _meta_call(
        meta_tokens.astype(F32), gain, wt, bf, place, ones)
    h, gate = _norm_call(x, gain, wt, bf, cm, place, ones)
    proj = _proj_call(h, wt, _HEAD_GROUPS, head_major=True)
    gates = _proj_call(h, wt, _GATE_GROUPS, head_major=False)
    a_conv, wa, wc, wo = _conv_call(h, wt, g_meta, conv_w[0].astype(F32),
                                    (w_att_o, w_conv_o, w_out), seq)
    a_att = _attn_call(proj, k_meta, v_meta, gate, gate_meta, eqk, batch, seq)
    out = _out_call(a_att, a_conv, gates, x.reshape(batch * seq, D_MODEL), wa, wc, wo, fgain)
    return out.reshape(batch, seq, D_MODEL)
```

```python
import functools
from typing import NamedTuple

import numpy as np

import jax
import jax.numpy as jnp
from jax import lax
from jax.experimental import pallas as pl
from jax.experimental.pallas import tpu as pltpu

D_MODEL = 2048
N_META = 16
HEADS = 16
HEAD_DIM = 128
CONV_K = 3
EPS = 1e-6
LANES = 128
SUBLANES = 8
NEG_BIG = -1e30
LOG2E = 1.4426950408889634
GATE_LANES = LANES // HEADS
SPLIT_TERMS = 3
VMEM_LIMIT = 56 * 1024 * 1024


class Tiles(NamedTuple):
    meta_tn: int = 512
    norm_tm: int = 1024
    norm_chunk: int = 256
    proj_tm: int = 2048
    proj_tn: int = 1024
    proj_rows: int = 512
    conv_tm: int = 2048
    conv_tc: int = 256
    conv_rows: int = 512
    attn_tq: int = 1024
    attn_tk: int = 512
    attn_ts: int = 512
    attn_rb: int = 128
    out_tm: int = 256


TILES = Tiles()

_GROUPS = ("q", "k", "v", "f", "z_att", "u", "gate_b", "gate_c", "z_conv", "m_att", "m_conv")
_WIDTHS = (D_MODEL, D_MODEL, D_MODEL, HEADS) + (D_MODEL,) * 7
COL = {name: sum(_WIDTHS[:n]) for n, name in enumerate(_GROUPS)}
N_IN = sum(_WIDTHS)

F32 = jnp.float32
BF16 = jnp.bfloat16


def _dot(a, b):
    return jnp.dot(a, b, preferred_element_type=F32)


def _dot_nt(a, b):
    return lax.dot_general(a, b, (((1,), (1,)), ((), ())), preferred_element_type=F32)


def _rmsnorm(x, gain):
    return x * lax.rsqrt(jnp.mean(x * x, axis=-1, keepdims=True) + EPS) * gain


def _log_sigmoid(x):
    return jnp.minimum(x, 0.0) - jnp.log1p(jnp.exp(-jnp.abs(x)))


def _silu(x):
    return x * jax.nn.sigmoid(x)


def _split3(x):
    hi = x.astype(BF16)
    r1 = x - hi.astype(F32)
    mid = r1.astype(BF16)
    lo = (r1 - mid.astype(F32)).astype(BF16)
    return hi, mid, lo


def _cumsum_rows(x):
    n = x.shape[0]
    tri = (lax.broadcasted_iota(jnp.int32, (n, n), 0)
           >= lax.broadcasted_iota(jnp.int32, (n, n), 1)).astype(BF16)
    hi, mid, lo = _split3(x)
    return _dot(tri, hi) + _dot(tri, mid) + _dot(tri, lo)


def _gate_features(c, place_ref, ones_ref):
    hi, mid, lo = _split3(c * LOG2E)
    cat = jnp.concatenate([hi, mid, lo], axis=1)
    return (_dot(cat, place_ref[...]) + ones_ref[...]).astype(BF16)


def _feature_constants():
    n = SPLIT_TERMS
    assert n + 1 <= GATE_LANES
    place = np.zeros((n * LANES, LANES), np.float32)
    ones = np.zeros((1, LANES), np.float32)
    eq = np.zeros((HEADS, LANES, LANES), np.float32)
    ek = np.zeros((HEADS, LANES, LANES), np.float32)
    for h in range(HEADS):
        base = GATE_LANES * h
        ones[0, base + n] = 1.0
        for f in range(n):
            place[f * LANES + h, base + f] = 1.0
            eq[h, base + f, f] = 1.0
            eq[h, base + n, n + f] = 1.0
            ek[h, base + n, f] = 1.0
            ek[h, base + f, n + f] = -1.0
    return (jnp.asarray(place, BF16), jnp.asarray(ones, F32),
            jnp.asarray(np.concatenate([eq, ek], axis=2), BF16))


def _w_tile_spec(tn, row_of):
    assert all(c % HEADS == 0 for c in COL.values()) and tn % HEADS == 0
    return pl.BlockSpec((pl.Element(tn), pl.Element(D_MODEL)),
                        lambda *ids: (pl.multiple_of(row_of(*ids), HEADS), 0))


def _forget_w_spec():
    assert COL["f"] % LANES == 0
    return pl.BlockSpec((LANES, D_MODEL), lambda *ids: (COL["f"] // LANES, 0))


def _stage_weight(src_ref, dst_ref, chunk=128):
    for r in range(0, src_ref.shape[0], chunk):
        dst_ref[r:r + chunk, :] = src_ref[r:r + chunk, :].astype(BF16)


def _meta_kernel(meta_ref, gain_ref, wk_ref, wv_ref, wu_ref, wgc_ref, wf_ref, bf_ref,
                 place_ref, ones_ref, k_ref, v_ref, g_ref, cm_ref, gm_ref):
    h = _rmsnorm(meta_ref[...], gain_ref[...]).astype(BF16)
    proj = lambda w_ref: _dot_nt(h, w_ref[...].astype(BF16))
    k_ref[...] = proj(wk_ref).astype(BF16)
    v_ref[...] = proj(wv_ref).astype(BF16)
    g_ref[...] = proj(wgc_ref) * proj(wu_ref)
    cm = _cumsum_rows(_log_sigmoid(proj(wf_ref) + bf_ref[...]))
    cm_ref[...] = cm
    gm_ref[...] = _gate_features(cm, place_ref, ones_ref)


def _meta_call(meta, gain, wt, bf, place, ones):
    tn = TILES.meta_tn
    nt = D_MODEL // tn
    const = lambda shape: pl.BlockSpec(shape, lambda j: (0,) * len(shape))
    group = lambda name: _w_tile_spec(tn, lambda j: COL[name] + j * tn)
    return pl.pallas_call(
        _meta_kernel,
        grid=(nt,),
        in_specs=[
            const((N_META, D_MODEL)), const((1, D_MODEL)),
            group("k"), group("v"), group("u"), group("gate_c"), _forget_w_spec(),
            const((1, LANES)), const((SPLIT_TERMS * LANES, LANES)), const((1, LANES)),
        ],
        out_specs=[
            pl.BlockSpec((N_META, tn), lambda j: (0, j)),
            pl.BlockSpec((N_META, tn), lambda j: (0, j)),
            pl.BlockSpec((N_META, tn), lambda j: (0, j)),
            const((N_META, LANES)),
            const((N_META, LANES)),
        ],
        out_shape=[
            jax.ShapeDtypeStruct((N_META, D_MODEL), BF16),
            jax.ShapeDtypeStruct((N_META, D_MODEL), BF16),
            jax.ShapeDtypeStruct((N_META, D_MODEL), F32),
            jax.ShapeDtypeStruct((N_META, LANES), F32),
            jax.ShapeDtypeStruct((N_META, LANES), BF16),
        ],
        compiler_params=pltpu.CompilerParams(
            dimension_semantics=("arbitrary",), vmem_limit_bytes=VMEM_LIMIT),
        name="meta_proj",
    )(meta, gain, wt, wt, wt, wt, wt, bf, place, ones)


def _norm_kernel(x_ref, gain_ref, wf_ref, bf_ref, cm_ref, place_ref, ones_ref,
                 h_ref, gate_ref, carry_ref, *, chunk):
    @pl.when(pl.program_id(1) == 0)
    def _():
        carry_ref[...] = cm_ref[N_META - 1:N_META, :]

    h = _rmsnorm(x_ref[...], gain_ref[...]).astype(BF16)
    h_ref[...] = h
    logf = _log_sigmoid(_dot_nt(h, wf_ref[...].astype(BF16)) + bf_ref[...])
    tm = logf.shape[0]
    run = carry_ref[...]
    parts = []
    for j in range(tm // chunk):
        c = _cumsum_rows(logf[j * chunk:(j + 1) * chunk]) + run
        run = c[chunk - 1:chunk, :]
        parts.append(c)
    carry_ref[...] = run
    gate_ref[...] = _gate_features(jnp.concatenate(parts, axis=0), place_ref, ones_ref)


def _norm_call(x, gain, wt, bf, cm, place, ones):
    tm, chunk = TILES.norm_tm, TILES.norm_chunk
    b, s, _ = x.shape
    nt = s // tm
    const = lambda shape: pl.BlockSpec(shape, lambda bi, i: (0,) * len(shape))
    return pl.pallas_call(
        functools.partial(_norm_kernel, chunk=chunk),
        grid=(b, nt),
        in_specs=[
            pl.BlockSpec((None, tm, D_MODEL), lambda bi, i: (bi, i, 0)),
            const((1, D_MODEL)),
            _forget_w_spec(),
            const((1, LANES)),
            const((N_META, LANES)),
            const((SPLIT_TERMS * LANES, LANES)),
            const((1, LANES)),
        ],
        out_specs=[
            pl.BlockSpec((tm, D_MODEL), lambda bi, i: (bi * nt + i, 0)),
            pl.BlockSpec((tm, LANES), lambda bi, i: (bi * nt + i, 0)),
        ],
        out_shape=[
            jax.ShapeDtypeStruct((b * s, D_MODEL), BF16),
            jax.ShapeDtypeStruct((b * s, LANES), BF16),
        ],
        scratch_shapes=[pltpu.VMEM((1, LANES), F32)],
        compiler_params=pltpu.CompilerParams(
            dimension_semantics=("arbitrary", "arbitrary"), vmem_limit_bytes=VMEM_LIMIT),
        name="norm_forget",
    )(x, gain, wt, bf, cm, place, ones)


_PROJ_GROUPS = ("q", "k", "v", "z_att", "m_att", "m_conv")


def _proj_group_row(g):
    row = jnp.int32(COL[_PROJ_GROUPS[0]])
    for n in range(1, len(_PROJ_GROUPS)):
        row = jnp.where(g == n, COL[_PROJ_GROUPS[n]], row)
    return row


def _proj_kernel(h_ref, w_ref, o_ref, w_s, *, row_chunk):
    @pl.when(pl.program_id(2) == 0)
    def _():
        _stage_weight(w_ref, w_s)

    scale = jnp.where(pl.program_id(0) == 0, HEAD_DIM ** -0.5 * LOG2E, 1.0).astype(F32)
    for r in range(0, h_ref.shape[0], row_chunk):
        y = _dot_nt(h_ref[r:r + row_chunk, :], w_s[...])
        o_ref[r:r + row_chunk, :] = (y * scale).astype(BF16)


def _proj_call(h, wt):
    tm, tn, row_chunk = TILES.proj_tm, TILES.proj_tn, TILES.proj_rows
    r = h.shape[0]
    ng = len(_PROJ_GROUPS)
    return pl.pallas_call(
        functools.partial(_proj_kernel, row_chunk=row_chunk),
        grid=(ng, D_MODEL // tn, r // tm),
        in_specs=[
            pl.BlockSpec((tm, D_MODEL), lambda g, j, i: (i, 0)),
            _w_tile_spec(tn, lambda g, j, i: _proj_group_row(g) + j * tn),
        ],
        out_specs=pl.BlockSpec((None, tm, tn), lambda g, j, i: (g, i, j)),
        out_shape=jax.ShapeDtypeStruct((ng, r, D_MODEL), BF16),
        scratch_shapes=[pltpu.VMEM((tn, D_MODEL), BF16)],
        compiler_params=pltpu.CompilerParams(
            dimension_semantics=("arbitrary", "arbitrary", "arbitrary"),
            vmem_limit_bytes=VMEM_LIMIT),
        name="plain_proj",
    )(h, wt)


_CONV_GROUPS = ("u", "gate_b", "gate_c", "z_conv")


def _conv_kernel(h_ref, wu_ref, wgb_ref, wgc_ref, wzc_ref, gm_ref, cw_ref,
                 wa_ref, wc_ref, wo_ref, o_ref, wa_out, wc_out, wo_out,
                 w_s, gbuf_ref, *, tiles_per_batch, row_chunk):
    i = pl.program_id(1)
    tm = h_ref.shape[0]
    for src, dst in ((wa_ref, wa_out), (wc_ref, wc_out), (wo_ref, wo_out)):
        dst[...] = src[...].astype(BF16)

    @pl.when(i == 0)
    def _():
        for n, w_ref in enumerate((wu_ref, wgb_ref, wgc_ref, wzc_ref)):
            _stage_weight(w_ref, w_s.at[n])

    @pl.when(i % tiles_per_batch == 0)
    def _():
        gbuf_ref[0:SUBLANES, :] = gm_ref[N_META - SUBLANES:N_META, :]

    for r in range(0, tm, row_chunk):
        h = h_ref[r:r + row_chunk, :]
        g = _dot_nt(h, w_s[2]) * _dot_nt(h, w_s[0])
        first = SUBLANES + r
        gbuf_ref[first:first + row_chunk, :] = g
        conv = (gbuf_ref[first - 2:first - 2 + row_chunk, :] * cw_ref[0:1, :]
                + gbuf_ref[first - 1:first - 1 + row_chunk, :] * cw_ref[1:2, :]
                + g * cw_ref[2:3, :])
        gb = _dot_nt(h, w_s[1])
        zc = _dot_nt(h, w_s[3])
        o_ref[r:r + row_chunk, :] = (gb * conv * _silu(zc)).astype(BF16)
    gbuf_ref[0:SUBLANES, :] = gbuf_ref[tm:tm + SUBLANES, :]


def _conv_call(h, wt, g_meta, conv_w, out_weights, rows_per_batch):
    tm, tc, row_chunk = TILES.conv_tm, TILES.conv_tc, TILES.conv_rows
    r = h.shape[0]
    nrow = r // tm
    steps = (D_MODEL // tc) * nrow
    slab = D_MODEL // steps
    assert slab * steps == D_MODEL and slab % (2 * SUBLANES) == 0
    w_specs = [_w_tile_spec(tc, lambda j, i, name=name: COL[name] + j * tc)
               for name in _CONV_GROUPS]
    slab_in = pl.BlockSpec((None, slab, D_MODEL), lambda j, i: (0, j * nrow + i, 0))
    slab_out = pl.BlockSpec((slab, D_MODEL), lambda j, i: (j * nrow + i, 0))
    cast_shape = jax.ShapeDtypeStruct((D_MODEL, D_MODEL), BF16)
    return pl.pallas_call(
        functools.partial(_conv_kernel, tiles_per_batch=rows_per_batch // tm, row_chunk=row_chunk),
        grid=(D_MODEL // tc, nrow),
        in_specs=[pl.BlockSpec((tm, D_MODEL), lambda j, i: (i, 0))] + w_specs + [
            pl.BlockSpec((N_META, tc), lambda j, i: (0, j)),
            pl.BlockSpec((CONV_K, tc), lambda j, i: (0, j)),
            slab_in, slab_in, slab_in,
        ],
        out_specs=[pl.BlockSpec((tm, tc), lambda j, i: (i, j)), slab_out, slab_out, slab_out],
        out_shape=[jax.ShapeDtypeStruct((r, D_MODEL), BF16), cast_shape, cast_shape, cast_shape],
        scratch_shapes=[pltpu.VMEM((len(_CONV_GROUPS), tc, D_MODEL), BF16),
                        pltpu.VMEM((SUBLANES + tm, tc), F32)],
        compiler_params=pltpu.CompilerParams(
            dimension_semantics=("arbitrary", "arbitrary"), vmem_limit_bytes=VMEM_LIMIT),
        name="conv_proj",
    )(h, *([wt] * len(_CONV_GROUPS)), g_meta, conv_w, *out_weights)


def _attn_kernel(q_ref, k_ref, v_ref, z_ref, km_ref, vm_ref, g_ref, gm_ref, eqk_ref,
                 o_ref, qa_ref, kf_ref, va_ref, tri_ref, s_ref, tmax_ref, m_ref, acc_ref,
                 *, tq, tk, ts, rb):
    seq = q_ref.shape[0]
    nq = seq // tq
    assert tq == 2 * tk and seq % tq == 0
    half = tk // 2

    tri_ref[...] = jnp.where(
        lax.broadcasted_iota(jnp.int32, (tk, tk), 1) <= lax.broadcasted_iota(jnp.int32, (tk, tk), 0),
        0.0, NEG_BIG)
    kma = jnp.concatenate(
        [km_ref[...], _dot(gm_ref[...], eqk_ref[:, LANES:]).astype(BF16)], axis=1)
    pad_lane = lax.broadcasted_iota(jnp.int32, (LANES - N_META, 2 * LANES), 1)
    k_meta_blk = jnp.concatenate(
        [kma, jnp.where(pad_lane == LANES + SPLIT_TERMS, NEG_BIG, 0.0).astype(BF16)],
        axis=0)
    v_meta_blk = jnp.concatenate(
        [vm_ref[...], jnp.ones((N_META, LANES), BF16)], axis=1)
    v_meta_blk = jnp.concatenate(
        [v_meta_blk, jnp.zeros((LANES - N_META, 2 * LANES), BF16)], axis=0)

    def setup(row0):
        rows = slice(row0, row0 + ts)
        feat = _dot(g_ref[rows, :], eqk_ref[...]).astype(BF16)
        kf_ref[rows, :] = feat[:, LANES:]
        qa_ref[rows, :] = jnp.concatenate([q_ref[rows, :], feat[:, :LANES]], axis=1)
        va_ref[rows, :] = jnp.concatenate([v_ref[rows, :], jnp.ones((ts, LANES), BF16)], axis=1)
        m_ref[rows, :] = jnp.full((ts, LANES), NEG_BIG, F32)
        acc_ref[rows, :] = jnp.zeros((ts, 2 * LANES), F32)

    def width(key0, nkeys):
        return nkeys + LANES if key0 == 0 else nkeys

    def stage_a(row0, nrows, key0, nkeys, tri, slot):
        keys = slice(key0, key0 + nkeys)
        ka = jnp.concatenate([k_ref[keys, :], kf_ref[keys, :]], axis=1)
        if key0 == 0:
            ka = jnp.concatenate([ka, k_meta_blk], axis=0)
        s = _dot_nt(qa_ref[row0:row0 + nrows, :], ka)
        w = width(key0, nkeys)
        for r0 in range(0, nrows, rb):
            part = s[r0:r0 + rb]
            if tri:
                masked = part[:, 0:nkeys] + tri_ref[r0:r0 + rb, 0:nkeys]
                part = jnp.concatenate([masked, part[:, nkeys:]], axis=1) if w > nkeys else masked
            s_ref[slot, r0:r0 + rb, 0:w] = part
            t = part[:, 0:LANES]
            for c in range(1, w // LANES):
                t = jnp.maximum(t, part[:, c * LANES:(c + 1) * LANES])
            tmax_ref[slot, r0:r0 + rb, :] = jnp.broadcast_to(
                jnp.max(t, axis=1, keepdims=True), (rb, LANES))

    def stage_b(row0, nrows, key0, nkeys, tri, slot):
        del tri
        w = width(key0, nkeys)
        alphas, ps = [], []
        for r0 in range(0, nrows, rb):
            rows = slice(row0 + r0, row0 + r0 + rb)
            m_old = m_ref[rows, :]
            m_new = jnp.maximum(m_old, tmax_ref[slot, r0:r0 + rb, :])
            m_ref[rows, :] = m_new
            alphas.append(jnp.exp2(m_old - m_new))
            ps.append(jnp.concatenate(
                [jnp.exp2(s_ref[slot, r0:r0 + rb, c * LANES:(c + 1) * LANES] - m_new).astype(BF16)
                 for c in range(w // LANES)], axis=1))
        va = va_ref[key0:key0 + nkeys, :]
        if key0 == 0:
            va = jnp.concatenate([va, v_meta_blk], axis=0)
        pv = _dot(jnp.concatenate(ps, axis=0), va)
        for n, r0 in enumerate(range(0, nrows, rb)):
            rows = slice(row0 + r0, row0 + r0 + rb)
            alpha = jnp.concatenate([alphas[n], alphas[n]], axis=1)
            acc_ref[rows, :] = alpha * acc_ref[rows, :] + pv[r0:r0 + rb]

    def diagonal(row0):
        return [(row0, half, row0, half, True),
                (row0 + half, half, row0, half, False),
                (row0 + half, half, row0 + half, half, True)]

    items = []
    for i in range(nq):
        top = i * tq
        items += [(top, tq, j * tk, tk, False) for j in range(2 * i)]
        items += [(top + tk, tk, top, tk, False)]
        items += diagonal(top) + diagonal(top + tk)

    def finalize(row0, nrows):
        for r0 in range(row0, row0 + nrows, rb):
            o = (acc_ref[r0:r0 + rb, :LANES] / acc_ref[r0:r0 + rb, LANES:]
                 * _silu(z_ref[r0:r0 + rb, :].astype(F32)))
            o_ref[r0:r0 + rb, :] = o.astype(BF16)

    rows_ready = [0]

    def scores(item, slot):
        row0, nrows, key0, nkeys, _ = item
        while rows_ready[0] < max(row0 + nrows, key0 + nkeys):
            setup(rows_ready[0])
            rows_ready[0] += ts
        stage_a(*item, slot)

    scores(items[0], 0)
    for n, item in enumerate(items):
        if n + 1 < len(items):
            scores(items[n + 1], 1 - n % 2)
        stage_b(*item, n % 2)
        if item[4]:
            finalize(item[0], item[1])


def _attn_call(proj, k_meta, v_meta, gate, gate_meta, eqk, batch, seq):
    tq, tk, ts, rb = TILES.attn_tq, TILES.attn_tk, TILES.attn_ts, TILES.attn_rb
    r = batch * seq
    head = lambda g: pl.BlockSpec((None, seq, HEAD_DIM), lambda b, h, g=g: (g, b, h))
    return pl.pallas_call(
        functools.partial(_attn_kernel, tq=tq, tk=tk, ts=ts, rb=rb),
        grid=(batch, HEADS),
        in_specs=[
            head(0), head(1), head(2), head(3),
            pl.BlockSpec((N_META, HEAD_DIM), lambda b, h: (0, h)),
            pl.BlockSpec((N_META, HEAD_DIM), lambda b, h: (0, h)),
            pl.BlockSpec((seq, LANES), lambda b, h: (b, 0)),
            pl.BlockSpec((N_META, LANES), lambda b, h: (0, 0)),
            pl.BlockSpec((None, LANES, 2 * LANES), lambda b, h: (h, 0, 0)),
        ],
        out_specs=pl.BlockSpec((seq, HEAD_DIM), lambda b, h: (b, h)),
        out_shape=jax.ShapeDtypeStruct((r, D_MODEL), BF16),
        scratch_shapes=[
            pltpu.VMEM((seq, 2 * LANES), BF16),
            pltpu.VMEM((seq, LANES), BF16),
            pltpu.VMEM((seq, 2 * LANES), BF16),
            pltpu.VMEM((tk, tk), F32),
            pltpu.VMEM((2, tq, tk + LANES), F32),
            pltpu.VMEM((2, tq, LANES), F32),
            pltpu.VMEM((seq, LANES), F32),
            pltpu.VMEM((seq, 2 * LANES), F32),
        ],
        compiler_params=pltpu.CompilerParams(
            dimension_semantics=("arbitrary", "arbitrary"), vmem_limit_bytes=VMEM_LIMIT),
        name="fox_attention",
    )(proj, proj, proj, proj, k_meta, v_meta, gate, gate_meta, eqk)


def _out_kernel(aa_ref, ac_ref, ma_ref, mc_ref, x_ref, wa_hbm, wc_hbm, wo_hbm, gain_ref, o_ref,
                w_s, sems):
    first = pl.program_id(0) == 0
    half = D_MODEL // 2
    copies = [pltpu.make_async_copy(wa_hbm.at[:, 0:half], w_s.at[0, :, 0:half], sems.at[0]),
              pltpu.make_async_copy(wc_hbm, w_s.at[1], sems.at[1]),
              pltpu.make_async_copy(wo_hbm, w_s.at[2], sems.at[2]),
              pltpu.make_async_copy(wa_hbm.at[:, half:], w_s.at[0, :, half:], sems.at[3])]

    @pl.when(first)
    def _():
        for n in (0, 3, 1, 2):
            copies[n].start()

    def step(wait):
        if wait:
            copies[0].wait()
            ya_lo = _dot(aa_ref[...], w_s[0, :, 0:half])
            copies[3].wait()
            ya = jnp.concatenate([ya_lo, _dot(aa_ref[...], w_s[0, :, half:])], axis=1)
        else:
            ya = _dot(aa_ref[...], w_s[0])
        if wait:
            copies[1].wait()
        yc = _dot(ac_ref[...], w_s[1])
        merged = (jax.nn.sigmoid(ma_ref[...].astype(F32)) * ya
                  + jax.nn.sigmoid(mc_ref[...].astype(F32)) * yc)
        if wait:
            copies[2].wait()
        y = x_ref[...] + _dot(merged.astype(BF16), w_s[2])
        o_ref[...] = _rmsnorm(y, gain_ref[...])

    pl.when(first)(lambda: step(True))
    pl.when(jnp.logical_not(first))(lambda: step(False))


def _out_call(a_att, a_conv, proj, x2d, wa, wc, wo, gain):
    tm = TILES.out_tm
    r = x2d.shape[0]
    row = lambda i: (i, 0)
    act = pl.BlockSpec((tm, D_MODEL), row)
    gate = lambda name: pl.BlockSpec((None, tm, D_MODEL),
                                     lambda i, g=_PROJ_GROUPS.index(name): (g, i, 0))
    weight = pl.BlockSpec(memory_space=pl.ANY)
    return pl.pallas_call(
        _out_kernel,
        grid=(r // tm,),
        in_specs=[act, act, gate("m_att"), gate("m_conv"), act, weight, weight, weight,
                  pl.BlockSpec((1, D_MODEL), lambda i: (0, 0))],
        out_specs=act,
        out_shape=jax.ShapeDtypeStruct((r, D_MODEL), F32),
        scratch_shapes=[pltpu.VMEM((3, D_MODEL, D_MODEL), BF16), pltpu.SemaphoreType.DMA((4,))],
        compiler_params=pltpu.CompilerParams(
            dimension_semantics=("arbitrary",), vmem_limit_bytes=VMEM_LIMIT),
        name="out_proj",
    )(a_att, a_conv, proj, proj, x2d, wa, wc, wo, gain)


def kernel(x, meta_tokens, norm_gain, w_in, b_f, conv_w, w_att_o, w_conv_o, w_out, final_gain):
    batch, seq, _ = x.shape
    assert x.shape == (batch, seq, D_MODEL) and x.dtype == F32
    assert seq % max(TILES.attn_tq, TILES.conv_tm, TILES.norm_tm) == 0
    assert meta_tokens.shape == (N_META, D_MODEL) and b_f.shape == (1, HEADS)
    assert norm_gain.shape[0] == 1, "single-layer block"
    assert w_in.shape == (1, D_MODEL, N_IN) and w_in.dtype == F32
    wt = jnp.transpose(w_in[0])
    bf = jnp.pad(b_f[0].astype(F32), (0, LANES - HEADS)).reshape(1, LANES)
    gain = norm_gain[0].reshape(1, D_MODEL).astype(F32)
    fgain = final_gain.reshape(1, D_MODEL).astype(F32)
    place, ones, eqk = _feature_constants()

    k_meta, v_meta, g_meta, cm, gate_meta = _meta_call(
        meta_tokens.astype(F32), gain, wt, bf, place, ones)
    h, gate = _norm_call(x, gain, wt, bf, cm, place, ones)
    proj = _proj_call(h, wt)
    a_conv, wa, wc, wo = _conv_call(h, wt, g_meta, conv_w[0].astype(F32),
                                    (w_att_o, w_conv_o, w_out), seq)
    a_att = _attn_call(proj, k_meta, v_meta, gate, gate_meta, eqk, batch, seq)
    out = _out_call(a_att, a_conv, proj, x.reshape(batch * seq, D_MODEL), wa, wc, wo, fgain)
    return out.reshape(batch, seq, D_MODEL)
```
